```python
import math
import jax, jax.numpy as jnp
from jax import lax
import numpy as np

D_MODEL = 1024
BATCH = 32
SEQ = 2048
DEPTH = 1

MEM_LEN = 256
D_MIX = D_MODEL
DIFF_WIDTH = D_MIX // 2
DIFF_HEADS = 4
DIFF_V_DIM = DIFF_WIDTH // DIFF_HEADS
DIFF_QK_DIM = DIFF_V_DIM // 2
MLSTM_WIDTH = D_MIX - DIFF_WIDTH
MLSTM_HEADS = 4
MLSTM_V_DIM = MLSTM_WIDTH // MLSTM_HEADS
MLSTM_QK_DIM = MLSTM_V_DIM // 2
CONV_K = 4
CHUNK = 128
Q_BLOCK = 128
IN_SIZES = (DIFF_HEADS * 2 * DIFF_QK_DIM, DIFF_HEADS * 2 * DIFF_QK_DIM, DIFF_HEADS * DIFF_V_DIM,
            2 * MLSTM_HEADS * MLSTM_QK_DIM, MLSTM_WIDTH, MLSTM_WIDTH, MLSTM_HEADS, MLSTM_HEADS)
D_IN = sum(IN_SIZES)
XATTN_HEADS = 4
XATTN_HEAD_DIM = D_MODEL // XATTN_HEADS
N_EXPERTS = 32
TOP_K = 4
D_EXPERT = D_MODEL
SWIGLU_ALPHA = 1.702
SWIGLU_LIMIT = 7.0
MOE_BLOCK = 512
RMS_EPS = 1e-5

kernel_name = 'hybrid_mlstm_diffattn_moe_block'


def rmsnorm(x, g):
    xf = x.astype(jnp.float32)
    y = xf * lax.rsqrt(jnp.mean(xf * xf, axis=-1, keepdims=True) + RMS_EPS)
    return (y * g.astype(jnp.float32)).astype(x.dtype)


def causal_dwconv_silu(u, w, b):
    s = u.shape[1]
    up = jnp.pad(u, ((0, 0), (CONV_K - 1, 0), (0, 0)))
    y = b + sum(up[:, j:j + s] * w[j] for j in range(CONV_K))
    return jax.nn.silu(y)


def diff_attention(q1, q2, k1, k2, v, lam):
    b, h, s, d = q1.shape
    nb = s // Q_BLOCK
    scale = d ** -0.5
    kpos = jnp.arange(s)

    def blocks(t):
        return jnp.moveaxis(t.reshape(b, h, nb, Q_BLOCK, d), 2, 0)

    def one_block(args):
        q1b, q2b, start = args
        qpos = start + jnp.arange(Q_BLOCK)
        mask = kpos[None, :] <= qpos[:, None]

        def probs(qb, k):
            sc = jnp.einsum('bhqd,bhkd->bhqk', qb, k).astype(jnp.float32) * scale
            return jax.nn.softmax(jnp.where(mask, sc, -jnp.inf), axis=-1)

        a = probs(q1b, k1) - lam * probs(q2b, k2)
        return jnp.einsum('bhqk,bhkv->bhqv', a.astype(v.dtype), v)

    out = lax.map(one_block, (blocks(q1), blocks(q2), jnp.arange(nb) * Q_BLOCK))
    return jnp.moveaxis(out, 0, 2).reshape(b, h, s, v.shape[-1])


def mlstm_chunkwise(q, k, v, ig, lf):
    b, h, s, dk = q.shape
    dv = v.shape[-1]
    nc = s // CHUNK

    def to_chunks(t):
        return jnp.moveaxis(t.reshape(t.shape[:2] + (nc, CHUNK) + t.shape[3:]), 2, 0)

    causal = jnp.tril(jnp.ones((CHUNK, CHUNK), dtype=bool))

    def step(carry, inp):
        c_st, n_st, m_st = carry
        q_, k_, v_, i_, f_ = inp
        bcum = jnp.cumsum(f_, axis=-1)
        dmat = jnp.where(causal, bcum[..., :, None] - bcum[..., None, :] + i_[..., None, :], -jnp.inf)
        inter = bcum + m_st[..., None]
        m_t = jnp.maximum(inter, jnp.max(dmat, axis=-1))
        w = jnp.einsum('bhtd,bhsd->bhts', q_, k_) * jnp.exp(dmat - m_t[..., None])
        g = jnp.exp(inter - m_t)
        num = g[..., None] * jnp.einsum('bhtd,bhde->bhte', q_, c_st) + jnp.einsum('bhts,bhse->bhte', w, v_)
        den = g * jnp.einsum('bhtd,bhd->bht', q_, n_st) + jnp.sum(w, axis=-1)
        h_t = num / jnp.maximum(jnp.abs(den), jnp.exp(-m_t))[..., None]
        b_last = bcum[..., -1]
        gs = b_last[..., None] - bcum + i_
        m_new = jnp.maximum(b_last + m_st, jnp.max(gs, axis=-1))
        decay = jnp.exp(b_last + m_st - m_new)
        ws = jnp.exp(gs - m_new[..., None])
        c_new = decay[..., None, None] * c_st + jnp.einsum('bhs,bhsd,bhse->bhde', ws, k_, v_)
        n_new = decay[..., None] * n_st + jnp.einsum('bhs,bhsd->bhd', ws, k_)
        return (c_new, n_new, m_new), h_t

    init = (jnp.zeros((b, h, dk, dv), jnp.float32), jnp.zeros((b, h, dk), jnp.float32),
            jnp.zeros((b, h), jnp.float32))
    _, hs = lax.scan(step, init, (to_chunks(q), to_chunks(k), to_chunks(v), to_chunks(ig), to_chunks(lf)))
    return jnp.moveaxis(hs, 0, 2).reshape(b, h, s, dv)


def hybrid_mixer(xn, w_in, conv_w, conv_b, b_igate, b_fgate, mlstm_norm_g,
                 lambda_q1, lambda_k1, lambda_q2, lambda_k2, diff_norm_g, w_out, lambda_init):
    b, s, _ = xn.shape
    f32 = jnp.float32
    proj = xn @ w_in
    cuts = np.cumsum(IN_SIZES)[:-1].tolist()
    dq, dk, dv, mqk, mv, mo, mi, mf = jnp.split(proj, cuts, axis=-1)

    def pair_heads(t):
        t = t.reshape(b, s, DIFF_HEADS, 2, DIFF_QK_DIM)
        return t[..., 0, :].transpose(0, 2, 1, 3), t[..., 1, :].transpose(0, 2, 1, 3)

    q1, q2 = pair_heads(dq)
    k1, k2 = pair_heads(dk)
    v = dv.reshape(b, s, DIFF_HEADS, DIFF_V_DIM).transpose(0, 2, 1, 3)
    lam = (jnp.exp(jnp.sum(lambda_q1.astype(f32) * lambda_k1.astype(f32)))
           - jnp.exp(jnp.sum(lambda_q2.astype(f32) * lambda_k2.astype(f32))) + lambda_init)
    od = diff_attention(q1, q2, k1, k2, v, lam)
    od = rmsnorm(od, diff_norm_g.reshape(DIFF_HEADS, 1, DIFF_V_DIM)) * (1.0 - lambda_init)
    od = od.transpose(0, 2, 1, 3).reshape(b, s, DIFF_WIDTH).astype(xn.dtype)

    qk = causal_dwconv_silu(mqk, conv_w, conv_b)
    mq, mk = jnp.split(qk, 2, axis=-1)

    def heads(t, d):
        return t.reshape(b, s, MLSTM_HEADS, d).transpose(0, 2, 1, 3).astype(f32)

    q = heads(mq, MLSTM_QK_DIM) * (MLSTM_QK_DIM ** -0.5)
    k = heads(mk, MLSTM_QK_DIM)
    vm = heads(mv, MLSTM_V_DIM)
    ig = (mi + b_igate).astype(f32).transpose(0, 2, 1)
    lf = jax.nn.log_sigmoid((mf + b_fgate).astype(f32)).transpose(0, 2, 1)
    hm = mlstm_chunkwise(q, k, vm, ig, lf)
    hm = rmsnorm(hm, mlstm_norm_g.reshape(MLSTM_HEADS, 1, MLSTM_V_DIM))
    hm = hm.transpose(0, 2, 1, 3).reshape(b, s, MLSTM_WIDTH)
    hm = (hm * jax.nn.sigmoid(mo.astype(f32))).astype(xn.dtype)

    return jnp.concatenate([od, hm], axis=-1) @ w_out


def cross_attention(xn, memn, w_cq, w_ckv, w_co):
    b, s, _ = xn.shape
    m = memn.shape[1]
    q = (xn @ w_cq).reshape(b, s, XATTN_HEADS, XATTN_HEAD_DIM)
    k, v = jnp.split(memn @ w_ckv, 2, axis=-1)
    k = k.reshape(b, m, XATTN_HEADS, XATTN_HEAD_DIM)
    v = v.reshape(b, m, XATTN_HEADS, XATTN_HEAD_DIM)
    sc = jnp.einsum('bqhd,bkhd->bhqk', q, k).astype(jnp.float32) * (XATTN_HEAD_DIM ** -0.5)
    p = jax.nn.softmax(sc, axis=-1)
    o = jnp.einsum('bhqk,bkhd->bqhd', p.astype(v.dtype), v).reshape(b, s, D_MODEL)
    return o @ w_co


def clamped_swiglu(hcat):
    hf = hcat.astype(jnp.float32)
    glu = jnp.minimum(hf[..., ::2], SWIGLU_LIMIT)
    lin = jnp.clip(hf[..., 1::2], -SWIGLU_LIMIT, SWIGLU_LIMIT)
    return (glu * jax.nn.sigmoid(SWIGLU_ALPHA * glu) * (lin + 1.0)).astype(hcat.dtype)


def moe_ffn(xn, w_router, b_router, w1, b1, w2, b2):
    b, s, d = xn.shape
    t = b * s
    a = t * TOP_K
    xf = xn.reshape(t, d)
    logits = xf.astype(jnp.float32) @ w_router.astype(jnp.float32) + b_router.astype(jnp.float32)
    top_v, top_e = lax.top_k(logits, TOP_K)
    gates = jax.nn.softmax(top_v, axis=-1)
    e_flat = top_e.reshape(a)
    order = jnp.argsort(e_flat)
    e_sorted = e_flat[order]
    tok_sorted = (order // TOP_K).astype(jnp.int32)
    gate_sorted = gates.reshape(a)[order]
    counts = jnp.bincount(e_flat, length=N_EXPERTS)
    padded = (counts + MOE_BLOCK - 1) // MOE_BLOCK * MOE_BLOCK
    start = jnp.cumsum(counts) - counts
    pend = jnp.cumsum(padded)
    pstart = pend - padded
    dest = pstart[e_sorted] + jnp.arange(a) - start[e_sorted]
    n_blocks = -(-a // MOE_BLOCK) + N_EXPERTS
    n_pad = n_blocks * MOE_BLOCK
    tok_pad = jnp.full((n_pad,), t, jnp.int32).at[dest].set(tok_sorted)
    gate_pad = jnp.zeros((n_pad,), jnp.float32).at[dest].set(gate_sorted)
    blk_expert = jnp.minimum(jnp.searchsorted(pend, jnp.arange(n_blocks) * MOE_BLOCK, side='right'),
                             N_EXPERTS - 1)
    x_ext = jnp.concatenate([xf, jnp.zeros((1, d), xf.dtype)], axis=0)

    def body(acc, inp):
        tok, g, e = inp
        hcat = x_ext[tok] @ w1[e] + b1[e]
        y = clamped_swiglu(hcat) @ w2[e] + b2[e]
        return acc.at[tok].add(y.astype(jnp.float32) * g[:, None]), None

    acc, _ = lax.scan(body, jnp.zeros((t + 1, d), jnp.float32),
                      (tok_pad.reshape(n_blocks, MOE_BLOCK), gate_pad.reshape(n_blocks, MOE_BLOCK), blk_expert))
    return acc[:t].reshape(b, s, d).astype(xn.dtype)


def setup_inputs(seed: int = 0) -> dict:
    key = jax.random.key(seed)
    ks = jax.random.split(key, 32)
    f32 = jnp.float32

    def nrm(k, shape, scale):
        return jax.random.normal(k, shape, f32) * scale

    def gain(k, n):
        return 1.0 + 0.02 * jax.random.normal(k, (DEPTH, n), f32)

    return {
        'x': nrm(ks[0], (BATCH, SEQ, D_MODEL), 1.0),
        'mem': nrm(ks[1], (BATCH, MEM_LEN, D_MODEL), 1.0),
        'norm_mix_g': gain(ks[2], D_MODEL),
        'w_in': nrm(ks[3], (DEPTH, D_MODEL, D_IN), D_MODEL ** -0.5),
        'conv_w': nrm(ks[4], (DEPTH, CONV_K, 2 * MLSTM_HEADS * MLSTM_QK_DIM), CONV_K ** -0.5),
        'conv_b': nrm(ks[5], (DEPTH, 2 * MLSTM_HEADS * MLSTM_QK_DIM), 0.02),
        'b_igate': nrm(ks[6], (DEPTH, MLSTM_HEADS), 0.1),
        'b_fgate': jnp.linspace(3.0, 6.0, MLSTM_HEADS, dtype=f32)[None, :] + nrm(ks[7], (DEPTH, MLSTM_HEADS), 0.1),
        'mlstm_norm_g': gain(ks[8], MLSTM_WIDTH),
        'lambda_q1': nrm(ks[9], (DEPTH, DIFF_QK_DIM), 0.1),
        'lambda_k1': nrm(ks[10], (DEPTH, DIFF_QK_DIM), 0.1),
        'lambda_q2': nrm(ks[11], (DEPTH, DIFF_QK_DIM), 0.1),
        'lambda_k2': nrm(ks[12], (DEPTH, DIFF_QK_DIM), 0.1),
        'diff_norm_g': gain(ks[13], DIFF_WIDTH),
        'w_out': nrm(ks[14], (DEPTH, D_MIX, D_MODEL), D_MIX ** -0.5),
        'norm_xattn_g': gain(ks[15], D_MODEL),
        'norm_mem_g': gain(ks[16], D_MODEL),
        'w_cq': nrm(ks[17], (DEPTH, D_MODEL, D_MODEL), D_MODEL ** -0.5),
        'w_ckv': nrm(ks[18], (DEPTH, D_MODEL, 2 * D_MODEL), D_MODEL ** -0.5),
        'w_co': nrm(ks[19], (DEPTH, D_MODEL, D_MODEL), D_MODEL ** -0.5),
        'norm_ffn_g': gain(ks[20], D_MODEL),
        'w_router': nrm(ks[21], (DEPTH, D_MODEL, N_EXPERTS), D_MODEL ** -0.5),
        'b_router': nrm(ks[22], (DEPTH, N_EXPERTS), 0.01),
        'w1': nrm(ks[23], (DEPTH, N_EXPERTS, D_MODEL, 2 * D_EXPERT), D_MODEL ** -0.5),
        'b1': nrm(ks[24], (DEPTH, N_EXPERTS, 2 * D_EXPERT), 0.01),
        'w2': nrm(ks[25], (DEPTH, N_EXPERTS, D_EXPERT, D_MODEL), D_EXPERT ** -0.5),
        'b2': nrm(ks[26], (DEPTH, N_EXPERTS, D_MODEL), 0.01),
        'norm_final_g': 1.0 + 0.02 * jax.random.normal(ks[27], (D_MODEL,), f32),
    }


def reference(x, mem, norm_mix_g, w_in, conv_w, conv_b, b_igate, b_fgate, mlstm_norm_g,
              lambda_q1, lambda_k1, lambda_q2, lambda_k2, diff_norm_g, w_out,
              norm_xattn_g, norm_mem_g, w_cq, w_ckv, w_co,
              norm_ffn_g, w_router, b_router, w1, b1, w2, b2, norm_final_g):
    h = x
    for l in range(DEPTH):
        lambda_init = 0.8 - 0.6 * math.exp(-0.3 * l)
        h = h + hybrid_mixer(rmsnorm(h, norm_mix_g[l]), w_in[l], conv_w[l], conv_b[l], b_igate[l], b_fgate[l],
                             mlstm_norm_g[l], lambda_q1[l], lambda_k1[l], lambda_q2[l], lambda_k2[l],
                             diff_norm_g[l], w_out[l], lambda_init).astype(h.dtype)
        h = h + cross_attention(rmsnorm(h, norm_xattn_g[l]), rmsnorm(mem, norm_mem_g[l]),
                                w_cq[l], w_ckv[l], w_co[l]).astype(h.dtype)
        h = h + moe_ffn(rmsnorm(h, norm_ffn_g[l]), w_router[l], b_router[l], w1[l], b1[l],
                        w2[l], b2[l]).astype(h.dtype)
    return rmsnorm(h, norm_final_g)
```

```python
import functools
import math

import jax
import jax.numpy as jnp
from jax import lax
from jax.experimental import pallas as pl
from jax.experimental.pallas import tpu as pltpu

F32 = jnp.float32
BF16 = jnp.bfloat16
I32 = jnp.int32

RMS_EPS = 1e-5
LANES = 128
VMEM_LIMIT = 56 * 1024 * 1024

DIFF_HEADS = 4
DIFF_QK_DIM = 64
MLSTM_HEADS = 4
MLSTM_QK_DIM = 64
CONV_K = 4
CHUNK = 128
XATTN_HEADS = 4
TOP_K = 4
SWIGLU_ALPHA = 1.702
SWIGLU_LIMIT = 7.0
MOE_BLOCK = 512
GROUP = 512
NEG_BIG = -1e30


def _rms(x, g):
    return x * lax.rsqrt(jnp.mean(x * x, axis=-1, keepdims=True) + RMS_EPS) * g


def _params(sem, vmem=VMEM_LIMIT):
    return pltpu.CompilerParams(dimension_semantics=sem, vmem_limit_bytes=vmem)


def _inproj_kernel(x_ref, g_ref, w_ref, wi_ref, wf_ref,
                   dq_ref, dk_ref, dv_ref, mqk_ref, mv_ref, mo_ref, gi_ref, gf_ref):
    xb = _rms(x_ref[...], g_ref[...]).astype(BF16)
    for n, o_ref in enumerate((dq_ref, dk_ref, dv_ref, mqk_ref, mv_ref, mo_ref)):
        o_ref[...] = jnp.dot(xb, w_ref[:, n * GROUP:(n + 1) * GROUP],
                             preferred_element_type=F32).astype(o_ref.dtype)
    gi_ref[...] = jnp.dot(xb, wi_ref[...], preferred_element_type=F32)
    gf_ref[...] = jnp.dot(xb, wf_ref[...], preferred_element_type=F32)


def _in_proj(x2, g, w_main, w_i, w_f, tm):
    t, d = x2.shape
    row = lambda i: (i, 0)
    const = lambda i: (0, 0)
    out_dtypes = (BF16, BF16, BF16, F32, BF16, F32)
    return pl.pallas_call(
        _inproj_kernel,
        grid=(t // tm,),
        in_specs=[pl.BlockSpec((tm, d), row), pl.BlockSpec((1, d), const),
                  pl.BlockSpec(w_main.shape, const), pl.BlockSpec(w_i.shape, const),
                  pl.BlockSpec(w_f.shape, const)],
        out_specs=[pl.BlockSpec((tm, GROUP), row)] * 6 + [pl.BlockSpec((tm, LANES), row)] * 2,
        out_shape=[jax.ShapeDtypeStruct((t, GROUP), dt) for dt in out_dtypes]
        + [jax.ShapeDtypeStruct((t, LANES), F32)] * 2,
        compiler_params=_params(("parallel",)),
        name="in_proj",
    )(x2, g, w_main, w_i, w_f)


def _diffattn_kernel(q_ref, k_ref, v_ref, lq1_ref, lk1_ref, lq2_ref, lk2_ref, g_ref, o_ref,
                     *, bq, lambda_init):
    i = pl.program_id(2)
    lane = lax.broadcasted_iota(I32, (1, LANES), 1)
    lo = lane < DIFF_QK_DIM
    q = q_ref[0] * jnp.asarray(DIFF_QK_DIM ** -0.5, BF16)
    zero = jnp.zeros_like(q)
    qs = (jnp.where(lo, q, zero), jnp.where(lo, zero, q))
    row = lax.broadcasted_iota(I32, (bq, bq), 0)
    col = lax.broadcasted_iota(I32, (bq, bq), 1)
    causal = col <= row

    def step(j, carry, masked):
        start = pl.multiple_of(j * bq, bq)
        kj = k_ref[0, pl.ds(start, bq), :]
        vj = v_ref[0, pl.ds(start, bq), :]
        new = []
        for qm, (m, l, a) in zip(qs, carry):
            s = lax.dot_general(qm, kj, (((1,), (1,)), ((), ())), preferred_element_type=F32)
            if masked:
                s = jnp.where(causal, s, -jnp.inf)
            mn = jnp.maximum(m, jnp.max(s, axis=-1, keepdims=True))
            p = jnp.exp(s - mn)
            alpha = jnp.exp(m - mn)
            l = alpha * l + jnp.sum(p, axis=-1, keepdims=True)
            a = alpha * a + jnp.dot(p.astype(BF16), vj, preferred_element_type=F32)
            new.append((mn, l, a))
        return tuple(new)

    init_one = (jnp.full((bq, 1), -jnp.inf, F32), jnp.zeros((bq, 1), F32), jnp.zeros((bq, LANES), F32))
    carry = lax.fori_loop(0, i, lambda j, c: step(j, c, False), (init_one, init_one))
    (_, l1, a1), (_, l2, a2) = step(i, carry, True)

    lam = (jnp.exp(jnp.sum(lq1_ref[...] * lk1_ref[...], axis=-1, keepdims=True))
           - jnp.exp(jnp.sum(lq2_ref[...] * lk2_ref[...], axis=-1, keepdims=True)) + lambda_init)
    od = a1 / l1 - lam * (a2 / l2)
    o_ref[0] = (_rms(od, g_ref[...]) * (1.0 - lambda_init)).astype(o_ref.dtype)


def _diff_attention(dq, dk, dv, lq1, lk1, lq2, lk2, g, lambda_init, bq):
    b, s, w = dq.shape
    qmap = lambda bi, h, i: (bi, i, h)
    kvmap = lambda bi, h, i: (bi, 0, h)
    vec = lambda bi, h, i: (0, 0)
    return pl.pallas_call(
        functools.partial(_diffattn_kernel, bq=bq, lambda_init=lambda_init),
        grid=(b, DIFF_HEADS, s // bq),
        in_specs=[pl.BlockSpec((1, bq, LANES), qmap), pl.BlockSpec((1, s, LANES), kvmap),
                  pl.BlockSpec((1, s, LANES), kvmap)]
        + [pl.BlockSpec((1, DIFF_QK_DIM), vec)] * 4
        + [pl.BlockSpec((1, LANES), lambda bi, h, i: (0, h))],
        out_specs=pl.BlockSpec((1, bq, LANES), qmap),
        out_shape=jax.ShapeDtypeStruct((b, s, w), BF16),
        compiler_params=_params(("parallel", "parallel", "arbitrary")),
        name="diff_attn",
    )(dq, dk, dv, lq1, lk1, lq2, lk2, g)


def _log_sigmoid(x):
    return -(jnp.maximum(-x, 0.0) + jnp.log1p(jnp.exp(-jnp.abs(x))))


def _mlstm_kernel(mqk_ref, mv_ref, gi_ref, gf_ref, mo_ref, cw_ref, cb_ref, bi_ref, bf_ref, ng_ref,
                  o_ref, conv_scr, c_scr, m_scr, *, bb):
    c = pl.program_id(1)
    L = CHUNK
    dk = MLSTM_QK_DIM
    nq = MLSTM_HEADS * dk
    tail = 8

    @pl.when(c == 0)
    def _():
        conv_scr[:, 0:tail, :] = jnp.zeros((bb, tail, 2 * nq), F32)
        c_scr[...] = jnp.zeros_like(c_scr)
        m_scr[...] = jnp.zeros_like(m_scr)

    row = lax.broadcasted_iota(I32, (L, L), 0)
    col = lax.broadcasted_iota(I32, (L, L), 1)
    causal = col <= row
    tril = causal.astype(F32)
    lane = lax.broadcasted_iota(I32, (1, LANES), 1)
    sub = lax.broadcasted_iota(I32, (LANES, 1), 0)
    ones_col = jnp.where(lane == 0, 1.0, 0.0).astype(BF16) * jnp.ones((L, 1), BF16)

    for b in range(bb):
        conv_scr[b, tail:tail + L, :] = mqk_ref[b]
        y = cb_ref[...]
        for j in range(CONV_K):
            y = y + conv_scr[b, pl.ds(tail - (CONV_K - 1) + j, L), :] * cw_ref[j:j + 1, :]
        conv_scr[b, 0:tail, :] = conv_scr[b, L:L + tail, :]
        qk = y * jax.nn.sigmoid(y)

        ig = gi_ref[b] + bi_ref[...]
        lf = _log_sigmoid(gf_ref[b] + bf_ref[...])
        bcum = jnp.dot(tril, lf, preferred_element_type=F32, precision=lax.Precision.HIGHEST)
        a_t = (ig - bcum).T

        for pair in range(MLSTM_HEADS // 2):
            q_pair = qk[:, pair * LANES:(pair + 1) * LANES] * (dk ** -0.5)
            k_pair = qk[:, nq + pair * LANES:nq + (pair + 1) * LANES]
            kt_pair = k_pair.T
            c_old = c_scr[b, pair]
            c_bf = c_old.astype(BF16)
            upd = jnp.zeros_like(c_old)
            decays = []
            for hh in range(2):
                h = 2 * pair + hh
                sel = (lane >= hh * dk) & (lane < (hh + 1) * dk)
                qm = jnp.where(sel, q_pair, 0.0).astype(BF16)
                km = jnp.where(sel, k_pair, 0.0).astype(BF16)
                selr = (sub >= hh * dk) & (sub < (hh + 1) * dk)
                ktm = jnp.where(selr, kt_pair, 0.0).astype(BF16)
                v_h = mv_ref[b, :, h * LANES:(h + 1) * LANES]
                v_aug = jnp.concatenate([v_h, ones_col], axis=1)

                m_st = m_scr[b, h, 0:1, 0:1]
                bc = bcum[:, h:h + 1]
                ic = ig[:, h:h + 1]
                a_m = jnp.where(causal, a_t[h:h + 1, :], -jnp.inf)
                inter = bc + m_st
                m_t = jnp.maximum(inter, bc + jnp.max(a_m, axis=-1, keepdims=True))
                qkt = lax.dot_general(qm, km, (((1,), (1,)), ((), ())), preferred_element_type=F32)
                w = qkt * jnp.exp(a_m + (bc - m_t))
                g = jnp.exp(inter - m_t)
                num_aug = (g * jnp.dot(qm, c_bf, preferred_element_type=F32)
                           + jnp.dot(w.astype(BF16), v_aug, preferred_element_type=F32))
                num = num_aug[:, :LANES]
                den = num_aug[:, LANES:LANES + 1]
                h_t = num / jnp.maximum(jnp.abs(den), jnp.exp(-m_t))

                b_last = bc[L - 1:L, :]
                gs = b_last - bc + ic
                m_new = jnp.maximum(b_last + m_st, jnp.max(gs, axis=0, keepdims=True))
                decays.append(jnp.exp(b_last + m_st - m_new))
                ws = jnp.exp(gs - m_new)
                wsv = (ws * v_aug.astype(F32)).astype(BF16)
                upd = upd + jnp.dot(ktm, wsv, preferred_element_type=F32)
                m_scr[b, h, 0:1, :] = jnp.broadcast_to(m_new, (1, LANES))

                hn = _rms(h_t, ng_ref[:, h * LANES:(h + 1) * LANES])
                gate = jax.nn.sigmoid(mo_ref[b, :, h * LANES:(h + 1) * LANES])
                o_ref[b, :, h * LANES:(h + 1) * LANES] = (hn * gate).astype(o_ref.dtype)
            d_rows = jnp.where(sub < dk, decays[0], decays[1])
            c_scr[b, pair] = d_rows * c_old + upd


def _mlstm(mqk, mv, gi, gf, mo, conv_w, conv_b, b_i, b_f, norm_g, bb):
    b, s, w = mqk.shape
    blk = lambda bi, c: (bi, c, 0)
    const = lambda bi, c: (0, 0)
    return pl.pallas_call(
        functools.partial(_mlstm_kernel, bb=bb),
        grid=(b // bb, s // CHUNK),
        in_specs=[pl.BlockSpec((bb, CHUNK, w), blk), pl.BlockSpec((bb, CHUNK, w), blk),
                  pl.BlockSpec((bb, CHUNK, LANES), blk), pl.BlockSpec((bb, CHUNK, LANES), blk),
                  pl.BlockSpec((bb, CHUNK, w), blk),
                  pl.BlockSpec(conv_w.shape, const), pl.BlockSpec(conv_b.shape, const),
                  pl.BlockSpec(b_i.shape, const), pl.BlockSpec(b_f.shape, const),
                  pl.BlockSpec(norm_g.shape, const)],
        out_specs=pl.BlockSpec((bb, CHUNK, w), blk),
        out_shape=jax.ShapeDtypeStruct((b, s, w), BF16),
        scratch_shapes=[pltpu.VMEM((bb, CHUNK + 8, w), F32),
                        pltpu.VMEM((bb, MLSTM_HEADS // 2, LANES, 2 * LANES), F32),
                        pltpu.VMEM((bb, MLSTM_HEADS, 8, LANES), F32)],
        compiler_params=_params(("parallel", "arbitrary")),
        name="mlstm",
    )(mqk, mv, gi, gf, mo, conv_w, conv_b, b_i, b_f, norm_g)


def _kvproj_kernel(mem_ref, g_ref, w_ref, k_ref, v_ref):
    d = mem_ref.shape[-1]
    mb = _rms(mem_ref[0], g_ref[...]).astype(BF16)
    k_ref[0] = jnp.dot(mb, w_ref[:, :d], preferred_element_type=F32).astype(k_ref.dtype)
    v_ref[0] = jnp.dot(mb, w_ref[:, d:], preferred_element_type=F32).astype(v_ref.dtype)


def _kv_proj(mem, g, w_ckv):
    b, m, d = mem.shape
    blk = lambda bi: (bi, 0, 0)
    const = lambda bi: (0, 0)
    return pl.pallas_call(
        _kvproj_kernel,
        grid=(b,),
        in_specs=[pl.BlockSpec((1, m, d), blk), pl.BlockSpec((1, d), const),
                  pl.BlockSpec(w_ckv.shape, const)],
        out_specs=[pl.BlockSpec((1, m, d), blk)] * 2,
        out_shape=[jax.ShapeDtypeStruct((b, m, d), BF16)] * 2,
        compiler_params=_params(("parallel",)),
        name="kv_proj",
    )(mem, g, w_ckv)


def _post_kernel(od_ref, hm_ref, x_ref, k_ref, v_ref, wo1_ref, wo2_ref, g2_ref, wcq_ref, wco_ref,
                 g3_ref, wr_ref, br_ref, h2_ref, xn3_ref, te_ref, tg_ref):
    d = x_ref.shape[-1]
    hd = d // XATTN_HEADS
    h1 = (x_ref[0] + jnp.dot(od_ref[0], wo1_ref[...], preferred_element_type=F32)
          + jnp.dot(hm_ref[0], wo2_ref[...], preferred_element_type=F32))
    q = jnp.dot(_rms(h1, g2_ref[...]).astype(BF16), wcq_ref[...], preferred_element_type=F32)
    q = (q * (hd ** -0.5)).astype(BF16)
    heads = []
    for h in range(XATTN_HEADS):
        sl = slice(h * hd, (h + 1) * hd)
        s = lax.dot_general(q[:, sl], k_ref[0, :, sl], (((1,), (1,)), ((), ())),
                            preferred_element_type=F32)
        e = jnp.exp(s - jnp.max(s, axis=-1, keepdims=True))
        p = e / jnp.sum(e, axis=-1, keepdims=True)
        heads.append(jnp.dot(p.astype(BF16), v_ref[0, :, sl], preferred_element_type=F32))
    o = jnp.concatenate(heads, axis=1).astype(BF16)
    h2 = h1 + jnp.dot(o, wco_ref[...], preferred_element_type=F32)
    h2_ref[0] = h2
    xn3 = _rms(h2, g3_ref[...])
    xn3_ref[0] = xn3
    logits = jnp.dot(xn3, wr_ref[...], preferred_element_type=F32,
                     precision=lax.Precision.HIGHEST) + br_ref[...]
    lane = lax.broadcasted_iota(I32, logits.shape, 1)
    lane_f = lane.astype(F32)
    cur = logits
    te = jnp.zeros(logits.shape, F32)
    vals = []
    for k in range(TOP_K):
        m = jnp.max(cur, axis=-1, keepdims=True)
        idx = jnp.min(jnp.where(cur == m, lane_f, float(LANES)), axis=-1, keepdims=True)
        vals.append(m)
        te = jnp.where(lane == k, idx, te)
        cur = jnp.where(lane_f == idx, NEG_BIG * 2.0, cur)
    es = [jnp.exp(v - vals[0]) for v in vals]
    tot = es[0] + es[1] + es[2] + es[3]
    tg = jnp.zeros(logits.shape, F32)
    for k in range(TOP_K):
        tg = jnp.where(lane == k, es[k] / tot, tg)
    te_ref[0] = te.astype(I32)
    tg_ref[0] = tg


def _post(od, hm, x, kmem, vmem, wo1, wo2, g2, wcq, wco, g3, wr, br, tm):
    b, s, d = x.shape
    w = od.shape[-1]
    m = kmem.shape[1]
    blk = lambda bi, i: (bi, i, 0)
    mem = lambda bi, i: (bi, 0, 0)
    const = lambda bi, i: (0, 0)
    full = lambda a: pl.BlockSpec(a.shape, const)
    return pl.pallas_call(
        _post_kernel,
        grid=(b, s // tm),
        in_specs=[pl.BlockSpec((1, tm, w), blk), pl.BlockSpec((1, tm, w), blk),
                  pl.BlockSpec((1, tm, d), blk), pl.BlockSpec((1, m, d), mem),
                  pl.BlockSpec((1, m, d), mem), full(wo1), full(wo2), full(g2), full(wcq),
                  full(wco), full(g3), full(wr), full(br)],
        out_specs=[pl.BlockSpec((1, tm, d), blk), pl.BlockSpec((1, tm, d), blk),
                   pl.BlockSpec((1, tm, LANES), blk), pl.BlockSpec((1, tm, LANES), blk)],
        out_shape=[jax.ShapeDtypeStruct((b, s, d), F32), jax.ShapeDtypeStruct((b, s, d), F32),
                   jax.ShapeDtypeStruct((b, s, LANES), I32), jax.ShapeDtypeStruct((b, s, LANES), F32)],
        compiler_params=_params(("parallel", "arbitrary")),
        name="post_mixer",
    )(od, hm, x, kmem, vmem, wo1, wo2, g2, wcq, wco, g3, wr, br)


def _expert_kernel(be_ref, na_ref, tok_ref, x_hbm, w1g_ref, w1l_ref, b1g_ref, b1l_ref, w2_ref, b2_ref,
                   y_ref, xbuf, sem, *, bm):
    i = pl.program_id(0)

    @pl.when(i < na_ref[0])
    def _():
        def issue(r, carry):
            tok = tok_ref[0, 0, r]
            pltpu.make_async_copy(x_hbm.at[pl.ds(tok, 1), :], xbuf.at[pl.ds(r, 1), :], sem).start()
            return carry
        lax.fori_loop(0, bm, issue, 0)
        pltpu.make_async_copy(x_hbm.at[pl.ds(0, bm), :], xbuf, sem).wait()

        xb = xbuf[...].astype(BF16)
        glu = jnp.dot(xb, w1g_ref[0], preferred_element_type=F32) + b1g_ref[0]
        lin = jnp.dot(xb, w1l_ref[0], preferred_element_type=F32) + b1l_ref[0]
        glu = jnp.minimum(glu, SWIGLU_LIMIT)
        lin = jnp.clip(lin, -SWIGLU_LIMIT, SWIGLU_LIMIT)
        act = glu * jax.nn.sigmoid(SWIGLU_ALPHA * glu) * (lin + 1.0)
        y_ref[...] = jnp.dot(act.astype(BF16), w2_ref[0], preferred_element_type=F32) + b2_ref[0]

    @pl.when(i >= na_ref[0])
    def _():
        y_ref[...] = jnp.zeros_like(y_ref)


def _experts(blk_expert, n_active, tok_blocks, xn3, w1g, w1l, b1g, b1l, w2, b2):
    n_blocks, _, bm = tok_blocks.shape
    t, d = xn3.shape
    de = w1g.shape[-1]
    last = lambda i, na: jnp.minimum(i, na[0] - 1)
    wmap = lambda i, be, na: (be[last(i, na)], 0, 0)
    grid_spec = pltpu.PrefetchScalarGridSpec(
        num_scalar_prefetch=2,
        grid=(n_blocks,),
        in_specs=[pl.BlockSpec((1, 1, bm), lambda i, be, na: (last(i, na), 0, 0), memory_space=pltpu.SMEM),
                  pl.BlockSpec(memory_space=pl.ANY),
                  pl.BlockSpec((1, d, de), wmap), pl.BlockSpec((1, d, de), wmap),
                  pl.BlockSpec((1, 1, de), wmap), pl.BlockSpec((1, 1, de), wmap),
                  pl.BlockSpec((1, de, d), wmap), pl.BlockSpec((1, 1, d), wmap)],
        out_specs=pl.BlockSpec((bm, d), lambda i, be, na: (i, 0)),
        scratch_shapes=[pltpu.VMEM((bm, d), F32), pltpu.SemaphoreType.DMA],
    )
    return pl.pallas_call(
        functools.partial(_expert_kernel, bm=bm),
        grid_spec=grid_spec,
        out_shape=jax.ShapeDtypeStruct((n_blocks * bm, d), F32),
        compiler_params=_params(("arbitrary",)),
        name="experts",
    )(blk_expert, n_active, tok_blocks, xn3, w1g, w1l, b1g, b1l, w2, b2)


def _combine_kernel(dest_ref, y_hbm, h2_ref, tg_ref, g_ref, o_ref, buf, sem, *, tb, final_norm):
    def issue(t, carry):
        for k in range(TOP_K):
            dst = dest_ref[0, 0, t * TOP_K + k]
            pltpu.make_async_copy(y_hbm.at[pl.ds(dst, 1), :], buf.at[k, pl.ds(t, 1), :], sem).start()
        return carry
    lax.fori_loop(0, tb, issue, 0)
    for k in range(TOP_K):
        pltpu.make_async_copy(y_hbm.at[pl.ds(0, tb), :], buf.at[k], sem).wait()

    acc = h2_ref[...]
    for k in range(TOP_K):
        acc = acc + buf[k] * tg_ref[:, k:k + 1]
    o_ref[...] = _rms(acc, g_ref[...]) if final_norm else acc


def _combine(dest_blocks, y, h2, tg, g, tb, final_norm):
    t, d = h2.shape
    row = lambda i: (i, 0)
    return pl.pallas_call(
        functools.partial(_combine_kernel, tb=tb, final_norm=final_norm),
        grid=(t // tb,),
        in_specs=[pl.BlockSpec((1, 1, tb * TOP_K), lambda i: (i, 0, 0), memory_space=pltpu.SMEM),
                  pl.BlockSpec(memory_space=pl.ANY),
                  pl.BlockSpec((tb, d), row), pl.BlockSpec((tb, LANES), row),
                  pl.BlockSpec((1, d), lambda i: (0, 0))],
        out_specs=pl.BlockSpec((tb, d), row),
        out_shape=jax.ShapeDtypeStruct((t, d), F32),
        scratch_shapes=[pltpu.VMEM((TOP_K, tb, d), F32), pltpu.SemaphoreType.DMA],
        compiler_params=_params(("arbitrary",)),
        name="combine",
    )(dest_blocks, y, h2, tg, g)


def _route(top_e, n_experts, bm):
    t = top_e.shape[0]
    a = t * TOP_K
    e_flat = top_e.reshape(a)
    order = jnp.argsort(e_flat)
    e_sorted = e_flat[order]
    counts = jnp.bincount(e_flat, length=n_experts)
    padded = (counts + bm - 1) // bm * bm
    start = jnp.cumsum(counts) - counts
    pend = jnp.cumsum(padded)
    pstart = pend - padded
    dest_sorted = (pstart[e_sorted] + jnp.arange(a) - start[e_sorted]).astype(I32)
    n_blocks = -(-a // bm) + n_experts
    tok_pad = jnp.zeros((n_blocks * bm,), I32).at[dest_sorted].set((order // TOP_K).astype(I32))
    dest = jnp.zeros((a,), I32).at[order].set(dest_sorted)
    blk_expert = jnp.minimum(jnp.searchsorted(pend, jnp.arange(n_blocks) * bm, side='right'),
                             n_experts - 1).astype(I32)
    n_active = (pend[-1] // bm).astype(I32).reshape(1)
    return tok_pad.reshape(n_blocks, 1, bm), dest, blk_expert, n_active


def _pad_cols(w, n):
    return jnp.pad(w, ((0, 0), (0, n - w.shape[1])))


def _layer(h, mem, l, p):
    b, s, d = h.shape
    t = b * s
    lambda_init = 0.8 - 0.6 * math.exp(-0.3 * l)
    n_main = 6 * GROUP
    w_in = p['w_in']
    w_main = w_in[:, :n_main].astype(BF16)
    w_i = _pad_cols(w_in[:, n_main:n_main + MLSTM_HEADS], LANES).astype(BF16)
    w_f = _pad_cols(w_in[:, n_main + MLSTM_HEADS:], LANES).astype(BF16)
    dq, dk, dv, mqk, mv, mo, gi, gf = _in_proj(h.reshape(t, d), p['norm_mix_g'].reshape(1, d),
                                               w_main, w_i, w_f, tm=min(512, t))
    r3 = lambda a: a.reshape(b, s, a.shape[-1])
    vec = lambda a: a.reshape(1, -1)
    od = _diff_attention(r3(dq), r3(dk), r3(dv), vec(p['lambda_q1']), vec(p['lambda_k1']),
                         vec(p['lambda_q2']), vec(p['lambda_k2']), vec(p['diff_norm_g']),
                         lambda_init, bq=min(512, s))
    hm = _mlstm(r3(mqk), r3(mv), r3(gi), r3(gf), r3(mo), p['conv_w'], vec(p['conv_b']),
                _pad_cols(vec(p['b_igate']), LANES), _pad_cols(vec(p['b_fgate']), LANES),
                vec(p['mlstm_norm_g']), bb=2 if b % 2 == 0 else 1)
    kmem, vmem = _kv_proj(mem, vec(p['norm_mem_g']), p['w_ckv'].astype(BF16))
    w_out = p['w_out'].astype(BF16)
    n_experts = p['w_router'].shape[1]
    wr = _pad_cols(p['w_router'], LANES)
    br = jnp.concatenate([vec(p['b_router']), jnp.full((1, LANES - n_experts), NEG_BIG, F32)], axis=1)
    h2, xn3, te, tg = _post(od, hm, h, kmem, vmem, w_out[:GROUP], w_out[GROUP:],
                            vec(p['norm_xattn_g']), p['w_cq'].astype(BF16), p['w_co'].astype(BF16),
                            vec(p['norm_ffn_g']), wr, br, tm=min(512, s))
    te = te.reshape(t, LANES)[:, :TOP_K]
    tok_blocks, dest, blk_expert, n_active = _route(te, n_experts, MOE_BLOCK)
    w1 = p['w1']
    y = _experts(blk_expert, n_active, tok_blocks, xn3.reshape(t, d),
                 w1[:, :, 0::2].astype(BF16), w1[:, :, 1::2].astype(BF16),
                 p['b1'][:, None, 0::2], p['b1'][:, None, 1::2],
                 p['w2'].astype(BF16), p['b2'][:, None, :])
    tb = min(256, t)
    return y, dest.reshape(t // tb, 1, tb * TOP_K), h2.reshape(t, d), tg.reshape(t, LANES), tb


def kernel(x, mem, norm_mix_g, w_in, conv_w, conv_b, b_igate, b_fgate, mlstm_norm_g, lambda_q1, lambda_k1, lambda_q2, lambda_k2, diff_norm_g, w_out, norm_xattn_g, norm_mem_g, w_cq, w_ckv, w_co, norm_ffn_g, w_router, b_router, w1, b1, w2, b2, norm_final_g):
    stacked = dict(norm_mix_g=norm_mix_g, w_in=w_in, conv_w=conv_w, conv_b=conv_b, b_igate=b_igate,
                   b_fgate=b_fgate, mlstm_norm_g=mlstm_norm_g, lambda_q1=lambda_q1, lambda_k1=lambda_k1,
                   lambda_q2=lambda_q2, lambda_k2=lambda_k2, diff_norm_g=diff_norm_g, w_out=w_out,
                   norm_xattn_g=norm_xattn_g, norm_mem_g=norm_mem_g, w_cq=w_cq, w_ckv=w_ckv, w_co=w_co,
                   norm_ffn_g=norm_ffn_g, w_router=w_router, b_router=b_router, w1=w1, b1=b1, w2=w2, b2=b2)
    depth = w_in.shape[0]
    b, s, d = x.shape
    h = x
    for l in range(depth):
        p = {k: v[l] for k, v in stacked.items()}
        y, dest_blocks, h2, tg, tb = _layer(h, mem, l, p)
        h = _combine(dest_blocks, y, h2, tg, norm_final_g.reshape(1, d), tb,
                     final_norm=l == depth - 1).reshape(b, s, d)
    return h
```

```python
import functools
import math

import jax
import jax.numpy as jnp
from jax import lax
from jax.experimental import pallas as pl
from jax.experimental.pallas import tpu as pltpu

F32 = jnp.float32
BF16 = jnp.bfloat16
I32 = jnp.int32

RMS_EPS = 1e-5
LANES = 128
VMEM_LIMIT = 56 * 1024 * 1024

DIFF_HEADS = 4
DIFF_QK_DIM = 64
MLSTM_HEADS = 4
MLSTM_QK_DIM = 64
CONV_K = 4
CHUNK = 128
XATTN_HEADS = 4
TOP_K = 4
SWIGLU_ALPHA = 1.702
SWIGLU_LIMIT = 7.0
MOE_BLOCK = 512
GROUP = 512
NEG_BIG = -1e30


def _rms(x, g):
    return x * lax.rsqrt(jnp.mean(x * x, axis=-1, keepdims=True) + RMS_EPS) * g


def _params(sem, vmem=VMEM_LIMIT):
    return pltpu.CompilerParams(dimension_semantics=sem, vmem_limit_bytes=vmem)


def _inproj_kernel(x_ref, g_ref, w_ref, wi_ref, wf_ref,
                   dq_ref, dk_ref, dv_ref, mqk_ref, mv_ref, mo_ref, gi_ref, gf_ref):
    xb = _rms(x_ref[...], g_ref[...]).astype(BF16)
    for n, o_ref in enumerate((dq_ref, dk_ref, dv_ref, mqk_ref, mv_ref, mo_ref)):
        o_ref[...] = jnp.dot(xb, w_ref[:, n * GROUP:(n + 1) * GROUP],
                             preferred_element_type=F32).astype(o_ref.dtype)
    gi_ref[...] = jnp.dot(xb, wi_ref[...], preferred_element_type=F32)
    gf_ref[...] = jnp.dot(xb, wf_ref[...], preferred_element_type=F32)


def _in_proj(x2, g, w_main, w_i, w_f, tm):
    t, d = x2.shape
    row = lambda i: (i, 0)
    const = lambda i: (0, 0)
    out_dtypes = (BF16, BF16, BF16, F32, BF16, F32)
    return pl.pallas_call(
        _inproj_kernel,
        grid=(t // tm,),
        in_specs=[pl.BlockSpec((tm, d), row), pl.BlockSpec((1, d), const),
                  pl.BlockSpec(w_main.shape, const), pl.BlockSpec(w_i.shape, const),
                  pl.BlockSpec(w_f.shape, const)],
        out_specs=[pl.BlockSpec((tm, GROUP), row)] * 6 + [pl.BlockSpec((tm, LANES), row)] * 2,
        out_shape=[jax.ShapeDtypeStruct((t, GROUP), dt) for dt in out_dtypes]
        + [jax.ShapeDtypeStruct((t, LANES), F32)] * 2,
        compiler_params=_params(("parallel",)),
        name="in_proj",
    )(x2, g, w_main, w_i, w_f)


def _diffattn_kernel(q_ref, k_ref, v_ref, lq1_ref, lk1_ref, lq2_ref, lk2_ref, g_ref, o_ref,
                     *, bq, lambda_init):
    i = pl.program_id(2)
    lane = lax.broadcasted_iota(I32, (1, LANES), 1)
    lo = lane < DIFF_QK_DIM
    q = q_ref[0] * jnp.asarray(DIFF_QK_DIM ** -0.5, BF16)
    zero = jnp.zeros_like(q)
    qs = (jnp.where(lo, q, zero), jnp.where(lo, zero, q))
    row = lax.broadcasted_iota(I32, (bq, bq), 0)
    col = lax.broadcasted_iota(I32, (bq, bq), 1)
    causal = col <= row

    def step(j, carry, masked):
        start = pl.multiple_of(j * bq, bq)
        kj = k_ref[0, pl.ds(start, bq), :]
        vj = v_ref[0, pl.ds(start, bq), :]
        new = []
        for qm, (m, l, a) in zip(qs, carry):
            s = lax.dot_general(qm, kj, (((1,), (1,)), ((), ())), preferred_element_type=F32)
            if masked:
                s = jnp.where(causal, s, -jnp.inf)
            mn = jnp.maximum(m, jnp.max(s, axis=-1, keepdims=True))
            p = jnp.exp(s - mn)
            alpha = jnp.exp(m - mn)
            l = alpha * l + jnp.sum(p, axis=-1, keepdims=True)
            a = alpha * a + jnp.dot(p.astype(BF16), vj, preferred_element_type=F32)
            new.append((mn, l, a))
        return tuple(new)

    init_one = (jnp.full((bq, 1), -jnp.inf, F32), jnp.zeros((bq, 1), F32), jnp.zeros((bq, LANES), F32))
    carry = lax.fori_loop(0, i, lambda j, c: step(j, c, False), (init_one, init_one))
    (_, l1, a1), (_, l2, a2) = step(i, carry, True)

    lam = (jnp.exp(jnp.sum(lq1_ref[...] * lk1_ref[...], axis=-1, keepdims=True))
           - jnp.exp(jnp.sum(lq2_ref[...] * lk2_ref[...], axis=-1, keepdims=True)) + lambda_init)
    od = a1 / l1 - lam * (a2 / l2)
    o_ref[0] = (_rms(od, g_ref[...]) * (1.0 - lambda_init)).astype(o_ref.dtype)


def _diff_attention(dq, dk, dv, lq1, lk1, lq2, lk2, g, lambda_init, bq):
    b, s, w = dq.shape
    qmap = lambda bi, h, i: (bi, i, h)
    kvmap = lambda bi, h, i: (bi, 0, h)
    vec = lambda bi, h, i: (0, 0)
    return pl.pallas_call(
        functools.partial(_diffattn_kernel, bq=bq, lambda_init=lambda_init),
        grid=(b, DIFF_HEADS, s // bq),
        in_specs=[pl.BlockSpec((1, bq, LANES), qmap), pl.BlockSpec((1, s, LANES), kvmap),
                  pl.BlockSpec((1, s, LANES), kvmap)]
        + [pl.BlockSpec((1, DIFF_QK_DIM), vec)] * 4
        + [pl.BlockSpec((1, LANES), lambda bi, h, i: (0, h))],
        out_specs=pl.BlockSpec((1, bq, LANES), qmap),
        out_shape=jax.ShapeDtypeStruct((b, s, w), BF16),
        compiler_params=_params(("parallel", "parallel", "arbitrary")),
        name="diff_attn",
    )(dq, dk, dv, lq1, lk1, lq2, lk2, g)


def _log_sigmoid(x):
    return -(jnp.maximum(-x, 0.0) + jnp.log1p(jnp.exp(-jnp.abs(x))))


def _mlstm_kernel(mqk_ref, mv_ref, gi_ref, gf_ref, mo_ref, cw_ref, cb_ref, bi_ref, bf_ref, ng_ref,
                  o_ref, conv_scr, c_scr, m_scr, *, bb):
    c = pl.program_id(1)
    L = CHUNK
    dk = MLSTM_QK_DIM
    nq = MLSTM_HEADS * dk
    tail = 8

    @pl.when(c == 0)
    def _():
        conv_scr[:, 0:tail, :] = jnp.zeros((bb, tail, 2 * nq), F32)
        c_scr[...] = jnp.zeros_like(c_scr)
        m_scr[...] = jnp.zeros_like(m_scr)

    row = lax.broadcasted_iota(I32, (L, L), 0)
    col = lax.broadcasted_iota(I32, (L, L), 1)
    causal = col <= row
    tril = causal.astype(F32)
    lane = lax.broadcasted_iota(I32, (1, LANES), 1)
    sub = lax.broadcasted_iota(I32, (LANES, 1), 0)
    ones_col = jnp.where(lane == 0, 1.0, 0.0).astype(BF16) * jnp.ones((L, 1), BF16)

    for b in range(bb):
        conv_scr[b, tail:tail + L, :] = mqk_ref[b]
        y = cb_ref[...]
        for j in range(CONV_K):
            y = y + conv_scr[b, pl.ds(tail - (CONV_K - 1) + j, L), :] * cw_ref[j:j + 1, :]
        conv_scr[b, 0:tail, :] = conv_scr[b, L:L + tail, :]
        qk = y * jax.nn.sigmoid(y)

        ig = gi_ref[b] + bi_ref[...]
        lf = _log_sigmoid(gf_ref[b] + bf_ref[...])
        bcum = jnp.dot(tril, lf, preferred_element_type=F32, precision=lax.Precision.HIGHEST)
        a_t = (ig - bcum).T

        for pair in range(MLSTM_HEADS // 2):
            q_pair = qk[:, pair * LANES:(pair + 1) * LANES] * (dk ** -0.5)
            k_pair = qk[:, nq + pair * LANES:nq + (pair + 1) * LANES]
            kt_pair = k_pair.T
            c_old = c_scr[b, pair]
            c_bf = c_old.astype(BF16)
            upd = jnp.zeros_like(c_old)
            decays = []
            for hh in range(2):
                h = 2 * pair + hh
                sel = (lane >= hh * dk) & (lane < (hh + 1) * dk)
                qm = jnp.where(sel, q_pair, 0.0).astype(BF16)
                km = jnp.where(sel, k_pair, 0.0).astype(BF16)
                selr = (sub >= hh * dk) & (sub < (hh + 1) * dk)
                ktm = jnp.where(selr, kt_pair, 0.0).astype(BF16)
                v_h = mv_ref[b, :, h * LANES:(h + 1) * LANES]
                v_aug = jnp.concatenate([v_h, ones_col], axis=1)

                m_st = m_scr[b, h, 0:1, 0:1]
                bc = bcum[:, h:h + 1]
                ic = ig[:, h:h + 1]
                a_m = jnp.where(causal, a_t[h:h + 1, :], -jnp.inf)
                inter = bc + m_st
                m_t = jnp.maximum(inter, bc + jnp.max(a_m, axis=-1, keepdims=True))
                qkt = lax.dot_general(qm, km, (((1,), (1,)), ((), ())), preferred_element_type=F32)
                w = qkt * jnp.exp(a_m + (bc - m_t))
                g = jnp.exp(inter - m_t)
                num_aug = (g * jnp.dot(qm, c_bf, preferred_element_type=F32)
                           + jnp.dot(w.astype(BF16), v_aug, preferred_element_type=F32))
                num = num_aug[:, :LANES]
                den = num_aug[:, LANES:LANES + 1]
                h_t = num / jnp.maximum(jnp.abs(den), jnp.exp(-m_t))

                b_last = bc[L - 1:L, :]
                gs = b_last - bc + ic
                m_new = jnp.maximum(b_last + m_st, jnp.max(gs, axis=0, keepdims=True))
                decays.append(jnp.exp(b_last + m_st - m_new))
                ws = jnp.exp(gs - m_new)
                wsv = (ws * v_aug.astype(F32)).astype(BF16)
                upd = upd + jnp.dot(ktm, wsv, preferred_element_type=F32)
                m_scr[b, h, 0:1, :] = jnp.broadcast_to(m_new, (1, LANES))

                hn = _rms(h_t, ng_ref[:, h * LANES:(h + 1) * LANES])
                gate = jax.nn.sigmoid(mo_ref[b, :, h * LANES:(h + 1) * LANES])
                o_ref[b, :, h * LANES:(h + 1) * LANES] = (hn * gate).astype(o_ref.dtype)
            d_rows = jnp.where(sub < dk, decays[0], decays[1])
            c_scr[b, pair] = d_rows * c_old + upd


def _mlstm(mqk, mv, gi, gf, mo, conv_w, conv_b, b_i, b_f, norm_g, bb):
    b, s, w = mqk.shape
    blk = lambda bi, c: (bi, c, 0)
    const = lambda bi, c: (0, 0)
    return pl.pallas_call(
        functools.partial(_mlstm_kernel, bb=bb),
        grid=(b // bb, s // CHUNK),
        in_specs=[pl.BlockSpec((bb, CHUNK, w), blk), pl.BlockSpec((bb, CHUNK, w), blk),
                  pl.BlockSpec((bb, CHUNK, LANES), blk), pl.BlockSpec((bb, CHUNK, LANES), blk),
                  pl.BlockSpec((bb, CHUNK, w), blk),
                  pl.BlockSpec(conv_w.shape, const), pl.BlockSpec(conv_b.shape, const),
                  pl.BlockSpec(b_i.shape, const), pl.BlockSpec(b_f.shape, const),
                  pl.BlockSpec(norm_g.shape, const)],
        out_specs=pl.BlockSpec((bb, CHUNK, w), blk),
        out_shape=jax.ShapeDtypeStruct((b, s, w), BF16),
        scratch_shapes=[pltpu.VMEM((bb, CHUNK + 8, w), F32),
                        pltpu.VMEM((bb, MLSTM_HEADS // 2, LANES, 2 * LANES), F32),
                        pltpu.VMEM((bb, MLSTM_HEADS, 8, LANES), F32)],
        compiler_params=_params(("parallel", "arbitrary")),
        name="mlstm",
    )(mqk, mv, gi, gf, mo, conv_w, conv_b, b_i, b_f, norm_g)


def _kvproj_kernel(mem_ref, g_ref, w_ref, k_ref, v_ref):
    d = mem_ref.shape[-1]
    mb = _rms(mem_ref[0], g_ref[...]).astype(BF16)
    k_ref[0] = jnp.dot(mb, w_ref[:, :d], preferred_element_type=F32).astype(k_ref.dtype)
    v_ref[0] = jnp.dot(mb, w_ref[:, d:], preferred_element_type=F32).astype(v_ref.dtype)


def _kv_proj(mem, g, w_ckv):
    b, m, d = mem.shape
    blk = lambda bi: (bi, 0, 0)
    const = lambda bi: (0, 0)
    return pl.pallas_call(
        _kvproj_kernel,
        grid=(b,),
        in_specs=[pl.BlockSpec((1, m, d), blk), pl.BlockSpec((1, d), const),
                  pl.BlockSpec(w_ckv.shape, const)],
        out_specs=[pl.BlockSpec((1, m, d), blk)] * 2,
        out_shape=[jax.ShapeDtypeStruct((b, m, d), BF16)] * 2,
        compiler_params=_params(("parallel",)),
        name="kv_proj",
    )(mem, g, w_ckv)


def _post_kernel(od_ref, hm_ref, x_ref, k_ref, v_ref, wo1_ref, wo2_ref, g2_ref, wcq_ref, wco_ref,
                 g3_ref, wr_ref, br_ref, h2_ref, xn3_ref, te_ref, tg_ref, cnt_ref, base_scr):
    d = x_ref.shape[-1]
    hd = d // XATTN_HEADS
    h1 = (x_ref[0] + jnp.dot(od_ref[0], wo1_ref[...], preferred_element_type=F32)
          + jnp.dot(hm_ref[0], wo2_ref[...], preferred_element_type=F32))
    q = jnp.dot(_rms(h1, g2_ref[...]).astype(BF16), wcq_ref[...], preferred_element_type=F32)
    q = (q * (hd ** -0.5)).astype(BF16)
    heads = []
    for h in range(XATTN_HEADS):
        sl = slice(h * hd, (h + 1) * hd)
        s = lax.dot_general(q[:, sl], k_ref[0, :, sl], (((1,), (1,)), ((), ())),
                            preferred_element_type=F32)
        e = jnp.exp(s - jnp.max(s, axis=-1, keepdims=True))
        p = e / jnp.sum(e, axis=-1, keepdims=True)
        heads.append(jnp.dot(p.astype(BF16), v_ref[0, :, sl], preferred_element_type=F32))
    o = jnp.concatenate(heads, axis=1).astype(BF16)
    h2 = h1 + jnp.dot(o, wco_ref[...], preferred_element_type=F32)
    h2_ref[0] = h2
    xn3 = _rms(h2, g3_ref[...])
    xn3_ref[0] = xn3
    logits = jnp.dot(xn3, wr_ref[...], preferred_element_type=F32,
                     precision=lax.Precision.HIGHEST) + br_ref[...]
    lane = lax.broadcasted_iota(I32, logits.shape, 1)
    lane_f = lane.astype(F32)
    cur = logits
    te = jnp.zeros(logits.shape, F32)
    chosen = jnp.zeros(logits.shape, F32)
    vals, hits = [], []
    for k in range(TOP_K):
        m = jnp.max(cur, axis=-1, keepdims=True)
        idx = jnp.min(jnp.where(cur == m, lane_f, float(LANES)), axis=-1, keepdims=True)
        hit = lane_f == idx
        vals.append(m)
        hits.append(hit)
        te = jnp.where(lane == k, idx, te)
        chosen = jnp.where(hit, 1.0, chosen)
        cur = jnp.where(hit, NEG_BIG * 2.0, cur)
    es = [jnp.exp(v - vals[0]) for v in vals]
    tot = es[0] + es[1] + es[2] + es[3]
    tg = jnp.zeros(logits.shape, F32)
    for k in range(TOP_K):
        tg = jnp.where(lane == k, es[k] / tot, tg)
    tg_ref[0] = tg

    @pl.when((pl.program_id(0) == 0) & (pl.program_id(1) == 0))
    def _():
        base_scr[...] = jnp.zeros_like(base_scr)
    tm = logits.shape[0]
    earlier = (lax.broadcasted_iota(I32, (tm, tm), 1) < lax.broadcasted_iota(I32, (tm, tm), 0)).astype(BF16)
    prior = jnp.dot(earlier, chosen.astype(BF16), preferred_element_type=F32) + base_scr[...]
    for k in range(TOP_K):
        rank = jnp.sum(jnp.where(hits[k], prior, 0.0), axis=-1, keepdims=True)
        te = jnp.where(lane == TOP_K + k, rank, te)
    te_ref[0] = te.astype(I32)
    base_scr[...] = base_scr[...] + jnp.sum(chosen, axis=0, keepdims=True)
    cnt_ref[...] = base_scr[...]


def _post(od, hm, x, kmem, vmem, wo1, wo2, g2, wcq, wco, g3, wr, br, tm):
    b, s, d = x.shape
    w = od.shape[-1]
    m = kmem.shape[1]
    blk = lambda bi, i: (bi, i, 0)
    mem = lambda bi, i: (bi, 0, 0)
    const = lambda bi, i: (0, 0)
    full = lambda a: pl.BlockSpec(a.shape, const)
    return pl.pallas_call(
        _post_kernel,
        grid=(b, s // tm),
        in_specs=[pl.BlockSpec((1, tm, w), blk), pl.BlockSpec((1, tm, w), blk),
                  pl.BlockSpec((1, tm, d), blk), pl.BlockSpec((1, m, d), mem),
                  pl.BlockSpec((1, m, d), mem), full(wo1), full(wo2), full(g2), full(wcq),
                  full(wco), full(g3), full(wr), full(br)],
        out_specs=[pl.BlockSpec((1, tm, d), blk), pl.BlockSpec((1, tm, d), blk),
                   pl.BlockSpec((1, tm, LANES), blk), pl.BlockSpec((1, tm, LANES), blk),
                   pl.BlockSpec((1, LANES), const)],
        out_shape=[jax.ShapeDtypeStruct((b, s, d), F32), jax.ShapeDtypeStruct((b, s, d), F32),
                   jax.ShapeDtypeStruct((b, s, LANES), I32), jax.ShapeDtypeStruct((b, s, LANES), F32),
                   jax.ShapeDtypeStruct((1, LANES), F32)],
        scratch_shapes=[pltpu.VMEM((1, LANES), F32)],
        compiler_params=_params(("arbitrary", "arbitrary")),
        name="post_mixer",
    )(od, hm, x, kmem, vmem, wo1, wo2, g2, wcq, wco, g3, wr, br)


def _split_kernel(w_ref, p_ref, wg_ref, wl_ref):
    wb = w_ref[0].astype(BF16)
    two = 2 * LANES
    for g in range(wb.shape[1] // two):
        r = jnp.dot(wb[:, g * two:(g + 1) * two], p_ref[...], preferred_element_type=F32)
        wg_ref[0, :, g * LANES:(g + 1) * LANES] = r[:, :LANES].astype(BF16)
        wl_ref[0, :, g * LANES:(g + 1) * LANES] = r[:, LANES:].astype(BF16)


def _split_w1(w1, rb):
    e, d, de2 = w1.shape
    two = 2 * LANES
    src = jnp.arange(two)[:, None]
    dst = jnp.arange(two)[None, :]
    perm = jnp.where(dst < LANES, src == 2 * dst, src == 2 * (dst - LANES) + 1).astype(BF16)
    blk = lambda ei, r: (ei, r, 0)
    return pl.pallas_call(
        _split_kernel,
        grid=(e, d // rb),
        in_specs=[pl.BlockSpec((1, rb, de2), blk), pl.BlockSpec((two, two), lambda ei, r: (0, 0))],
        out_specs=[pl.BlockSpec((1, rb, de2 // 2), blk)] * 2,
        out_shape=[jax.ShapeDtypeStruct((e, d, de2 // 2), BF16)] * 2,
        compiler_params=_params(("parallel", "parallel")),
        name="split_w1",
    )(w1, perm)


def _dispatch_kernel(nz_ref, dest_ref, x_ref, xs_hbm, zbuf, sem, zsem, *, tb, bm, n_blocks):
    i = pl.program_id(0)

    @pl.when(i == 0)
    def _():
        zbuf[...] = jnp.zeros_like(zbuf)

        def zstart(blk, carry):
            @pl.when(nz_ref[blk] != 0)
            def _():
                pltpu.make_async_copy(zbuf, xs_hbm.at[pl.ds(pl.multiple_of(blk * bm, bm), bm), :], zsem).start()
            return carry
        lax.fori_loop(0, n_blocks, zstart, 0)

        def zwait(blk, carry):
            @pl.when(nz_ref[blk] != 0)
            def _():
                pltpu.make_async_copy(zbuf, xs_hbm.at[pl.ds(0, bm), :], zsem).wait()
            return carry
        lax.fori_loop(0, n_blocks, zwait, 0)

    def issue(t, carry):
        for k in range(TOP_K):
            dst = dest_ref[0, 0, t * TOP_K + k]
            pltpu.make_async_copy(x_ref.at[pl.ds(t, 1), :], xs_hbm.at[pl.ds(dst, 1), :], sem).start()
        return carry
    lax.fori_loop(0, tb, issue, 0)
    for k in range(TOP_K):
        pltpu.make_async_copy(x_ref, xs_hbm.at[pl.ds(0, tb), :], sem).wait()


def _dispatch(needs_zero, dest_blocks, xn3, bm, tb):
    t, d = xn3.shape
    n_blocks = needs_zero.shape[0]
    grid_spec = pltpu.PrefetchScalarGridSpec(
        num_scalar_prefetch=1,
        grid=(t // tb,),
        in_specs=[pl.BlockSpec((1, 1, tb * TOP_K), lambda i, nz: (i, 0, 0), memory_space=pltpu.SMEM),
                  pl.BlockSpec((tb, d), lambda i, nz: (i, 0))],
        out_specs=pl.BlockSpec(memory_space=pl.ANY),
        scratch_shapes=[pltpu.VMEM((bm, d), F32), pltpu.SemaphoreType.DMA, pltpu.SemaphoreType.DMA],
    )
    return pl.pallas_call(
        functools.partial(_dispatch_kernel, tb=tb, bm=bm, n_blocks=n_blocks),
        grid_spec=grid_spec,
        out_shape=jax.ShapeDtypeStruct((n_blocks * bm, d), F32),
        compiler_params=_params(("arbitrary",)),
        name="dispatch",
    )(needs_zero, dest_blocks, xn3)


def _expert_kernel(be_ref, na_ref, x_ref, w1g_ref, w1l_ref, b1g_ref, b1l_ref, w2_ref, b2_ref, y_ref):
    i = pl.program_id(0)

    @pl.when(i < na_ref[0])
    def _():
        xb = x_ref[...].astype(BF16)
        glu = jnp.dot(xb, w1g_ref[0], preferred_element_type=F32) + b1g_ref[0]
        lin = jnp.dot(xb, w1l_ref[0], preferred_element_type=F32) + b1l_ref[0]
        glu = jnp.minimum(glu, SWIGLU_LIMIT)
        lin = jnp.clip(lin, -SWIGLU_LIMIT, SWIGLU_LIMIT)
        act = glu * jax.nn.sigmoid(SWIGLU_ALPHA * glu) * (lin + 1.0)
        y_ref[...] = jnp.dot(act.astype(BF16), w2_ref[0], preferred_element_type=F32) + b2_ref[0]

    @pl.when(i >= na_ref[0])
    def _():
        y_ref[...] = jnp.zeros_like(y_ref)


def _experts(blk_expert, n_active, xs, w1g, w1l, b1g, b1l, w2, b2, bm):
    n_pad, d = xs.shape
    n_blocks = n_pad // bm
    de = w1g.shape[-1]
    last = lambda i, na: jnp.minimum(i, na[0] - 1)
    wmap = lambda i, be, na: (be[last(i, na)], 0, 0)
    grid_spec = pltpu.PrefetchScalarGridSpec(
        num_scalar_prefetch=2,
        grid=(n_blocks,),
        in_specs=[pl.BlockSpec((bm, d), lambda i, be, na: (last(i, na), 0)),
                  pl.BlockSpec((1, d, de), wmap), pl.BlockSpec((1, d, de), wmap),
                  pl.BlockSpec((1, 1, de), wmap), pl.BlockSpec((1, 1, de), wmap),
                  pl.BlockSpec((1, de, d), wmap), pl.BlockSpec((1, 1, d), wmap)],
        out_specs=pl.BlockSpec((bm, d), lambda i, be, na: (i, 0)),
    )
    return pl.pallas_call(
        _expert_kernel,
        grid_spec=grid_spec,
        out_shape=jax.ShapeDtypeStruct((n_pad, d), F32),
        compiler_params=_params(("arbitrary",)),
        name="experts",
    )(blk_expert, n_active, xs, w1g, w1l, b1g, b1l, w2, b2)


def _combine_kernel(dest_ref, y_hbm, h2_ref, tg_ref, g_ref, o_ref, buf, sem, *, tb, final_norm):
    def issue(t, carry):
        for k in range(TOP_K):
            dst = dest_ref[0, 0, t * TOP_K + k]
            pltpu.make_async_copy(y_hbm.at[pl.ds(dst, 1), :], buf.at[k, pl.ds(t, 1), :], sem).start()
        return carry
    lax.fori_loop(0, tb, issue, 0)
    for k in range(TOP_K):
        pltpu.make_async_copy(y_hbm.at[pl.ds(0, tb), :], buf.at[k], sem).wait()

    acc = h2_ref[...]
    for k in range(TOP_K):
        acc = acc + buf[k] * tg_ref[:, k:k + 1]
    o_ref[...] = _rms(acc, g_ref[...]) if final_norm else acc


def _combine(dest_blocks, y, h2, tg, g, tb, final_norm):
    t, d = h2.shape
    row = lambda i: (i, 0)
    return pl.pallas_call(
        functools.partial(_combine_kernel, tb=tb, final_norm=final_norm),
        grid=(t // tb,),
        in_specs=[pl.BlockSpec((1, 1, tb * TOP_K), lambda i: (i, 0, 0), memory_space=pltpu.SMEM),
                  pl.BlockSpec(memory_space=pl.ANY),
                  pl.BlockSpec((tb, d), row), pl.BlockSpec((tb, LANES), row),
                  pl.BlockSpec((1, d), lambda i: (0, 0))],
        out_specs=pl.BlockSpec((tb, d), row),
        out_shape=jax.ShapeDtypeStruct((t, d), F32),
        scratch_shapes=[pltpu.VMEM((TOP_K, tb, d), F32), pltpu.SemaphoreType.DMA],
        compiler_params=_params(("arbitrary",)),
        name="combine",
    )(dest_blocks, y, h2, tg, g)


def _route(top_e, rank, counts, n_experts, bm):
    a = top_e.size
    padded = (counts + bm - 1) // bm * bm
    pend = jnp.cumsum(padded)
    pstart = pend - padded
    dest = (jnp.take(pstart, top_e) + rank).astype(I32)
    n_blocks = -(-a // bm) + n_experts
    blk_lo = jnp.arange(n_blocks, dtype=I32) * bm
    blk_expert = jnp.minimum(jnp.sum(blk_lo[:, None] >= pend[None, :], axis=1), n_experts - 1).astype(I32)
    n_active = (pend[-1] // bm).astype(I32)
    has_pad = jnp.any((blk_lo[:, None] + bm) == pend[None, :], axis=1)
    needs_zero = (has_pad | (jnp.arange(n_blocks) >= n_active)).astype(I32)
    return dest, blk_expert, n_active.reshape(1), needs_zero


def _pad_cols(w, n):
    return jnp.pad(w, ((0, 0), (0, n - w.shape[1])))


def _layer(h, mem, l, p):
    b, s, d = h.shape
    t = b * s
    lambda_init = 0.8 - 0.6 * math.exp(-0.3 * l)
    n_main = 6 * GROUP
    w_in = p['w_in']
    w_main = w_in[:, :n_main].astype(BF16)
    w_i = _pad_cols(w_in[:, n_main:n_main + MLSTM_HEADS], LANES).astype(BF16)
    w_f = _pad_cols(w_in[:, n_main + MLSTM_HEADS:], LANES).astype(BF16)
    dq, dk, dv, mqk, mv, mo, gi, gf = _in_proj(h.reshape(t, d), p['norm_mix_g'].reshape(1, d),
                                               w_main, w_i, w_f, tm=min(512, t))
    r3 = lambda a: a.reshape(b, s, a.shape[-1])
    vec = lambda a: a.reshape(1, -1)
    od = _diff_attention(r3(dq), r3(dk), r3(dv), vec(p['lambda_q1']), vec(p['lambda_k1']),
                         vec(p['lambda_q2']), vec(p['lambda_k2']), vec(p['diff_norm_g']),
                         lambda_init, bq=min(512, s))
    hm = _mlstm(r3(mqk), r3(mv), r3(gi), r3(gf), r3(mo), p['conv_w'], vec(p['conv_b']),
                _pad_cols(vec(p['b_igate']), LANES), _pad_cols(vec(p['b_fgate']), LANES),
                vec(p['mlstm_norm_g']), bb=2 if b % 2 == 0 else 1)
    kmem, vmem = _kv_proj(mem, vec(p['norm_mem_g']), p['w_ckv'].astype(BF16))
    w_out = p['w_out'].astype(BF16)
    n_experts = p['w_router'].shape[1]
    wr = _pad_cols(p['w_router'], LANES)
    br = jnp.concatenate([vec(p['b_router']), jnp.full((1, LANES - n_experts), NEG_BIG, F32)], axis=1)
    h2, xn3, te, tg, cnt = _post(od, hm, h, kmem, vmem, w_out[:GROUP], w_out[GROUP:],
                                 vec(p['norm_xattn_g']), p['w_cq'].astype(BF16), p['w_co'].astype(BF16),
                                 vec(p['norm_ffn_g']), wr, br, tm=min(512, s))
    te = te.reshape(t, LANES)
    dest, blk_expert, n_active, needs_zero = _route(te[:, :TOP_K], te[:, TOP_K:2 * TOP_K],
                                                    cnt[0, :n_experts].astype(I32), n_experts, MOE_BLOCK)
    tb_d = min(512, t)
    xs = _dispatch(needs_zero, dest.reshape(t // tb_d, 1, tb_d * TOP_K), xn3.reshape(t, d), MOE_BLOCK, tb_d)
    w1g, w1l = _split_w1(p['w1'], rb=256)
    y = _experts(blk_expert, n_active, xs, w1g, w1l,
                 p['b1'][:, None, 0::2], p['b1'][:, None, 1::2],
                 p['w2'].astype(BF16), p['b2'][:, None, :], MOE_BLOCK)
    tb = min(256, t)
    return y, dest.reshape(t // tb, 1, tb * TOP_K), h2.reshape(t, d), tg.reshape(t, LANES), tb


def kernel(x, mem, norm_mix_g, w_in, conv_w, conv_b, b_igate, b_fgate, mlstm_norm_g, lambda_q1, lambda_k1, lambda_q2, lambda_k2, diff_norm_g, w_out, norm_xattn_g, norm_mem_g, w_cq, w_ckv, w_co, norm_ffn_g, w_router, b_router, w1, b1, w2, b2, norm_final_g):
    stacked = dict(norm_mix_g=norm_mix_g, w_in=w_in, conv_w=conv_w, conv_b=conv_b, b_igate=b_igate,
                   b_fgate=b_fgate, mlstm_norm_g=mlstm_norm_g, lambda_q1=lambda_q1, lambda_k1=lambda_k1,
                   lambda_q2=lambda_q2, lambda_k2=lambda_k2, diff_norm_g=diff_norm_g, w_out=w_out,
                   norm_xattn_g=norm_xattn_g, norm_mem_g=norm_mem_g, w_cq=w_cq, w_ckv=w_ckv, w_co=w_co,
                   norm_ffn_g=norm_ffn_g, w_router=w_router, b_router=b_router, w1=w1, b1=b1, w2=w2, b2=b2)
    depth = w_in.shape[0]
    b, s, d = x.shape
    h = x
    for l in range(depth):
        p = {k: v[l] for k, v in stacked.items()}
        y, dest_blocks, h2, tg, tb = _layer(h, mem, l, p)
        h = _combine(dest_blocks, y, h2, tg, norm_final_g.reshape(1, d), tb,
                     final_norm=l == depth - 1).reshape(b, s, d)
    return h
```

```python
import functools
import math

import jax
import jax.numpy as jnp
from jax import lax
from jax.experimental import pallas as pl
from jax.experimental.pallas import tpu as pltpu

F32 = jnp.float32
BF16 = jnp.bfloat16
I32 = jnp.int32

RMS_EPS = 1e-5
LANES = 128
VMEM_LIMIT = 56 * 1024 * 1024

DIFF_HEADS = 4
DIFF_QK_DIM = 64
MLSTM_HEADS = 4
MLSTM_QK_DIM = 64
CONV_K = 4
CHUNK = 128
XATTN_HEADS = 4
TOP_K = 4
SWIGLU_ALPHA = 1.702
SWIGLU_LIMIT = 7.0
MOE_BLOCK = 512
GROUP = 512
NEG_BIG = -1e30


def _rms(x, g):
    return x * lax.rsqrt(jnp.mean(x * x, axis=-1, keepdims=True) + RMS_EPS) * g


def _params(sem, vmem=VMEM_LIMIT, flags=None):
    return pltpu.CompilerParams(dimension_semantics=sem, vmem_limit_bytes=vmem, flags=flags)


def _inproj_kernel(x_ref, g_ref, w_ref, wi_ref, wf_ref,
                   dq_ref, dk_ref, dv_ref, mqk_ref, mv_ref, mo_ref, gi_ref, gf_ref):
    xb = _rms(x_ref[...], g_ref[...]).astype(BF16)
    for n, o_ref in enumerate((dq_ref, dk_ref, dv_ref, mqk_ref, mv_ref, mo_ref)):
        o_ref[...] = jnp.dot(xb, w_ref[:, n * GROUP:(n + 1) * GROUP],
                             preferred_element_type=F32).astype(o_ref.dtype)
    gi_ref[...] = jnp.dot(xb, wi_ref[...], preferred_element_type=F32)
    gf_ref[...] = jnp.dot(xb, wf_ref[...], preferred_element_type=F32)


def _in_proj(x2, g, w_main, w_i, w_f, tm):
    t, d = x2.shape
    row = lambda i: (i, 0)
    const = lambda i: (0, 0)
    out_dtypes = (BF16, BF16, BF16, F32, BF16, F32)
    return pl.pallas_call(
        _inproj_kernel,
        grid=(t // tm,),
        in_specs=[pl.BlockSpec((tm, d), row), pl.BlockSpec((1, d), const),
                  pl.BlockSpec(w_main.shape, const), pl.BlockSpec(w_i.shape, const),
                  pl.BlockSpec(w_f.shape, const)],
        out_specs=[pl.BlockSpec((tm, GROUP), row)] * 6 + [pl.BlockSpec((tm, LANES), row)] * 2,
        out_shape=[jax.ShapeDtypeStruct((t, GROUP), dt) for dt in out_dtypes]
        + [jax.ShapeDtypeStruct((t, LANES), F32)] * 2,
        compiler_params=_params(("parallel",)),
        name="in_proj",
    )(x2, g, w_main, w_i, w_f)


def _diffattn_kernel(q_ref, k_ref, v_ref, lq1_ref, lk1_ref, lq2_ref, lk2_ref, g_ref, o_ref,
                     *, bq, lambda_init):
    i = pl.program_id(2)
    lane = lax.broadcasted_iota(I32, (1, LANES), 1)
    lo = lane < DIFF_QK_DIM
    q = q_ref[0] * jnp.asarray(DIFF_QK_DIM ** -0.5, BF16)
    zero = jnp.zeros_like(q)
    qs = (jnp.where(lo, q, zero), jnp.where(lo, zero, q))
    row = lax.broadcasted_iota(I32, (bq, bq), 0)
    col = lax.broadcasted_iota(I32, (bq, bq), 1)
    causal = col <= row

    def step(j, carry, masked):
        start = pl.multiple_of(j * bq, bq)
        kj = k_ref[0, pl.ds(start, bq), :]
        vj = v_ref[0, pl.ds(start, bq), :]
        new = []
        for qm, (m, l, a) in zip(qs, carry):
            s = lax.dot_general(qm, kj, (((1,), (1,)), ((), ())), preferred_element_type=F32)
            if masked:
                s = jnp.where(causal, s, -jnp.inf)
            mn = jnp.maximum(m, jnp.max(s, axis=-1, keepdims=True))
            p = jnp.exp(s - mn)
            alpha = jnp.exp(m - mn)
            l = alpha * l + jnp.sum(p, axis=-1, keepdims=True)
            a = alpha * a + jnp.dot(p.astype(BF16), vj, preferred_element_type=F32)
            new.append((mn, l, a))
        return tuple(new)

    init_one = (jnp.full((bq, 1), -jnp.inf, F32), jnp.zeros((bq, 1), F32), jnp.zeros((bq, LANES), F32))
    carry = lax.fori_loop(0, i, lambda j, c: step(j, c, False), (init_one, init_one))
    (_, l1, a1), (_, l2, a2) = step(i, carry, True)

    lam = (jnp.exp(jnp.sum(lq1_ref[...] * lk1_ref[...], axis=-1, keepdims=True))
           - jnp.exp(jnp.sum(lq2_ref[...] * lk2_ref[...], axis=-1, keepdims=True)) + lambda_init)
    od = a1 / l1 - lam * (a2 / l2)
    o_ref[0] = (_rms(od, g_ref[...]) * (1.0 - lambda_init)).astype(o_ref.dtype)


def _diff_attention(dq, dk, dv, lq1, lk1, lq2, lk2, g, lambda_init, bq):
    b, s, w = dq.shape
    qmap = lambda bi, h, i: (bi, i, h)
    kvmap = lambda bi, h, i: (bi, 0, h)
    vec = lambda bi, h, i: (0, 0)
    return pl.pallas_call(
        functools.partial(_diffattn_kernel, bq=bq, lambda_init=lambda_init),
        grid=(b, DIFF_HEADS, s // bq),
        in_specs=[pl.BlockSpec((1, bq, LANES), qmap), pl.BlockSpec((1, s, LANES), kvmap),
                  pl.BlockSpec((1, s, LANES), kvmap)]
        + [pl.BlockSpec((1, DIFF_QK_DIM), vec)] * 4
        + [pl.BlockSpec((1, LANES), lambda bi, h, i: (0, h))],
        out_specs=pl.BlockSpec((1, bq, LANES), qmap),
        out_shape=jax.ShapeDtypeStruct((b, s, w), BF16),
        compiler_params=_params(("parallel", "parallel", "arbitrary")),
        name="diff_attn",
    )(dq, dk, dv, lq1, lk1, lq2, lk2, g)


def _log_sigmoid(x):
    return -(jnp.maximum(-x, 0.0) + jnp.log1p(jnp.exp(-jnp.abs(x))))


def _mlstm_kernel(mqk_ref, mv_ref, gi_ref, gf_ref, mo_ref, cw_ref, cb_ref, bi_ref, bf_ref, ng_ref,
                  o_ref, conv_scr, c_scr, m_scr, *, bb):
    c = pl.program_id(1)
    L = CHUNK
    dk = MLSTM_QK_DIM
    nq = MLSTM_HEADS * dk
    tail = 8

    @pl.when(c == 0)
    def _():
        conv_scr[:, 0:tail, :] = jnp.zeros((bb, tail, 2 * nq), F32)
        c_scr[...] = jnp.zeros_like(c_scr)
        m_scr[...] = jnp.zeros_like(m_scr)

    row = lax.broadcasted_iota(I32, (L, L), 0)
    col = lax.broadcasted_iota(I32, (L, L), 1)
    causal = col <= row
    tril = causal.astype(F32)
    lane = lax.broadcasted_iota(I32, (1, LANES), 1)
    sub = lax.broadcasted_iota(I32, (LANES, 1), 0)
    ones_col = jnp.where(lane == 0, 1.0, 0.0).astype(BF16) * jnp.ones((L, 1), BF16)

    for b in range(bb):
        conv_scr[b, tail:tail + L, :] = mqk_ref[b]
        y = cb_ref[...]
        for j in range(CONV_K):
            y = y + conv_scr[b, pl.ds(tail - (CONV_K - 1) + j, L), :] * cw_ref[j:j + 1, :]
        conv_scr[b, 0:tail, :] = conv_scr[b, L:L + tail, :]
        qk = y * jax.nn.sigmoid(y)

        ig = gi_ref[b] + bi_ref[...]
        lf = _log_sigmoid(gf_ref[b] + bf_ref[...])
        bcum = jnp.dot(tril, lf, preferred_element_type=F32, precision=lax.Precision.HIGHEST)
        a_t = (ig - bcum).T

        for pair in range(MLSTM_HEADS // 2):
            q_pair = qk[:, pair * LANES:(pair + 1) * LANES] * (dk ** -0.5)
            k_pair = qk[:, nq + pair * LANES:nq + (pair + 1) * LANES]
            kt_pair = k_pair.T
            c_old = c_scr[b, pair]
            c_bf = c_old.astype(BF16)
            upd = jnp.zeros_like(c_old)
            decays = []
            for hh in range(2):
                h = 2 * pair + hh
                sel = (lane >= hh * dk) & (lane < (hh + 1) * dk)
                qm = jnp.where(sel, q_pair, 0.0).astype(BF16)
                km = jnp.where(sel, k_pair, 0.0).astype(BF16)
                selr = (sub >= hh * dk) & (sub < (hh + 1) * dk)
                ktm = jnp.where(selr, kt_pair, 0.0).astype(BF16)
                v_h = mv_ref[b, :, h * LANES:(h + 1) * LANES]
                v_aug = jnp.concatenate([v_h, ones_col], axis=1)

                m_st = m_scr[b, h, 0:1, 0:1]
                bc = bcum[:, h:h + 1]
                ic = ig[:, h:h + 1]
                a_m = jnp.where(causal, a_t[h:h + 1, :], -jnp.inf)
                inter = bc + m_st
                m_t = jnp.maximum(inter, bc + jnp.max(a_m, axis=-1, keepdims=True))
                qkt = lax.dot_general(qm, km, (((1,), (1,)), ((), ())), preferred_element_type=F32)
                w = qkt * jnp.exp(a_m + (bc - m_t))
                g = jnp.exp(inter - m_t)
                num_aug = (g * jnp.dot(qm, c_bf, preferred_element_type=F32)
                           + jnp.dot(w.astype(BF16), v_aug, preferred_element_type=F32))
                num = num_aug[:, :LANES]
                den = num_aug[:, LANES:LANES + 1]
                h_t = num / jnp.maximum(jnp.abs(den), jnp.exp(-m_t))

                b_last = bc[L - 1:L, :]
                gs = b_last - bc + ic
                m_new = jnp.maximum(b_last + m_st, jnp.max(gs, axis=0, keepdims=True))
                decays.append(jnp.exp(b_last + m_st - m_new))
                ws = jnp.exp(gs - m_new)
                wsv = (ws * v_aug.astype(F32)).astype(BF16)
                upd = upd + jnp.dot(ktm, wsv, preferred_element_type=F32)
                m_scr[b, h, 0:1, :] = jnp.broadcast_to(m_new, (1, LANES))

                hn = _rms(h_t, ng_ref[:, h * LANES:(h + 1) * LANES])
                gate = jax.nn.sigmoid(mo_ref[b, :, h * LANES:(h + 1) * LANES])
                o_ref[b, :, h * LANES:(h + 1) * LANES] = (hn * gate).astype(o_ref.dtype)
            d_rows = jnp.where(sub < dk, decays[0], decays[1])
            c_scr[b, pair] = d_rows * c_old + upd


def _mlstm(mqk, mv, gi, gf, mo, conv_w, conv_b, b_i, b_f, norm_g, bb):
    b, s, w = mqk.shape
    blk = lambda bi, c: (bi, c, 0)
    const = lambda bi, c: (0, 0)
    return pl.pallas_call(
        functools.partial(_mlstm_kernel, bb=bb),
        grid=(b // bb, s // CHUNK),
        in_specs=[pl.BlockSpec((bb, CHUNK, w), blk), pl.BlockSpec((bb, CHUNK, w), blk),
                  pl.BlockSpec((bb, CHUNK, LANES), blk), pl.BlockSpec((bb, CHUNK, LANES), blk),
                  pl.BlockSpec((bb, CHUNK, w), blk),
                  pl.BlockSpec(conv_w.shape, const), pl.BlockSpec(conv_b.shape, const),
                  pl.BlockSpec(b_i.shape, const), pl.BlockSpec(b_f.shape, const),
                  pl.BlockSpec(norm_g.shape, const)],
        out_specs=pl.BlockSpec((bb, CHUNK, w), blk),
        out_shape=jax.ShapeDtypeStruct((b, s, w), BF16),
        scratch_shapes=[pltpu.VMEM((bb, CHUNK + 8, w), F32),
                        pltpu.VMEM((bb, MLSTM_HEADS // 2, LANES, 2 * LANES), F32),
                        pltpu.VMEM((bb, MLSTM_HEADS, 8, LANES), F32)],
        compiler_params=_params(("parallel", "arbitrary")),
        name="mlstm",
    )(mqk, mv, gi, gf, mo, conv_w, conv_b, b_i, b_f, norm_g)


def _kvproj_kernel(mem_ref, g_ref, w_ref, k_ref, v_ref):
    d = mem_ref.shape[-1]
    mb = _rms(mem_ref[0], g_ref[...]).astype(BF16)
    k_ref[0] = jnp.dot(mb, w_ref[:, :d], preferred_element_type=F32).astype(k_ref.dtype)
    v_ref[0] = jnp.dot(mb, w_ref[:, d:], preferred_element_type=F32).astype(v_ref.dtype)


def _kv_proj(mem, g, w_ckv):
    b, m, d = mem.shape
    blk = lambda bi: (bi, 0, 0)
    const = lambda bi: (0, 0)
    return pl.pallas_call(
        _kvproj_kernel,
        grid=(b,),
        in_specs=[pl.BlockSpec((1, m, d), blk), pl.BlockSpec((1, d), const),
                  pl.BlockSpec(w_ckv.shape, const)],
        out_specs=[pl.BlockSpec((1, m, d), blk)] * 2,
        out_shape=[jax.ShapeDtypeStruct((b, m, d), BF16)] * 2,
        compiler_params=_params(("parallel",)),
        name="kv_proj",
    )(mem, g, w_ckv)


def _post_kernel(od_ref, hm_ref, x_ref, k_ref, v_ref, wo1_ref, wo2_ref, g2_ref, wcq_ref, wco_ref,
                 g3_ref, wr_ref, br_ref, h2_ref, xn3_ref, te_ref, tg_ref, cnt_ref, base_scr):
    d = x_ref.shape[-1]
    hd = d // XATTN_HEADS
    h1 = (x_ref[0] + jnp.dot(od_ref[0], wo1_ref[...], preferred_element_type=F32)
          + jnp.dot(hm_ref[0], wo2_ref[...], preferred_element_type=F32))
    q = jnp.dot(_rms(h1, g2_ref[...]).astype(BF16), wcq_ref[...], preferred_element_type=F32)
    q = (q * (hd ** -0.5)).astype(BF16)
    heads = []
    for h in range(XATTN_HEADS):
        sl = slice(h * hd, (h + 1) * hd)
        s = lax.dot_general(q[:, sl], k_ref[0, :, sl], (((1,), (1,)), ((), ())),
                            preferred_element_type=F32)
        e = jnp.exp(s - jnp.max(s, axis=-1, keepdims=True))
        p = e / jnp.sum(e, axis=-1, keepdims=True)
        heads.append(jnp.dot(p.astype(BF16), v_ref[0, :, sl], preferred_element_type=F32))
    o = jnp.concatenate(heads, axis=1).astype(BF16)
    h2 = h1 + jnp.dot(o, wco_ref[...], preferred_element_type=F32)
    h2_ref[0] = h2
    xn3 = _rms(h2, g3_ref[...])
    xn3_ref[0] = xn3
    x_hi = xn3.astype(BF16)
    x_lo = (xn3 - x_hi.astype(F32)).astype(BF16)
    hh_hl = jnp.dot(x_hi, wr_ref[...], preferred_element_type=F32)
    lh = jnp.dot(x_lo, wr_ref[:, :LANES], preferred_element_type=F32)
    logits = hh_hl[:, :LANES] + (hh_hl[:, LANES:] + lh) + br_ref[...]
    lane = lax.broadcasted_iota(I32, logits.shape, 1)
    lane_f = lane.astype(F32)
    cur = logits
    te = jnp.zeros(logits.shape, F32)
    chosen = jnp.zeros(logits.shape, F32)
    vals, hits = [], []
    for k in range(TOP_K):
        m = jnp.max(cur, axis=-1, keepdims=True)
        idx = jnp.min(jnp.where(cur == m, lane_f, float(LANES)), axis=-1, keepdims=True)
        hit = lane_f == idx
        vals.append(m)
        hits.append(hit)
        te = jnp.where(lane == k, idx, te)
        chosen = jnp.where(hit, 1.0, chosen)
        cur = jnp.where(hit, NEG_BIG * 2.0, cur)
    es = [jnp.exp(v - vals[0]) for v in vals]
    tot = es[0] + es[1] + es[2] + es[3]
    tg = jnp.zeros(logits.shape, F32)
    for k in range(TOP_K):
        tg = jnp.where(lane == k, es[k] / tot, tg)
    tg_ref[0] = tg

    @pl.when((pl.program_id(0) == 0) & (pl.program_id(1) == 0))
    def _():
        base_scr[...] = jnp.zeros_like(base_scr)
    tm = logits.shape[0]
    earlier = (lax.broadcasted_iota(I32, (tm, tm), 1) < lax.broadcasted_iota(I32, (tm, tm), 0)).astype(BF16)
    prior = jnp.dot(earlier, chosen.astype(BF16), preferred_element_type=F32) + base_scr[...]
    for k in range(TOP_K):
        rank = jnp.sum(jnp.where(hits[k], prior, 0.0), axis=-1, keepdims=True)
        te = jnp.where(lane == TOP_K + k, rank, te)
    te_ref[0] = te.astype(I32)
    base_scr[...] = base_scr[...] + jnp.sum(chosen, axis=0, keepdims=True)
    cnt_ref[...] = base_scr[...]


def _post(od, hm, x, kmem, vmem, wo1, wo2, g2, wcq, wco, g3, wr, br, tm):
    b, s, d = x.shape
    w = od.shape[-1]
    m = kmem.shape[1]
    blk = lambda bi, i: (bi, i, 0)
    mem = lambda bi, i: (bi, 0, 0)
    const = lambda bi, i: (0, 0)
    full = lambda a: pl.BlockSpec(a.shape, const)
    return pl.pallas_call(
        _post_kernel,
        grid=(b, s // tm),
        in_specs=[pl.BlockSpec((1, tm, w), blk), pl.BlockSpec((1, tm, w), blk),
                  pl.BlockSpec((1, tm, d), blk), pl.BlockSpec((1, m, d), mem),
                  pl.BlockSpec((1, m, d), mem), full(wo1), full(wo2), full(g2), full(wcq),
                  full(wco), full(g3), full(wr), full(br)],
        out_specs=[pl.BlockSpec((1, tm, d), blk), pl.BlockSpec((1, tm, d), blk),
                   pl.BlockSpec((1, tm, LANES), blk), pl.BlockSpec((1, tm, LANES), blk),
                   pl.BlockSpec((1, LANES), const)],
        out_shape=[jax.ShapeDtypeStruct((b, s, d), F32), jax.ShapeDtypeStruct((b, s, d), F32),
                   jax.ShapeDtypeStruct((b, s, LANES), I32), jax.ShapeDtypeStruct((b, s, LANES), F32),
                   jax.ShapeDtypeStruct((1, LANES), F32)],
        scratch_shapes=[pltpu.VMEM((1, LANES), F32)],
        compiler_params=_params(("arbitrary", "arbitrary")),
        name="post_mixer",
    )(od, hm, x, kmem, vmem, wo1, wo2, g2, wcq, wco, g3, wr, br)


def _split_kernel(w_ref, p_ref, wg_ref, wl_ref):
    wb = w_ref[0].astype(BF16)
    two = 2 * LANES
    for g in range(wb.shape[1] // two):
        r = jnp.dot(wb[:, g * two:(g + 1) * two], p_ref[...], preferred_element_type=F32)
        wg_ref[0, :, g * LANES:(g + 1) * LANES] = r[:, :LANES].astype(BF16)
        wl_ref[0, :, g * LANES:(g + 1) * LANES] = r[:, LANES:].astype(BF16)


def _split_w1(w1, rb):
    e, d, de2 = w1.shape
    two = 2 * LANES
    src = jnp.arange(two)[:, None]
    dst = jnp.arange(two)[None, :]
    perm = jnp.where(dst < LANES, src == 2 * dst, src == 2 * (dst - LANES) + 1).astype(BF16)
    blk = lambda ei, r: (ei, r, 0)
    return pl.pallas_call(
        _split_kernel,
        grid=(e, d // rb),
        in_specs=[pl.BlockSpec((1, rb, de2), blk), pl.BlockSpec((two, two), lambda ei, r: (0, 0))],
        out_specs=[pl.BlockSpec((1, rb, de2 // 2), blk)] * 2,
        out_shape=[jax.ShapeDtypeStruct((e, d, de2 // 2), BF16)] * 2,
        compiler_params=_params(("parallel", "parallel")),
        name="split_w1",
    )(w1, perm)


def _dispatch_kernel(nz_ref, dest_ref, x_ref, xs_hbm, zbuf, sem, zsem, *, tb, bm, n_blocks):
    i = pl.program_id(0)

    @pl.when(i == 0)
    def _():
        zbuf[...] = jnp.zeros_like(zbuf)

        def zstart(blk, carry):
            @pl.when(nz_ref[blk] != 0)
            def _():
                pltpu.make_async_copy(zbuf, xs_hbm.at[pl.ds(pl.multiple_of(blk * bm, bm), bm), :], zsem).start()
            return carry
        lax.fori_loop(0, n_blocks, zstart, 0)

        def zwait(blk, carry):
            @pl.when(nz_ref[blk] != 0)
            def _():
                pltpu.make_async_copy(zbuf, xs_hbm.at[pl.ds(0, bm), :], zsem).wait()
            return carry
        lax.fori_loop(0, n_blocks, zwait, 0)

    def issue(t, carry):
        for k in range(TOP_K):
            dst = dest_ref[0, 0, t * TOP_K + k]
            pltpu.make_async_copy(x_ref.at[pl.ds(t, 1), :], xs_hbm.at[pl.ds(dst, 1), :], sem).start(priority=k % 2)
        return carry
    lax.fori_loop(0, tb, issue, 0)
    for k in range(TOP_K):
        pltpu.make_async_copy(x_ref, xs_hbm.at[pl.ds(0, tb), :], sem).wait()


def _dispatch(needs_zero, dest_blocks, xn3, bm, tb):
    t, d = xn3.shape
    n_blocks = needs_zero.shape[0]
    grid_spec = pltpu.PrefetchScalarGridSpec(
        num_scalar_prefetch=1,
        grid=(t // tb,),
        in_specs=[pl.BlockSpec((1, 1, tb * TOP_K), lambda i, nz: (i, 0, 0), memory_space=pltpu.SMEM),
                  pl.BlockSpec((tb, d), lambda i, nz: (i, 0))],
        out_specs=pl.BlockSpec(memory_space=pl.ANY),
        scratch_shapes=[pltpu.VMEM((bm, d), F32), pltpu.SemaphoreType.DMA, pltpu.SemaphoreType.DMA],
    )
    return pl.pallas_call(
        functools.partial(_dispatch_kernel, tb=tb, bm=bm, n_blocks=n_blocks),
        grid_spec=grid_spec,
        out_shape=jax.ShapeDtypeStruct((n_blocks * bm, d), F32),
        compiler_params=_params(("arbitrary",)),
        name="dispatch",
    )(needs_zero, dest_blocks, xn3)


def _expert_kernel(be_ref, na_ref, x_ref, w1g_ref, w1l_ref, b1g_ref, b1l_ref, w2_ref, b2_ref, y_ref):
    i = pl.program_id(0)

    @pl.when(i < na_ref[0])
    def _():
        xb = x_ref[...].astype(BF16)
        glu = jnp.dot(xb, w1g_ref[0], preferred_element_type=F32) + b1g_ref[0]
        lin = jnp.dot(xb, w1l_ref[0], preferred_element_type=F32) + b1l_ref[0]
        glu = jnp.minimum(glu, SWIGLU_LIMIT)
        lin = jnp.clip(lin, -SWIGLU_LIMIT, SWIGLU_LIMIT)
        act = glu * jax.nn.sigmoid(SWIGLU_ALPHA * glu) * (lin + 1.0)
        y_ref[...] = jnp.dot(act.astype(BF16), w2_ref[0], preferred_element_type=F32) + b2_ref[0]

    @pl.when(i >= na_ref[0])
    def _():
        y_ref[...] = jnp.zeros_like(y_ref)


def _experts(blk_expert, n_active, xs, w1g, w1l, b1g, b1l, w2, b2, bm):
    n_pad, d = xs.shape
    n_blocks = n_pad // bm
    de = w1g.shape[-1]
    last = lambda i, na: jnp.minimum(i, na[0] - 1)
    wmap = lambda i, be, na: (be[last(i, na)], 0, 0)
    grid_spec = pltpu.PrefetchScalarGridSpec(
        num_scalar_prefetch=2,
        grid=(n_blocks,),
        in_specs=[pl.BlockSpec((bm, d), lambda i, be, na: (last(i, na), 0)),
                  pl.BlockSpec((1, d, de), wmap), pl.BlockSpec((1, d, de), wmap),
                  pl.BlockSpec((1, 1, de), wmap), pl.BlockSpec((1, 1, de), wmap),
                  pl.BlockSpec((1, de, d), wmap), pl.BlockSpec((1, 1, d), wmap)],
        out_specs=pl.BlockSpec((bm, d), lambda i, be, na: (i, 0)),
    )
    return pl.pallas_call(
        _expert_kernel,
        grid_spec=grid_spec,
        out_shape=jax.ShapeDtypeStruct((n_pad, d), F32),
        compiler_params=_params(("arbitrary",)),
        name="experts",
    )(blk_expert, n_active, xs, w1g, w1l, b1g, b1l, w2, b2)


def _combine_kernel(dest_ref, y_hbm, h2_ref, tg_ref, g_ref, o_ref, buf, sem, *, tb, final_norm):
    def issue(t, carry):
        for k in range(TOP_K):
            dst = dest_ref[0, 0, t * TOP_K + k]
            pltpu.make_async_copy(y_hbm.at[pl.ds(dst, 1), :], buf.at[k, pl.ds(t, 1), :], sem).start(priority=k % 2)
        return carry
    lax.fori_loop(0, tb, issue, 0)
    for k in range(TOP_K):
        pltpu.make_async_copy(y_hbm.at[pl.ds(0, tb), :], buf.at[k], sem).wait()

    acc = h2_ref[...]
    for k in range(TOP_K):
        acc = acc + buf[k] * tg_ref[:, k:k + 1]
    o_ref[...] = _rms(acc, g_ref[...]) if final_norm else acc


def _combine(dest_blocks, y, h2, tg, g, tb, final_norm):
    t, d = h2.shape
    row = lambda i: (i, 0)
    return pl.pallas_call(
        functools.partial(_combine_kernel, tb=tb, final_norm=final_norm),
        grid=(t // tb,),
        in_specs=[pl.BlockSpec((1, 1, tb * TOP_K), lambda i: (i, 0, 0), memory_space=pltpu.SMEM),
                  pl.BlockSpec(memory_space=pl.ANY),
                  pl.BlockSpec((tb, d), row), pl.BlockSpec((tb, LANES), row),
                  pl.BlockSpec((1, d), lambda i: (0, 0))],
        out_specs=pl.BlockSpec((tb, d), row),
        out_shape=jax.ShapeDtypeStruct((t, d), F32),
        scratch_shapes=[pltpu.VMEM((TOP_K, tb, d), F32), pltpu.SemaphoreType.DMA],
        compiler_params=_params(("arbitrary",)),
        name="combine",
    )(dest_blocks, y, h2, tg, g)


def _route(top_e, rank, counts, n_experts, bm):
    a = top_e.size
    padded = (counts + bm - 1) // bm * bm
    pend = jnp.cumsum(padded)
    pstart = pend - padded
    dest = (jnp.take(pstart, top_e) + rank).astype(I32)
    n_blocks = -(-a // bm) + n_experts
    blk_lo = jnp.arange(n_blocks, dtype=I32) * bm
    blk_expert = jnp.minimum(jnp.sum(blk_lo[:, None] >= pend[None, :], axis=1), n_experts - 1).astype(I32)
    n_active = (pend[-1] // bm).astype(I32)
    has_pad = jnp.any((blk_lo[:, None] + bm) == pend[None, :], axis=1)
    needs_zero = (has_pad | (jnp.arange(n_blocks) >= n_active)).astype(I32)
    return dest, blk_expert, n_active.reshape(1), needs_zero


def _pad_cols(w, n):
    return jnp.pad(w, ((0, 0), (0, n - w.shape[1])))


def _layer(h, mem, l, p):
    b, s, d = h.shape
    t = b * s
    lambda_init = 0.8 - 0.6 * math.exp(-0.3 * l)
    n_main = 6 * GROUP
    w_in = p['w_in']
    w_main = w_in[:, :n_main].astype(BF16)
    w_i = _pad_cols(w_in[:, n_main:n_main + MLSTM_HEADS], LANES).astype(BF16)
    w_f = _pad_cols(w_in[:, n_main + MLSTM_HEADS:], LANES).astype(BF16)
    dq, dk, dv, mqk, mv, mo, gi, gf = _in_proj(h.reshape(t, d), p['norm_mix_g'].reshape(1, d),
                                               w_main, w_i, w_f, tm=min(512, t))
    r3 = lambda a: a.reshape(b, s, a.shape[-1])
    vec = lambda a: a.reshape(1, -1)
    od = _diff_attention(r3(dq), r3(dk), r3(dv), vec(p['lambda_q1']), vec(p['lambda_k1']),
                         vec(p['lambda_q2']), vec(p['lambda_k2']), vec(p['diff_norm_g']),
                         lambda_init, bq=min(512, s))
    hm = _mlstm(r3(mqk), r3(mv), r3(gi), r3(gf), r3(mo), p['conv_w'], vec(p['conv_b']),
                _pad_cols(vec(p['b_igate']), LANES), _pad_cols(vec(p['b_fgate']), LANES),
                vec(p['mlstm_norm_g']), bb=2 if b % 2 == 0 else 1)
    kmem, vmem = _kv_proj(mem, vec(p['norm_mem_g']), p['w_ckv'].astype(BF16))
    w_out = p['w_out'].astype(BF16)
    n_experts = p['w_router'].shape[1]
    wr = _pad_cols(p['w_router'], LANES)
    wr_hi = wr.astype(BF16)
    wr = jnp.concatenate([wr_hi, (wr - wr_hi.astype(F32)).astype(BF16)], axis=1)
    br =jnp.concatenate([vec(p['b_router']), jnp.full((1, LANES - n_experts), NEG_BIG, F32)], axis=1)
    h2, xn3, te, tg, cnt = _post(od, hm, h, kmem, vmem, w_out[:GROUP], w_out[GROUP:],
                                 vec(p['norm_xattn_g']), p['w_cq'].astype(BF16), p['w_co'].astype(BF16),
                                 vec(p['norm_ffn_g']), wr, br, tm=min(512, s))
    te = te.reshape(t, LANES)
    dest, blk_expert, n_active, needs_zero = _route(te[:, :TOP_K], te[:, TOP_K:2 * TOP_K],
                                                    cnt[0, :n_experts].astype(I32), n_experts, MOE_BLOCK)
    tb_d = min(512, t)
    xs = _dispatch(needs_zero, dest.reshape(t // tb_d, 1, tb_d * TOP_K), xn3.reshape(t, d), MOE_BLOCK, tb_d)
    w1g, w1l = _split_w1(p['w1'], rb=256)
    y = _experts(blk_expert, n_active, xs, w1g, w1l,
                 p['b1'][:, None, 0::2], p['b1'][:, None, 1::2],
                 p['w2'].astype(BF16), p['b2'][:, None, :], MOE_BLOCK)
    tb = min(256, t)
    return y, dest.reshape(t // tb, 1, tb * TOP_K), h2.reshape(t, d), tg.reshape(t, LANES), tb


def kernel(x, mem, norm_mix_g, w_in, conv_w, conv_b, b_igate, b_fgate, mlstm_norm_g, lambda_q1, lambda_k1, lambda_q2, lambda_k2, diff_norm_g, w_out, norm_xattn_g, norm_mem_g, w_cq, w_ckv, w_co, norm_ffn_g, w_router, b_router, w1, b1, w2, b2, norm_final_g):
    stacked = dict(norm_mix_g=norm_mix_g, w_in=w_in, conv_w=conv_w, conv_b=conv_b, b_igate=b_igate,
                   b_fgate=b_fgate, mlstm_norm_g=mlstm_norm_g, lambda_q1=lambda_q1, lambda_k1=lambda_k1,
                   lambda_q2=lambda_q2, lambda_k2=lambda_k2, diff_norm_g=diff_norm_g, w_out=w_out,
                   norm_xattn_g=norm_xattn_g, norm_mem_g=norm_mem_g, w_cq=w_cq, w_ckv=w_ckv, w_co=w_co,
                   norm_ffn_g=norm_ffn_g, w_router=w_router, b_router=b_router, w1=w1, b1=b1, w2=w2, b2=b2)
    depth = w_in.shape[0]
    b, s, d = x.shape
    h = x
    for l in range(depth):
        p = {k: v[l] for k, v in stacked.items()}
        y, dest_blocks, h2, tg, tb = _layer(h, mem, l, p)
        h = _combine(dest_blocks, y, h2, tg, norm_final_g.reshape(1, d), tb,
                     final_norm=l == depth - 1).reshape(b, s, d)
    return h
```

```python
import functools
import math

import jax
import jax.numpy as jnp
from jax import lax
from jax.experimental import pallas as pl
from jax.experimental.pallas import tpu as pltpu

F32 = jnp.float32
BF16 = jnp.bfloat16
I32 = jnp.int32

RMS_EPS = 1e-5
LANES = 128
VMEM_LIMIT = 56 * 1024 * 1024

DIFF_HEADS = 4
DIFF_QK_DIM = 64
MLSTM_HEADS = 4
MLSTM_QK_DIM = 64
CONV_K = 4
CHUNK = 128
XATTN_HEADS = 4
TOP_K = 4
SWIGLU_ALPHA = 1.702
SWIGLU_LIMIT = 7.0
MOE_BLOCK = 512
GROUP = 512
NEG_BIG = -1e30


def _rms(x, g):
    return x * lax.rsqrt(jnp.mean(x * x, axis=-1, keepdims=True) + RMS_EPS) * g


def _store_row_tiles(ref, m):
    rows, d = m.shape
    sub = d // LANES
    for c in range(sub):
        ref[pl.ds(c, rows, stride=sub), :] = m[:, c * LANES:(c + 1) * LANES]


def _load_row_tiles(ref, sub):
    rows = ref.shape[0] // sub
    return jnp.concatenate([ref[pl.ds(c, rows, stride=sub), :] for c in range(sub)], axis=1)


def _params(sem, vmem=VMEM_LIMIT, flags=None):
    return pltpu.CompilerParams(dimension_semantics=sem, vmem_limit_bytes=vmem, flags=flags)


def _inproj_kernel(x_ref, g_ref, w_ref, wi_ref, wf_ref,
                   dq_ref, dk_ref, dv_ref, mqk_ref, mv_ref, mo_ref, gi_ref, gf_ref):
    xb = _rms(x_ref[...], g_ref[...]).astype(BF16)
    for n, o_ref in enumerate((dq_ref, dk_ref, dv_ref, mqk_ref, mv_ref, mo_ref)):
        o_ref[...] = jnp.dot(xb, w_ref[:, n * GROUP:(n + 1) * GROUP],
                             preferred_element_type=F32).astype(o_ref.dtype)
    gi_ref[...] = jnp.dot(xb, wi_ref[...], preferred_element_type=F32)
    gf_ref[...] = jnp.dot(xb, wf_ref[...], preferred_element_type=F32)


def _in_proj(x2, g, w_main, w_i, w_f, tm):
    t, d = x2.shape
    row = lambda i: (i, 0)
    const = lambda i: (0, 0)
    out_dtypes = (BF16, BF16, BF16, F32, BF16, F32)
    return pl.pallas_call(
        _inproj_kernel,
        grid=(t // tm,),
        in_specs=[pl.BlockSpec((tm, d), row), pl.BlockSpec((1, d), const),
                  pl.BlockSpec(w_main.shape, const), pl.BlockSpec(w_i.shape, const),
                  pl.BlockSpec(w_f.shape, const)],
        out_specs=[pl.BlockSpec((tm, GROUP), row)] * 6 + [pl.BlockSpec((tm, LANES), row)] * 2,
        out_shape=[jax.ShapeDtypeStruct((t, GROUP), dt) for dt in out_dtypes]
        + [jax.ShapeDtypeStruct((t, LANES), F32)] * 2,
        compiler_params=_params(("parallel",)),
        name="in_proj",
    )(x2, g, w_main, w_i, w_f)


def _diffattn_kernel(q_ref, k_ref, v_ref, lq1_ref, lk1_ref, lq2_ref, lk2_ref, g_ref, o_ref,
                     *, bq, lambda_init):
    i = pl.program_id(2)
    lane = lax.broadcasted_iota(I32, (1, LANES), 1)
    lo = lane < DIFF_QK_DIM
    q = q_ref[0] * jnp.asarray(DIFF_QK_DIM ** -0.5, BF16)
    zero = jnp.zeros_like(q)
    qs = (jnp.where(lo, q, zero), jnp.where(lo, zero, q))
    row = lax.broadcasted_iota(I32, (bq, bq), 0)
    col = lax.broadcasted_iota(I32, (bq, bq), 1)
    causal = col <= row

    def step(j, carry, masked):
        start = pl.multiple_of(j * bq, bq)
        kj = k_ref[0, pl.ds(start, bq), :]
        vj = v_ref[0, pl.ds(start, bq), :]
        new = []
        for qm, (m, l, a) in zip(qs, carry):
            s = lax.dot_general(qm, kj, (((1,), (1,)), ((), ())), preferred_element_type=F32)
            if masked:
                s = jnp.where(causal, s, -jnp.inf)
            mn = jnp.maximum(m, jnp.max(s, axis=-1, keepdims=True))
            p = jnp.exp(s - mn)
            alpha = jnp.exp(m - mn)
            l = alpha * l + jnp.sum(p, axis=-1, keepdims=True)
            a = alpha * a + jnp.dot(p.astype(BF16), vj, preferred_element_type=F32)
            new.append((mn, l, a))
        return tuple(new)

    init_one = (jnp.full((bq, 1), -jnp.inf, F32), jnp.zeros((bq, 1), F32), jnp.zeros((bq, LANES), F32))
    carry = lax.fori_loop(0, i, lambda j, c: step(j, c, False), (init_one, init_one))
    (_, l1, a1), (_, l2, a2) = step(i, carry, True)

    lam = (jnp.exp(jnp.sum(lq1_ref[...] * lk1_ref[...], axis=-1, keepdims=True))
           - jnp.exp(jnp.sum(lq2_ref[...] * lk2_ref[...], axis=-1, keepdims=True)) + lambda_init)
    od = a1 / l1 - lam * (a2 / l2)
    o_ref[0] = (_rms(od, g_ref[...]) * (1.0 - lambda_init)).astype(o_ref.dtype)


def _diff_attention(dq, dk, dv, lq1, lk1, lq2, lk2, g, lambda_init, bq):
    b, s, w = dq.shape
    qmap = lambda bi, h, i: (bi, i, h)
    kvmap = lambda bi, h, i: (bi, 0, h)
    vec = lambda bi, h, i: (0, 0)
    return pl.pallas_call(
        functools.partial(_diffattn_kernel, bq=bq, lambda_init=lambda_init),
        grid=(b, DIFF_HEADS, s // bq),
        in_specs=[pl.BlockSpec((1, bq, LANES), qmap), pl.BlockSpec((1, s, LANES), kvmap),
                  pl.BlockSpec((1, s, LANES), kvmap)]
        + [pl.BlockSpec((1, DIFF_QK_DIM), vec)] * 4
        + [pl.BlockSpec((1, LANES), lambda bi, h, i: (0, h))],
        out_specs=pl.BlockSpec((1, bq, LANES), qmap),
        out_shape=jax.ShapeDtypeStruct((b, s, w), BF16),
        compiler_params=_params(("parallel", "parallel", "arbitrary")),
        name="diff_attn",
    )(dq, dk, dv, lq1, lk1, lq2, lk2, g)


def _log_sigmoid(x):
    return -(jnp.maximum(-x, 0.0) + jnp.log1p(jnp.exp(-jnp.abs(x))))


def _mlstm_kernel(mqk_ref, mv_ref, gi_ref, gf_ref, mo_ref, cw_ref, cb_ref, bi_ref, bf_ref, ng_ref,
                  o_ref, conv_scr, c_scr, m_scr, *, bb):
    c = pl.program_id(1)
    L = CHUNK
    dk = MLSTM_QK_DIM
    nq = MLSTM_HEADS * dk
    tail = 8

    @pl.when(c == 0)
    def _():
        conv_scr[:, 0:tail, :] = jnp.zeros((bb, tail, 2 * nq), F32)
        c_scr[...] = jnp.zeros_like(c_scr)
        m_scr[...] = jnp.zeros_like(m_scr)

    row = lax.broadcasted_iota(I32, (L, L), 0)
    col = lax.broadcasted_iota(I32, (L, L), 1)
    causal = col <= row
    tril = causal.astype(F32)
    lane = lax.broadcasted_iota(I32, (1, LANES), 1)
    sub = lax.broadcasted_iota(I32, (LANES, 1), 0)
    ones_col = jnp.where(lane == 0, 1.0, 0.0).astype(BF16) * jnp.ones((L, 1), BF16)

    for b in range(bb):
        conv_scr[b, tail:tail + L, :] = mqk_ref[b]
        y = cb_ref[...]
        for j in range(CONV_K):
            y = y + conv_scr[b, pl.ds(tail - (CONV_K - 1) + j, L), :] * cw_ref[j:j + 1, :]
        conv_scr[b, 0:tail, :] = conv_scr[b, L:L + tail, :]
        qk = y * jax.nn.sigmoid(y)

        ig = gi_ref[b] + bi_ref[...]
        lf = _log_sigmoid(gf_ref[b] + bf_ref[...])
        bcum = jnp.dot(tril, lf, preferred_element_type=F32, precision=lax.Precision.HIGHEST)
        a_t = (ig - bcum).T

        for pair in range(MLSTM_HEADS // 2):
            q_pair = qk[:, pair * LANES:(pair + 1) * LANES] * (dk ** -0.5)
            k_pair = qk[:, nq + pair * LANES:nq + (pair + 1) * LANES]
            kt_pair = k_pair.T
            c_old = c_scr[b, pair]
            c_bf = c_old.astype(BF16)
            upd = jnp.zeros_like(c_old)
            decays = []
            for hh in range(2):
                h = 2 * pair + hh
                sel = (lane >= hh * dk) & (lane < (hh + 1) * dk)
                qm = jnp.where(sel, q_pair, 0.0).astype(BF16)
                km = jnp.where(sel, k_pair, 0.0).astype(BF16)
                selr = (sub >= hh * dk) & (sub < (hh + 1) * dk)
                ktm = jnp.where(selr, kt_pair, 0.0).astype(BF16)
                v_h = mv_ref[b, :, h * LANES:(h + 1) * LANES]
                v_aug = jnp.concatenate([v_h, ones_col], axis=1)

                m_st = m_scr[b, h, 0:1, 0:1]
                bc = bcum[:, h:h + 1]
                ic = ig[:, h:h + 1]
                a_m = jnp.where(causal, a_t[h:h + 1, :], -jnp.inf)
                inter = bc + m_st
                m_t = jnp.maximum(inter, bc + jnp.max(a_m, axis=-1, keepdims=True))
                qkt = lax.dot_general(qm, km, (((1,), (1,)), ((), ())), preferred_element_type=F32)
                w = qkt * jnp.exp(a_m + (bc - m_t))
                g = jnp.exp(inter - m_t)
                num_aug = (g * jnp.dot(qm, c_bf, preferred_element_type=F32)
                           + jnp.dot(w.astype(BF16), v_aug, preferred_element_type=F32))
                num = num_aug[:, :LANES]
                den = num_aug[:, LANES:LANES + 1]
                h_t = num / jnp.maximum(jnp.abs(den), jnp.exp(-m_t))

                b_last = bc[L - 1:L, :]
                gs = b_last - bc + ic
                m_new = jnp.maximum(b_last + m_st, jnp.max(gs, axis=0, keepdims=True))
                decays.append(jnp.exp(b_last + m_st - m_new))
                ws = jnp.exp(gs - m_new)
                wsv = (ws * v_aug.astype(F32)).astype(BF16)
                upd = upd + jnp.dot(ktm, wsv, preferred_element_type=F32)
                m_scr[b, h, 0:1, :] = jnp.broadcast_to(m_new, (1, LANES))

                hn = _rms(h_t, ng_ref[:, h * LANES:(h + 1) * LANES])
                gate = jax.nn.sigmoid(mo_ref[b, :, h * LANES:(h + 1) * LANES])
                o_ref[b, :, h * LANES:(h + 1) * LANES] = (hn * gate).astype(o_ref.dtype)
            d_rows = jnp.where(sub < dk, decays[0], decays[1])
            c_scr[b, pair] = d_rows * c_old + upd


def _mlstm(mqk, mv, gi, gf, mo, conv_w, conv_b, b_i, b_f, norm_g, bb):
    b, s, w = mqk.shape
    blk = lambda bi, c: (bi, c, 0)
    const = lambda bi, c: (0, 0)
    return pl.pallas_call(
        functools.partial(_mlstm_kernel, bb=bb),
        grid=(b // bb, s // CHUNK),
        in_specs=[pl.BlockSpec((bb, CHUNK, w), blk), pl.BlockSpec((bb, CHUNK, w), blk),
                  pl.BlockSpec((bb, CHUNK, LANES), blk), pl.BlockSpec((bb, CHUNK, LANES), blk),
                  pl.BlockSpec((bb, CHUNK, w), blk),
                  pl.BlockSpec(conv_w.shape, const), pl.BlockSpec(conv_b.shape, const),
                  pl.BlockSpec(b_i.shape, const), pl.BlockSpec(b_f.shape, const),
                  pl.BlockSpec(norm_g.shape, const)],
        out_specs=pl.BlockSpec((bb, CHUNK, w), blk),
        out_shape=jax.ShapeDtypeStruct((b, s, w), BF16),
        scratch_shapes=[pltpu.VMEM((bb, CHUNK + 8, w), F32),
                        pltpu.VMEM((bb, MLSTM_HEADS // 2, LANES, 2 * LANES), F32),
                        pltpu.VMEM((bb, MLSTM_HEADS, 8, LANES), F32)],
        compiler_params=_params(("parallel", "arbitrary")),
        name="mlstm",
    )(mqk, mv, gi, gf, mo, conv_w, conv_b, b_i, b_f, norm_g)


def _kvproj_kernel(mem_ref, g_ref, w_ref, k_ref, v_ref):
    d = mem_ref.shape[-1]
    mb = _rms(mem_ref[0], g_ref[...]).astype(BF16)
    k_ref[0] = jnp.dot(mb, w_ref[:, :d], preferred_element_type=F32).astype(k_ref.dtype)
    v_ref[0] = jnp.dot(mb, w_ref[:, d:], preferred_element_type=F32).astype(v_ref.dtype)


def _kv_proj(mem, g, w_ckv):
    b, m, d = mem.shape
    blk = lambda bi: (bi, 0, 0)
    const = lambda bi: (0, 0)
    return pl.pallas_call(
        _kvproj_kernel,
        grid=(b,),
        in_specs=[pl.BlockSpec((1, m, d), blk), pl.BlockSpec((1, d), const),
                  pl.BlockSpec(w_ckv.shape, const)],
        out_specs=[pl.BlockSpec((1, m, d), blk)] * 2,
        out_shape=[jax.ShapeDtypeStruct((b, m, d), BF16)] * 2,
        compiler_params=_params(("parallel",)),
        name="kv_proj",
    )(mem, g, w_ckv)


def _post_kernel(od_ref, hm_ref, x_ref, k_ref, v_ref, wo1_ref, wo2_ref, g2_ref, wcq_ref, wco_ref,
                 g3_ref, wr_ref, br_ref, h2_ref, xn3_ref, te_ref, tg_ref, cnt_ref, base_scr):
    d = x_ref.shape[-1]
    hd = d // XATTN_HEADS
    h1 = (x_ref[0] + jnp.dot(od_ref[0], wo1_ref[...], preferred_element_type=F32)
          + jnp.dot(hm_ref[0], wo2_ref[...], preferred_element_type=F32))
    q = jnp.dot(_rms(h1, g2_ref[...]).astype(BF16), wcq_ref[...], preferred_element_type=F32)
    q = (q * (hd ** -0.5)).astype(BF16)
    heads = []
    for h in range(XATTN_HEADS):
        sl = slice(h * hd, (h + 1) * hd)
        s = lax.dot_general(q[:, sl], k_ref[0, :, sl], (((1,), (1,)), ((), ())),
                            preferred_element_type=F32)
        e = jnp.exp(s - jnp.max(s, axis=-1, keepdims=True))
        p = e / jnp.sum(e, axis=-1, keepdims=True)
        heads.append(jnp.dot(p.astype(BF16), v_ref[0, :, sl], preferred_element_type=F32))
    o = jnp.concatenate(heads, axis=1).astype(BF16)
    h2 = h1 + jnp.dot(o, wco_ref[...], preferred_element_type=F32)
    h2_ref[0] = h2
    xn3 = _rms(h2, g3_ref[...])
    _store_row_tiles(xn3_ref.at[0], xn3)
    x_hi = xn3.astype(BF16)
    x_lo = (xn3 - x_hi.astype(F32)).astype(BF16)
    hh_hl = jnp.dot(x_hi, wr_ref[...], preferred_element_type=F32)
    lh = jnp.dot(x_lo, wr_ref[:, :LANES], preferred_element_type=F32)
    logits = hh_hl[:, :LANES] + (hh_hl[:, LANES:] + lh) + br_ref[...]
    lane = lax.broadcasted_iota(I32, logits.shape, 1)
    lane_f = lane.astype(F32)
    cur = logits
    te = jnp.zeros(logits.shape, F32)
    chosen = jnp.zeros(logits.shape, F32)
    vals, hits = [], []
    for k in range(TOP_K):
        m = jnp.max(cur, axis=-1, keepdims=True)
        idx = jnp.min(jnp.where(cur == m, lane_f, float(LANES)), axis=-1, keepdims=True)
        hit = lane_f == idx
        vals.append(m)
        hits.append(hit)
        te = jnp.where(lane == k, idx, te)
        chosen = jnp.where(hit, 1.0, chosen)
        cur = jnp.where(hit, NEG_BIG * 2.0, cur)
    es = [jnp.exp(v - vals[0]) for v in vals]
    tot = es[0] + es[1] + es[2] + es[3]
    tg = jnp.zeros(logits.shape, F32)
    for k in range(TOP_K):
        tg = jnp.where(lane == k, es[k] / tot, tg)
    tg_ref[0] = tg

    @pl.when((pl.program_id(0) == 0) & (pl.program_id(1) == 0))
    def _():
        base_scr[...] = jnp.zeros_like(base_scr)
    tm = logits.shape[0]
    earlier = (lax.broadcasted_iota(I32, (tm, tm), 1) < lax.broadcasted_iota(I32, (tm, tm), 0)).astype(BF16)
    prior = jnp.dot(earlier, chosen.astype(BF16), preferred_element_type=F32) + base_scr[...]
    for k in range(TOP_K):
        rank = jnp.sum(jnp.where(hits[k], prior, 0.0), axis=-1, keepdims=True)
        te = jnp.where(lane == TOP_K + k, rank, te)
    te_ref[0] = te.astype(I32)
    base_scr[...] = base_scr[...] + jnp.sum(chosen, axis=0, keepdims=True)
    cnt_ref[...] = base_scr[...]


def _post(od, hm, x, kmem, vmem, wo1, wo2, g2, wcq, wco, g3, wr, br, tm):
    b, s, d = x.shape
    w = od.shape[-1]
    m = kmem.shape[1]
    blk = lambda bi, i: (bi, i, 0)
    mem = lambda bi, i: (bi, 0, 0)
    const = lambda bi, i: (0, 0)
    full = lambda a: pl.BlockSpec(a.shape, const)
    return pl.pallas_call(
        _post_kernel,
        grid=(b, s // tm),
        in_specs=[pl.BlockSpec((1, tm, w), blk), pl.BlockSpec((1, tm, w), blk),
                  pl.BlockSpec((1, tm, d), blk), pl.BlockSpec((1, m, d), mem),
                  pl.BlockSpec((1, m, d), mem), full(wo1), full(wo2), full(g2), full(wcq),
                  full(wco), full(g3), full(wr), full(br)],
        out_specs=[pl.BlockSpec((1, tm, d), blk),
                   pl.BlockSpec((1, tm * (d // LANES), LANES), blk),
                   pl.BlockSpec((1, tm, LANES), blk), pl.BlockSpec((1, tm, LANES), blk),
                   pl.BlockSpec((1, LANES), const)],
        out_shape=[jax.ShapeDtypeStruct((b, s, d), F32), jax.ShapeDtypeStruct((b, s * (d // LANES), LANES), F32),
                   jax.ShapeDtypeStruct((b, s, LANES), I32), jax.ShapeDtypeStruct((b, s, LANES), F32),
                   jax.ShapeDtypeStruct((1, LANES), F32)],
        scratch_shapes=[pltpu.VMEM((1, LANES), F32)],
        compiler_params=_params(("arbitrary", "arbitrary")),
        name="post_mixer",
    )(od, hm, x, kmem, vmem, wo1, wo2, g2, wcq, wco, g3, wr, br)


def _split_kernel(w_ref, p_ref, wg_ref, wl_ref):
    wb = w_ref[0].astype(BF16)
    two = 2 * LANES
    for g in range(wb.shape[1] // two):
        r = jnp.dot(wb[:, g * two:(g + 1) * two], p_ref[...], preferred_element_type=F32)
        wg_ref[0, :, g * LANES:(g + 1) * LANES] = r[:, :LANES].astype(BF16)
        wl_ref[0, :, g * LANES:(g + 1) * LANES] = r[:, LANES:].astype(BF16)


def _split_w1(w1, rb):
    e, d, de2 = w1.shape
    two = 2 * LANES
    src = jnp.arange(two)[:, None]
    dst = jnp.arange(two)[None, :]
    perm = jnp.where(dst < LANES, src == 2 * dst, src == 2 * (dst - LANES) + 1).astype(BF16)
    blk = lambda ei, r: (ei, r, 0)
    return pl.pallas_call(
        _split_kernel,
        grid=(e, d // rb),
        in_specs=[pl.BlockSpec((1, rb, de2), blk), pl.BlockSpec((two, two), lambda ei, r: (0, 0))],
        out_specs=[pl.BlockSpec((1, rb, de2 // 2), blk)] * 2,
        out_shape=[jax.ShapeDtypeStruct((e, d, de2 // 2), BF16)] * 2,
        compiler_params=_params(("parallel", "parallel")),
        name="split_w1",
    )(w1, perm)


def _dispatch_kernel(nz_ref, dest_ref, x_ref, xs_hbm, zbuf, sem, zsem, *, tb, bm, n_blocks, sub):
    i = pl.program_id(0)
    tile = lambda r: pl.ds(pl.multiple_of(r * sub, sub), sub)

    @pl.when(i == 0)
    def _():
        zbuf[...] = jnp.zeros_like(zbuf)

        def zstart(blk, carry):
            @pl.when(nz_ref[blk] != 0)
            def _():
                pltpu.make_async_copy(zbuf, xs_hbm.at[pl.ds(pl.multiple_of(blk * bm * sub, bm * sub), bm * sub)],
                                      zsem).start()
            return carry
        lax.fori_loop(0, n_blocks, zstart, 0)

        def zwait(blk, carry):
            @pl.when(nz_ref[blk] != 0)
            def _():
                pltpu.make_async_copy(zbuf, xs_hbm.at[pl.ds(0, bm * sub)], zsem).wait()
            return carry
        lax.fori_loop(0, n_blocks, zwait, 0)

    def issue(t, carry):
        for k in range(TOP_K):
            dst = dest_ref[0, 0, t * TOP_K + k]
            pltpu.make_async_copy(x_ref.at[tile(t)], xs_hbm.at[tile(dst)], sem).start(priority=k % 2)
        return carry
    lax.fori_loop(0, tb, issue, 0)
    for k in range(TOP_K):
        pltpu.make_async_copy(x_ref, xs_hbm.at[pl.ds(0, tb * sub)], sem).wait()


def _dispatch(needs_zero, dest_blocks, xn3, sub, bm, tb):
    t = xn3.shape[0] // sub
    n_blocks = needs_zero.shape[0]
    grid_spec = pltpu.PrefetchScalarGridSpec(
        num_scalar_prefetch=1,
        grid=(t // tb,),
        in_specs=[pl.BlockSpec((1, 1, tb * TOP_K), lambda i, nz: (i, 0, 0), memory_space=pltpu.SMEM),
                  pl.BlockSpec((tb * sub, LANES), lambda i, nz: (i, 0))],
        out_specs=pl.BlockSpec(memory_space=pl.ANY),
        scratch_shapes=[pltpu.VMEM((bm * sub, LANES), F32), pltpu.SemaphoreType.DMA, pltpu.SemaphoreType.DMA],
    )
    return pl.pallas_call(
        functools.partial(_dispatch_kernel, tb=tb, bm=bm, n_blocks=n_blocks, sub=sub),
        grid_spec=grid_spec,
        out_shape=jax.ShapeDtypeStruct((n_blocks * bm * sub, LANES), F32),
        compiler_params=_params(("arbitrary",)),
        name="dispatch",
    )(needs_zero, dest_blocks, xn3)


def _expert_kernel(be_ref, na_ref, x_ref, w1g_ref, w1l_ref, b1g_ref, b1l_ref, w2_ref, b2_ref, y_ref):
    i = pl.program_id(0)

    @pl.when(i < na_ref[0])
    def _():
        xb = _load_row_tiles(x_ref, w1g_ref.shape[1] // LANES).astype(BF16)
        glu = jnp.dot(xb, w1g_ref[0], preferred_element_type=F32) + b1g_ref[0]
        lin = jnp.dot(xb, w1l_ref[0], preferred_element_type=F32) + b1l_ref[0]
        glu = jnp.minimum(glu, SWIGLU_LIMIT)
        lin = jnp.clip(lin, -SWIGLU_LIMIT, SWIGLU_LIMIT)
        act = glu * jax.nn.sigmoid(SWIGLU_ALPHA * glu) * (lin + 1.0)
        _store_row_tiles(y_ref, jnp.dot(act.astype(BF16), w2_ref[0], preferred_element_type=F32) + b2_ref[0])

    @pl.when(i >= na_ref[0])
    def _():
        y_ref[...] = jnp.zeros_like(y_ref)


def _experts(blk_expert, n_active, xs, w1g, w1l, b1g, b1l, w2, b2, bm):
    d = w1g.shape[1]
    sub = d // LANES
    n_pad = xs.shape[0] // sub
    n_blocks = n_pad // bm
    de = w1g.shape[-1]
    last = lambda i, na: jnp.minimum(i, na[0] - 1)
    wmap = lambda i, be, na: (be[last(i, na)], 0, 0)
    grid_spec = pltpu.PrefetchScalarGridSpec(
        num_scalar_prefetch=2,
        grid=(n_blocks,),
        in_specs=[pl.BlockSpec((bm * sub, LANES), lambda i, be, na: (last(i, na), 0)),
                  pl.BlockSpec((1, d, de), wmap), pl.BlockSpec((1, d, de), wmap),
                  pl.BlockSpec((1, 1, de), wmap), pl.BlockSpec((1, 1, de), wmap),
                  pl.BlockSpec((1, de, d), wmap), pl.BlockSpec((1, 1, d), wmap)],
        out_specs=pl.BlockSpec((bm * sub, LANES), lambda i, be, na: (i, 0)),
    )
    return pl.pallas_call(
        _expert_kernel,
        grid_spec=grid_spec,
        out_shape=jax.ShapeDtypeStruct((n_pad * sub, LANES), F32),
        compiler_params=_params(("arbitrary",)),
        name="experts",
    )(blk_expert, n_active, xs, w1g, w1l, b1g, b1l, w2, b2)


def _combine_kernel(dest_ref, y_hbm, h2_ref, tg_ref, g_ref, o_ref, buf, sem, *, tb, final_norm):
    sub = h2_ref.shape[1] // LANES
    tile = lambda r: pl.ds(pl.multiple_of(r * sub, sub), sub)

    def issue(t, carry):
        for k in range(TOP_K):
            dst = dest_ref[0, 0, t * TOP_K + k]
            pltpu.make_async_copy(y_hbm.at[tile(dst)], buf.at[k, tile(t)], sem).start(priority=k % 2)
        return carry
    lax.fori_loop(0, tb, issue, 0)
    for k in range(TOP_K):
        pltpu.make_async_copy(y_hbm.at[pl.ds(0, tb * sub)], buf.at[k], sem).wait()

    acc = h2_ref[...]
    for k in range(TOP_K):
        acc = acc + _load_row_tiles(buf.at[k], sub) * tg_ref[:, k:k + 1]
    o_ref[...] = _rms(acc, g_ref[...]) if final_norm else acc


def _combine(dest_blocks, y, h2, tg, g, tb, final_norm):
    t, d = h2.shape
    row = lambda i: (i, 0)
    return pl.pallas_call(
        functools.partial(_combine_kernel, tb=tb, final_norm=final_norm),
        grid=(t // tb,),
        in_specs=[pl.BlockSpec((1, 1, tb * TOP_K), lambda i: (i, 0, 0), memory_space=pltpu.SMEM),
                  pl.BlockSpec(memory_space=pl.ANY),
                  pl.BlockSpec((tb, d), row), pl.BlockSpec((tb, LANES), row),
                  pl.BlockSpec((1, d), lambda i: (0, 0))],
        out_specs=pl.BlockSpec((tb, d), row),
        out_shape=jax.ShapeDtypeStruct((t, d), F32),
        scratch_shapes=[pltpu.VMEM((TOP_K, tb * (d // LANES), LANES), F32), pltpu.SemaphoreType.DMA],
        compiler_params=_params(("arbitrary",)),
        name="combine",
    )(dest_blocks, y, h2, tg, g)


def _route(top_e, rank, counts, n_experts, bm):
    a = top_e.size
    padded = (counts + bm - 1) // bm * bm
    pend = jnp.cumsum(padded)
    pstart = pend - padded
    dest = (jnp.take(pstart, top_e) + rank).astype(I32)
    n_blocks = -(-a // bm) + n_experts
    blk_lo = jnp.arange(n_blocks, dtype=I32) * bm
    blk_expert = jnp.minimum(jnp.sum(blk_lo[:, None] >= pend[None, :], axis=1), n_experts - 1).astype(I32)
    n_active = (pend[-1] // bm).astype(I32)
    has_pad = jnp.any((blk_lo[:, None] + bm) == pend[None, :], axis=1)
    needs_zero = (has_pad | (jnp.arange(n_blocks) >= n_active)).astype(I32)
    return dest, blk_expert, n_active.reshape(1), needs_zero


def _pad_cols(w, n):
    return jnp.pad(w, ((0, 0), (0, n - w.shape[1])))


def _layer(h, mem, l, p):
    b, s, d = h.shape
    t = b * s
    lambda_init = 0.8 - 0.6 * math.exp(-0.3 * l)
    n_main = 6 * GROUP
    w_in = p['w_in']
    w_main = w_in[:, :n_main].astype(BF16)
    w_i = _pad_cols(w_in[:, n_main:n_main + MLSTM_HEADS], LANES).astype(BF16)
    w_f = _pad_cols(w_in[:, n_main + MLSTM_HEADS:], LANES).astype(BF16)
    dq, dk, dv, mqk, mv, mo, gi, gf = _in_proj(h.reshape(t, d), p['norm_mix_g'].reshape(1, d),
                                               w_main, w_i, w_f, tm=min(512, t))
    r3 = lambda a: a.reshape(b, s, a.shape[-1])
    vec = lambda a: a.reshape(1, -1)
    od = _diff_attention(r3(dq), r3(dk), r3(dv), vec(p['lambda_q1']), vec(p['lambda_k1']),
                         vec(p['lambda_q2']), vec(p['lambda_k2']), vec(p['diff_norm_g']),
                         lambda_init, bq=min(512, s))
    hm = _mlstm(r3(mqk), r3(mv), r3(gi), r3(gf), r3(mo), p['conv_w'], vec(p['conv_b']),
                _pad_cols(vec(p['b_igate']), LANES), _pad_cols(vec(p['b_fgate']), LANES),
                vec(p['mlstm_norm_g']), bb=2 if b % 2 == 0 else 1)
    kmem, vmem = _kv_proj(mem, vec(p['norm_mem_g']), p['w_ckv'].astype(BF16))
    w_out = p['w_out'].astype(BF16)
    n_experts = p['w_router'].shape[1]
    wr = _pad_cols(p['w_router'], LANES)
    wr_hi = wr.astype(BF16)
    wr = jnp.concatenate([wr_hi, (wr - wr_hi.astype(F32)).astype(BF16)], axis=1)
    br =jnp.concatenate([vec(p['b_router']), jnp.full((1, LANES - n_experts), NEG_BIG, F32)], axis=1)
    h2, xn3, te, tg, cnt = _post(od, hm, h, kmem, vmem, w_out[:GROUP], w_out[GROUP:],
                                 vec(p['norm_xattn_g']), p['w_cq'].astype(BF16), p['w_co'].astype(BF16),
                                 vec(p['norm_ffn_g']), wr, br, tm=min(512, s))
    te = te.reshape(t, LANES)
    dest, blk_expert, n_active, needs_zero = _route(te[:, :TOP_K], te[:, TOP_K:2 * TOP_K],
                                                    cnt[0, :n_experts].astype(I32), n_experts, MOE_BLOCK)
    tb_d = min(512, t)
    xs = _dispatch(needs_zero, dest.reshape(t // tb_d, 1, tb_d * TOP_K), xn3.reshape(t * (d // LANES), LANES),
                   d // LANES, MOE_BLOCK, tb_d)
    w1g, w1l = _split_w1(p['w1'], rb=256)
    y = _experts(blk_expert, n_active, xs, w1g, w1l,
                 p['b1'][:, None, 0::2], p['b1'][:, None, 1::2],
                 p['w2'].astype(BF16), p['b2'][:, None, :], MOE_BLOCK)
    tb = min(256, t)
    return y, dest.reshape(t // tb, 1, tb * TOP_K), h2.reshape(t, d), tg.reshape(t, LANES), tb


def kernel(x, mem, norm_mix_g, w_in, conv_w, conv_b, b_igate, b_fgate, mlstm_norm_g, lambda_q1, lambda_k1, lambda_q2, lambda_k2, diff_norm_g, w_out, norm_xattn_g, norm_mem_g, w_cq, w_ckv, w_co, norm_ffn_g, w_router, b_router, w1, b1, w2, b2, norm_final_g):
    stacked = dict(norm_mix_g=norm_mix_g, w_in=w_in, conv_w=conv_w, conv_b=conv_b, b_igate=b_igate,
                   b_fgate=b_fgate, mlstm_norm_g=mlstm_norm_g, lambda_q1=lambda_q1, lambda_k1=lambda_k1,
                   lambda_q2=lambda_q2, lambda_k2=lambda_k2, diff_norm_g=diff_norm_g, w_out=w_out,
                   norm_xattn_g=norm_xattn_g, norm_mem_g=norm_mem_g, w_cq=w_cq, w_ckv=w_ckv, w_co=w_co,
                   norm_ffn_g=norm_ffn_g, w_router=w_router, b_router=b_router, w1=w1, b1=b1, w2=w2, b2=b2)
    depth = w_in.shape[0]
    b, s, d = x.shape
    h = x
    for l in range(depth):
        p = {k: v[l] for k, v in stacked.items()}
        y, dest_blocks, h2, tg, tb = _layer(h, mem, l, p)
        h = _combine(dest_blocks, y, h2, tg, norm_final_g.reshape(1, d), tb,
                     final_norm=l == depth - 1).reshape(b, s, d)
    return h
```

```python
import functools
import math

import jax
import jax.numpy as jnp
from jax import lax
from jax.experimental import pallas as pl
from jax.experimental.pallas import tpu as pltpu

F32 = jnp.float32
BF16 = jnp.bfloat16
I32 = jnp.int32

RMS_EPS = 1e-5
LANES = 128
VMEM_LIMIT = 56 * 1024 * 1024

DIFF_HEADS = 4
DIFF_QK_DIM = 64
MLSTM_HEADS = 4
MLSTM_QK_DIM = 64
CONV_K = 4
CHUNK = 128
XATTN_HEADS = 4
TOP_K = 4
SWIGLU_ALPHA = 1.702
SWIGLU_LIMIT = 7.0
MOE_BLOCK = 512
GROUP = 512
NEG_BIG = -1e30


def _rms(x, g):
    return x * lax.rsqrt(jnp.mean(x * x, axis=-1, keepdims=True) + RMS_EPS) * g


def _store_row_tiles(ref, m):
    rows, d = m.shape
    sub = d // LANES
    for c in range(sub):
        ref[pl.ds(c, rows, stride=sub), :] = m[:, c * LANES:(c + 1) * LANES]


def _load_row_tiles(ref, sub):
    rows = ref.shape[0] // sub
    return jnp.concatenate([ref[pl.ds(c, rows, stride=sub), :] for c in range(sub)], axis=1)


def _params(sem, vmem=VMEM_LIMIT, flags=None):
    return pltpu.CompilerParams(dimension_semantics=sem, vmem_limit_bytes=vmem, flags=flags)


def _inproj_kernel(x_ref, g_ref, w_ref, wi_ref, wf_ref,
                   dq_ref, dk_ref, dv_ref, mqk_ref, mv_ref, mo_ref, gi_ref, gf_ref):
    xb = _rms(x_ref[...], g_ref[...]).astype(BF16)
    for n, o_ref in enumerate((dq_ref, dk_ref, dv_ref, mqk_ref, mv_ref, mo_ref)):
        o_ref[...] = jnp.dot(xb, w_ref[:, n * GROUP:(n + 1) * GROUP],
                             preferred_element_type=F32).astype(o_ref.dtype)
    gi_ref[...] = jnp.dot(xb, wi_ref[...], preferred_element_type=F32)
    gf_ref[...] = jnp.dot(xb, wf_ref[...], preferred_element_type=F32)


def _in_proj(x2, g, w_main, w_i, w_f, tm):
    t, d = x2.shape
    row = lambda i: (i, 0)
    const = lambda i: (0, 0)
    out_dtypes = (BF16, BF16, BF16, F32, BF16, F32)
    return pl.pallas_call(
        _inproj_kernel,
        grid=(t // tm,),
        in_specs=[pl.BlockSpec((tm, d), row), pl.BlockSpec((1, d), const),
                  pl.BlockSpec(w_main.shape, const), pl.BlockSpec(w_i.shape, const),
                  pl.BlockSpec(w_f.shape, const)],
        out_specs=[pl.BlockSpec((tm, GROUP), row)] * 6 + [pl.BlockSpec((tm, LANES), row)] * 2,
        out_shape=[jax.ShapeDtypeStruct((t, GROUP), dt) for dt in out_dtypes]
        + [jax.ShapeDtypeStruct((t, LANES), F32)] * 2,
        compiler_params=_params(("parallel",)),
        name="in_proj",
    )(x2, g, w_main, w_i, w_f)


def _diffattn_kernel(q_ref, k_ref, v_ref, lq1_ref, lk1_ref, lq2_ref, lk2_ref, g_ref, o_ref,
                     *, bq, lambda_init):
    s_len = q_ref.shape[1]
    lane = lax.broadcasted_iota(I32, (1, LANES), 1)
    lo = lane < DIFF_QK_DIM
    row = lax.broadcasted_iota(I32, (bq, bq), 0)
    col = lax.broadcasted_iota(I32, (bq, bq), 1)
    causal = col <= row
    lam = (jnp.exp(jnp.sum(lq1_ref[...] * lk1_ref[...], axis=-1, keepdims=True))
           - jnp.exp(jnp.sum(lq2_ref[...] * lk2_ref[...], axis=-1, keepdims=True)) + lambda_init)

    def step(qs, j, carry, masked):
        kj = k_ref[0, j * bq:(j + 1) * bq, :]
        vj = v_ref[0, j * bq:(j + 1) * bq, :]
        new = []
        for qm, (m, l, a) in zip(qs, carry):
            s = lax.dot_general(qm, kj, (((1,), (1,)), ((), ())), preferred_element_type=F32)
            if masked:
                s = jnp.where(causal, s, -jnp.inf)
            mn = jnp.maximum(m, jnp.max(s, axis=-1, keepdims=True))
            p = jnp.exp(s - mn)
            alpha = jnp.exp(m - mn)
            l = alpha * l + jnp.sum(p, axis=-1, keepdims=True)
            a = alpha * a + jnp.dot(p.astype(BF16), vj, preferred_element_type=F32)
            new.append((mn, l, a))
        return tuple(new)

    for i in range(s_len // bq):
        q = q_ref[0, i * bq:(i + 1) * bq, :] * jnp.asarray(DIFF_QK_DIM ** -0.5, BF16)
        zero = jnp.zeros_like(q)
        qs = (jnp.where(lo, q, zero), jnp.where(lo, zero, q))
        init_one = (jnp.full((bq, 1), -jnp.inf, F32), jnp.zeros((bq, 1), F32), jnp.zeros((bq, LANES), F32))
        carry = (init_one, init_one)
        for j in range(i):
            carry = step(qs, j, carry, False)
        (_, l1, a1), (_, l2, a2) = step(qs, i, carry, True)
        od = a1 / l1 - lam * (a2 / l2)
        o_ref[0, i * bq:(i + 1) * bq, :] = (_rms(od, g_ref[...]) * (1.0 - lambda_init)).astype(o_ref.dtype)


def _diff_attention(dq, dk, dv, lq1, lk1, lq2, lk2, g, lambda_init, bq):
    b, s, w = dq.shape
    blk = lambda bi, h: (bi, 0, h)
    vec = lambda bi, h: (0, 0)
    return pl.pallas_call(
        functools.partial(_diffattn_kernel, bq=bq, lambda_init=lambda_init),
        grid=(b, DIFF_HEADS),
        in_specs=[pl.BlockSpec((1, s, LANES), blk)] * 3
        + [pl.BlockSpec((1, DIFF_QK_DIM), vec)] * 4
        + [pl.BlockSpec((1, LANES), lambda bi, h: (0, h))],
        out_specs=pl.BlockSpec((1, s, LANES), blk),
        out_shape=jax.ShapeDtypeStruct((b, s, w), BF16),
        compiler_params=_params(("parallel", "parallel")),
        name="diff_attn",
    )(dq, dk, dv, lq1, lk1, lq2, lk2, g)


def _log_sigmoid(x):
    return -(jnp.maximum(-x, 0.0) + jnp.log1p(jnp.exp(-jnp.abs(x))))


def _mlstm_kernel(mqk_ref, mv_ref, gi_ref, gf_ref, mo_ref, cw_ref, cb_ref, bi_ref, bf_ref, ng_ref,
                  o_ref, conv_scr, c_scr, m_scr, *, bb):
    c = pl.program_id(1)
    L = CHUNK
    dk = MLSTM_QK_DIM
    nq = MLSTM_HEADS * dk
    tail = 8

    @pl.when(c == 0)
    def _():
        conv_scr[:, 0:tail, :] = jnp.zeros((bb, tail, 2 * nq), F32)
        c_scr[...] = jnp.zeros_like(c_scr)
        m_scr[...] = jnp.zeros_like(m_scr)

    row = lax.broadcasted_iota(I32, (L, L), 0)
    col = lax.broadcasted_iota(I32, (L, L), 1)
    causal = col <= row
    tril = causal.astype(F32)
    lane = lax.broadcasted_iota(I32, (1, LANES), 1)
    sub = lax.broadcasted_iota(I32, (LANES, 1), 0)
    ones_col = jnp.where(lane == 0, 1.0, 0.0).astype(BF16) * jnp.ones((L, 1), BF16)

    for b in range(bb):
        conv_scr[b, tail:tail + L, :] = mqk_ref[b]
        y = cb_ref[...]
        for j in range(CONV_K):
            y = y + conv_scr[b, pl.ds(tail - (CONV_K - 1) + j, L), :] * cw_ref[j:j + 1, :]
        conv_scr[b, 0:tail, :] = conv_scr[b, L:L + tail, :]
        qk = y * jax.nn.sigmoid(y)

        ig = gi_ref[b] + bi_ref[...]
        lf = _log_sigmoid(gf_ref[b] + bf_ref[...])
        bcum = jnp.dot(tril, lf, preferred_element_type=F32, precision=lax.Precision.HIGHEST)
        a_t = (ig - bcum).T

        for pair in range(MLSTM_HEADS // 2):
            q_pair = qk[:, pair * LANES:(pair + 1) * LANES] * (dk ** -0.5)
            k_pair = qk[:, nq + pair * LANES:nq + (pair + 1) * LANES]
            kt_pair = k_pair.T
            c_old = c_scr[b, pair]
            c_bf = c_old.astype(BF16)
            upd = jnp.zeros_like(c_old)
            decays = []
            for hh in range(2):
                h = 2 * pair + hh
                sel = (lane >= hh * dk) & (lane < (hh + 1) * dk)
                qm = jnp.where(sel, q_pair, 0.0).astype(BF16)
                km = jnp.where(sel, k_pair, 0.0).astype(BF16)
                selr = (sub >= hh * dk) & (sub < (hh + 1) * dk)
                ktm = jnp.where(selr, kt_pair, 0.0).astype(BF16)
                v_h = mv_ref[b, :, h * LANES:(h + 1) * LANES]
                v_aug = jnp.concatenate([v_h, ones_col], axis=1)

                m_st = m_scr[b, h, 0:1, 0:1]
                bc = bcum[:, h:h + 1]
                ic = ig[:, h:h + 1]
                a_m = jnp.where(causal, a_t[h:h + 1, :], -jnp.inf)
                inter = bc + m_st
                m_t = jnp.maximum(inter, bc + jnp.max(a_m, axis=-1, keepdims=True))
                qkt = lax.dot_general(qm, km, (((1,), (1,)), ((), ())), preferred_element_type=F32)
                w = qkt * jnp.exp(a_m + (bc - m_t))
                g = jnp.exp(inter - m_t)
                num_aug = (g * jnp.dot(qm, c_bf, preferred_element_type=F32)
                           + jnp.dot(w.astype(BF16), v_aug, preferred_element_type=F32))
                num = num_aug[:, :LANES]
                den = num_aug[:, LANES:LANES + 1]
                h_t = num / jnp.maximum(jnp.abs(den), jnp.exp(-m_t))

                b_last = bc[L - 1:L, :]
                gs = b_last - bc + ic
                m_new = jnp.maximum(b_last + m_st, jnp.max(gs, axis=0, keepdims=True))
                decays.append(jnp.exp(b_last + m_st - m_new))
                ws = jnp.exp(gs - m_new)
                wsv = (ws * v_aug.astype(F32)).astype(BF16)
                upd = upd + jnp.dot(ktm, wsv, preferred_element_type=F32)
                m_scr[b, h, 0:1, :] = jnp.broadcast_to(m_new, (1, LANES))

                hn = _rms(h_t, ng_ref[:, h * LANES:(h + 1) * LANES])
                gate = jax.nn.sigmoid(mo_ref[b, :, h * LANES:(h + 1) * LANES])
                o_ref[b, :, h * LANES:(h + 1) * LANES] = (hn * gate).astype(o_ref.dtype)
            d_rows = jnp.where(sub < dk, decays[0], decays[1])
            c_scr[b, pair] = d_rows * c_old + upd


def _mlstm(mqk, mv, gi, gf, mo, conv_w, conv_b, b_i, b_f, norm_g, bb):
    b, s, w = mqk.shape
    blk = lambda bi, c: (bi, c, 0)
    const = lambda bi, c: (0, 0)
    return pl.pallas_call(
        functools.partial(_mlstm_kernel, bb=bb),
        grid=(b // bb, s // CHUNK),
        in_specs=[pl.BlockSpec((bb, CHUNK, w), blk), pl.BlockSpec((bb, CHUNK, w), blk),
                  pl.BlockSpec((bb, CHUNK, LANES), blk), pl.BlockSpec((bb, CHUNK, LANES), blk),
                  pl.BlockSpec((bb, CHUNK, w), blk),
                  pl.BlockSpec(conv_w.shape, const), pl.BlockSpec(conv_b.shape, const),
                  pl.BlockSpec(b_i.shape, const), pl.BlockSpec(b_f.shape, const),
                  pl.BlockSpec(norm_g.shape, const)],
        out_specs=pl.BlockSpec((bb, CHUNK, w), blk),
        out_shape=jax.ShapeDtypeStruct((b, s, w), BF16),
        scratch_shapes=[pltpu.VMEM((bb, CHUNK + 8, w), F32),
                        pltpu.VMEM((bb, MLSTM_HEADS // 2, LANES, 2 * LANES), F32),
                        pltpu.VMEM((bb, MLSTM_HEADS, 8, LANES), F32)],
        compiler_params=_params(("parallel", "arbitrary")),
        name="mlstm",
    )(mqk, mv, gi, gf, mo, conv_w, conv_b, b_i, b_f, norm_g)


def _kvproj_kernel(mem_ref, g_ref, w_ref, k_ref, v_ref):
    d = mem_ref.shape[-1]
    mb = _rms(mem_ref[0], g_ref[...]).astype(BF16)
    k_ref[0] = jnp.dot(mb, w_ref[:, :d], preferred_element_type=F32).astype(k_ref.dtype)
    v_ref[0] = jnp.dot(mb, w_ref[:, d:], preferred_element_type=F32).astype(v_ref.dtype)


def _kv_proj(mem, g, w_ckv):
    b, m, d = mem.shape
    blk = lambda bi: (bi, 0, 0)
    const = lambda bi: (0, 0)
    return pl.pallas_call(
        _kvproj_kernel,
        grid=(b,),
        in_specs=[pl.BlockSpec((1, m, d), blk), pl.BlockSpec((1, d), const),
                  pl.BlockSpec(w_ckv.shape, const)],
        out_specs=[pl.BlockSpec((1, m, d), blk)] * 2,
        out_shape=[jax.ShapeDtypeStruct((b, m, d), BF16)] * 2,
        compiler_params=_params(("parallel",)),
        name="kv_proj",
    )(mem, g, w_ckv)


def _post_kernel(od_ref, hm_ref, x_ref, k_ref, v_ref, wo1_ref, wo2_ref, g2_ref, wcq_ref, wco_ref,
                 g3_ref, wr_ref, br_ref, h2_ref, xn3_ref, te_ref, tg_ref, cnt_ref, base_scr):
    d = x_ref.shape[-1]
    hd = d // XATTN_HEADS
    h1 = (x_ref[0] + jnp.dot(od_ref[0], wo1_ref[...], preferred_element_type=F32)
          + jnp.dot(hm_ref[0], wo2_ref[...], preferred_element_type=F32))
    q = jnp.dot(_rms(h1, g2_ref[...]).astype(BF16), wcq_ref[...], preferred_element_type=F32)
    q = (q * (hd ** -0.5)).astype(BF16)
    heads = []
    for h in range(XATTN_HEADS):
        sl = slice(h * hd, (h + 1) * hd)
        s = lax.dot_general(q[:, sl], k_ref[0, :, sl], (((1,), (1,)), ((), ())),
                            preferred_element_type=F32)
        e = jnp.exp(s - jnp.max(s, axis=-1, keepdims=True))
        p = e / jnp.sum(e, axis=-1, keepdims=True)
        heads.append(jnp.dot(p.astype(BF16), v_ref[0, :, sl], preferred_element_type=F32))
    o = jnp.concatenate(heads, axis=1).astype(BF16)
    h2 = h1 + jnp.dot(o, wco_ref[...], preferred_element_type=F32)
    h2_ref[0] = h2
    xn3 = _rms(h2, g3_ref[...])
    _store_row_tiles(xn3_ref.at[0], xn3)
    x_hi = xn3.astype(BF16)
    x_lo = (xn3 - x_hi.astype(F32)).astype(BF16)
    hh_hl = jnp.dot(x_hi, wr_ref[...], preferred_element_type=F32)
    lh = jnp.dot(x_lo, wr_ref[:, :LANES], preferred_element_type=F32)
    logits = hh_hl[:, :LANES] + (hh_hl[:, LANES:] + lh) + br_ref[...]
    lane = lax.broadcasted_iota(I32, logits.shape, 1)
    lane_f = lane.astype(F32)
    cur = logits
    te = jnp.zeros(logits.shape, F32)
    chosen = jnp.zeros(logits.shape, F32)
    vals, hits = [], []
    for k in range(TOP_K):
        m = jnp.max(cur, axis=-1, keepdims=True)
        idx = jnp.min(jnp.where(cur == m, lane_f, float(LANES)), axis=-1, keepdims=True)
        hit = lane_f == idx
        vals.append(m)
        hits.append(hit)
        te = jnp.where(lane == k, idx, te)
        chosen = jnp.where(hit, 1.0, chosen)
        cur = jnp.where(hit, NEG_BIG * 2.0, cur)
    es = [jnp.exp(v - vals[0]) for v in vals]
    tot = es[0] + es[1] + es[2] + es[3]
    tg = jnp.zeros(logits.shape, F32)
    for k in range(TOP_K):
        tg = jnp.where(lane == k, es[k] / tot, tg)
    tg_ref[0] = tg

    @pl.when((pl.program_id(0) == 0) & (pl.program_id(1) == 0))
    def _():
        base_scr[...] = jnp.zeros_like(base_scr)
    tm = logits.shape[0]
    earlier = (lax.broadcasted_iota(I32, (tm, tm), 1) < lax.broadcasted_iota(I32, (tm, tm), 0)).astype(BF16)
    prior = jnp.dot(earlier, chosen.astype(BF16), preferred_element_type=F32) + base_scr[...]
    for k in range(TOP_K):
        rank = jnp.sum(jnp.where(hits[k], prior, 0.0), axis=-1, keepdims=True)
        te = jnp.where(lane == TOP_K + k, rank, te)
    te_ref[0] = te.astype(I32)
    base_scr[...] = base_scr[...] + jnp.sum(chosen, axis=0, keepdims=True)
    cnt_ref[...] = base_scr[...]


def _post(od, hm, x, kmem, vmem, wo1, wo2, g2, wcq, wco, g3, wr, br, tm):
    b, s, d = x.shape
    w = od.shape[-1]
    m = kmem.shape[1]
    blk = lambda bi, i: (bi, i, 0)
    mem = lambda bi, i: (bi, 0, 0)
    const = lambda bi, i: (0, 0)
    full = lambda a: pl.BlockSpec(a.shape, const)
    return pl.pallas_call(
        _post_kernel,
        grid=(b, s // tm),
        in_specs=[pl.BlockSpec((1, tm, w), blk), pl.BlockSpec((1, tm, w), blk),
                  pl.BlockSpec((1, tm, d), blk), pl.BlockSpec((1, m, d), mem),
                  pl.BlockSpec((1, m, d), mem), full(wo1), full(wo2), full(g2), full(wcq),
                  full(wco), full(g3), full(wr), full(br)],
        out_specs=[pl.BlockSpec((1, tm, d), blk),
                   pl.BlockSpec((1, tm * (d // LANES), LANES), blk),
                   pl.BlockSpec((1, tm, LANES), blk), pl.BlockSpec((1, tm, LANES), blk),
                   pl.BlockSpec((1, LANES), const)],
        out_shape=[jax.ShapeDtypeStruct((b, s, d), F32), jax.ShapeDtypeStruct((b, s * (d // LANES), LANES), F32),
                   jax.ShapeDtypeStruct((b, s, LANES), I32), jax.ShapeDtypeStruct((b, s, LANES), F32),
                   jax.ShapeDtypeStruct((1, LANES), F32)],
        scratch_shapes=[pltpu.VMEM((1, LANES), F32)],
        compiler_params=_params(("arbitrary", "arbitrary")),
        name="post_mixer",
    )(od, hm, x, kmem, vmem, wo1, wo2, g2, wcq, wco, g3, wr, br)


def _split_kernel(w_ref, p_ref, wg_ref, wl_ref):
    wb = w_ref[0].astype(BF16)
    two = 2 * LANES
    for g in range(wb.shape[1] // two):
        r = jnp.dot(wb[:, g * two:(g + 1) * two], p_ref[...], preferred_element_type=F32)
        wg_ref[0, :, g * LANES:(g + 1) * LANES] = r[:, :LANES].astype(BF16)
        wl_ref[0, :, g * LANES:(g + 1) * LANES] = r[:, LANES:].astype(BF16)


def _split_w1(w1, rb):
    e, d, de2 = w1.shape
    two = 2 * LANES
    src = jnp.arange(two)[:, None]
    dst = jnp.arange(two)[None, :]
    perm = jnp.where(dst < LANES, src == 2 * dst, src == 2 * (dst - LANES) + 1).astype(BF16)
    blk = lambda ei, r: (ei, r, 0)
    return pl.pallas_call(
        _split_kernel,
        grid=(e, d // rb),
        in_specs=[pl.BlockSpec((1, rb, de2), blk), pl.BlockSpec((two, two), lambda ei, r: (0, 0))],
        out_specs=[pl.BlockSpec((1, rb, de2 // 2), blk)] * 2,
        out_shape=[jax.ShapeDtypeStruct((e, d, de2 // 2), BF16)] * 2,
        compiler_params=_params(("parallel", "parallel")),
        name="split_w1",
    )(w1, perm)


def _dispatch_kernel(nz_ref, dest_ref, x_ref, xs_hbm, zbuf, sem, zsem, *, tb, bm, n_blocks, sub):
    i = pl.program_id(0)
    tile = lambda r: pl.ds(pl.multiple_of(r * sub, sub), sub)

    @pl.when(i == 0)
    def _():
        zbuf[...] = jnp.zeros_like(zbuf)

        def zstart(blk, carry):
            @pl.when(nz_ref[blk] != 0)
            def _():
                pltpu.make_async_copy(zbuf, xs_hbm.at[pl.ds(pl.multiple_of(blk * bm * sub, bm * sub), bm * sub)],
                                      zsem).start()
            return carry
        lax.fori_loop(0, n_blocks, zstart, 0)

        def zwait(blk, carry):
            @pl.when(nz_ref[blk] != 0)
            def _():
                pltpu.make_async_copy(zbuf, xs_hbm.at[pl.ds(0, bm * sub)], zsem).wait()
            return carry
        lax.fori_loop(0, n_blocks, zwait, 0)

    def issue(t, carry):
        for k in range(TOP_K):
            dst = dest_ref[0, 0, t * TOP_K + k]
            pltpu.make_async_copy(x_ref.at[tile(t)], xs_hbm.at[tile(dst)], sem).start(priority=k % 2)
        return carry
    lax.fori_loop(0, tb, issue, 0)
    for k in range(TOP_K):
        pltpu.make_async_copy(x_ref, xs_hbm.at[pl.ds(0, tb * sub)], sem).wait()


def _dispatch(needs_zero, dest_blocks, xn3, sub, bm, tb):
    t = xn3.shape[0] // sub
    n_blocks = needs_zero.shape[0]
    grid_spec = pltpu.PrefetchScalarGridSpec(
        num_scalar_prefetch=1,
        grid=(t // tb,),
        in_specs=[pl.BlockSpec((1, 1, tb * TOP_K), lambda i, nz: (i, 0, 0), memory_space=pltpu.SMEM),
                  pl.BlockSpec((tb * sub, LANES), lambda i, nz: (i, 0))],
        out_specs=pl.BlockSpec(memory_space=pl.ANY),
        scratch_shapes=[pltpu.VMEM((bm * sub, LANES), F32), pltpu.SemaphoreType.DMA, pltpu.SemaphoreType.DMA],
    )
    return pl.pallas_call(
        functools.partial(_dispatch_kernel, tb=tb, bm=bm, n_blocks=n_blocks, sub=sub),
        grid_spec=grid_spec,
        out_shape=jax.ShapeDtypeStruct((n_blocks * bm * sub, LANES), F32),
        compiler_params=_params(("arbitrary",)),
        name="dispatch",
    )(needs_zero, dest_blocks, xn3)


def _expert_kernel(be_ref, na_ref, x_ref, w1g_ref, w1l_ref, b1g_ref, b1l_ref, w2_ref, b2_ref, y_ref):
    i = pl.program_id(0)

    @pl.when(i < na_ref[0])
    def _():
        xb = _load_row_tiles(x_ref, w1g_ref.shape[1] // LANES).astype(BF16)
        glu = jnp.dot(xb, w1g_ref[0], preferred_element_type=F32) + b1g_ref[0]
        lin = jnp.dot(xb, w1l_ref[0], preferred_element_type=F32) + b1l_ref[0]
        glu = jnp.minimum(glu, SWIGLU_LIMIT)
        lin = jnp.clip(lin, -SWIGLU_LIMIT, SWIGLU_LIMIT)
        act = glu * jax.nn.sigmoid(SWIGLU_ALPHA * glu) * (lin + 1.0)
        _store_row_tiles(y_ref, jnp.dot(act.astype(BF16), w2_ref[0], preferred_element_type=F32) + b2_ref[0])

    @pl.when(i >= na_ref[0])
    def _():
        y_ref[...] = jnp.zeros_like(y_ref)


def _experts(blk_expert, n_active, xs, w1g, w1l, b1g, b1l, w2, b2, bm):
    d = w1g.shape[1]
    sub = d // LANES
    n_pad = xs.shape[0] // sub
    n_blocks = n_pad // bm
    de = w1g.shape[-1]
    last = lambda i, na: jnp.minimum(i, na[0] - 1)
    wmap = lambda i, be, na: (be[last(i, na)], 0, 0)
    grid_spec = pltpu.PrefetchScalarGridSpec(
        num_scalar_prefetch=2,
        grid=(n_blocks,),
        in_specs=[pl.BlockSpec((bm * sub, LANES), lambda i, be, na: (last(i, na), 0)),
                  pl.BlockSpec((1, d, de), wmap), pl.BlockSpec((1, d, de), wmap),
                  pl.BlockSpec((1, 1, de), wmap), pl.BlockSpec((1, 1, de), wmap),
                  pl.BlockSpec((1, de, d), wmap), pl.BlockSpec((1, 1, d), wmap)],
        out_specs=pl.BlockSpec((bm * sub, LANES), lambda i, be, na: (i, 0)),
    )
    return pl.pallas_call(
        _expert_kernel,
        grid_spec=grid_spec,
        out_shape=jax.ShapeDtypeStruct((n_pad * sub, LANES), F32),
        compiler_params=_params(("arbitrary",)),
        name="experts",
    )(blk_expert, n_active, xs, w1g, w1l, b1g, b1l, w2, b2)


def _combine_kernel(dest_ref, y_hbm, h2_ref, tg_ref, g_ref, o_ref, buf, sem, *, tb, final_norm):
    sub = h2_ref.shape[1] // LANES
    tile = lambda r: pl.ds(pl.multiple_of(r * sub, sub), sub)

    def issue(t, carry):
        for k in range(TOP_K):
            dst = dest_ref[0, 0, t * TOP_K + k]
            pltpu.make_async_copy(y_hbm.at[tile(dst)], buf.at[k, tile(t)], sem).start(priority=k % 2)
        return carry
    lax.fori_loop(0, tb, issue, 0)
    for k in range(TOP_K):
        pltpu.make_async_copy(y_hbm.at[pl.ds(0, tb * sub)], buf.at[k], sem).wait()

    acc = h2_ref[...]
    for k in range(TOP_K):
        acc = acc + _load_row_tiles(buf.at[k], sub) * tg_ref[:, k:k + 1]
    o_ref[...] = _rms(acc, g_ref[...]) if final_norm else acc


def _combine(dest_blocks, y, h2, tg, g, tb, final_norm):
    t, d = h2.shape
    row = lambda i: (i, 0)
    return pl.pallas_call(
        functools.partial(_combine_kernel, tb=tb, final_norm=final_norm),
        grid=(t // tb,),
        in_specs=[pl.BlockSpec((1, 1, tb * TOP_K), lambda i: (i, 0, 0), memory_space=pltpu.SMEM),
                  pl.BlockSpec(memory_space=pl.ANY),
                  pl.BlockSpec((tb, d), row), pl.BlockSpec((tb, LANES), row),
                  pl.BlockSpec((1, d), lambda i: (0, 0))],
        out_specs=pl.BlockSpec((tb, d), row),
        out_shape=jax.ShapeDtypeStruct((t, d), F32),
        scratch_shapes=[pltpu.VMEM((TOP_K, tb * (d // LANES), LANES), F32), pltpu.SemaphoreType.DMA],
        compiler_params=_params(("arbitrary",)),
        name="combine",
    )(dest_blocks, y, h2, tg, g)


def _route(top_e, rank, counts, n_experts, bm):
    a = top_e.size
    padded = (counts + bm - 1) // bm * bm
    pend = jnp.cumsum(padded)
    pstart = pend - padded
    dest = (jnp.take(pstart, top_e) + rank).astype(I32)
    n_blocks = -(-a // bm) + n_experts
    blk_lo = jnp.arange(n_blocks, dtype=I32) * bm
    blk_expert = jnp.minimum(jnp.sum(blk_lo[:, None] >= pend[None, :], axis=1), n_experts - 1).astype(I32)
    n_active = (pend[-1] // bm).astype(I32)
    has_pad = jnp.any((blk_lo[:, None] + bm) == pend[None, :], axis=1)
    needs_zero = (has_pad | (jnp.arange(n_blocks) >= n_active)).astype(I32)
    return dest, blk_expert, n_active.reshape(1), needs_zero


def _pad_cols(w, n):
    return jnp.pad(w, ((0, 0), (0, n - w.shape[1])))


def _layer(h, mem, l, p):
    b, s, d = h.shape
    t = b * s
    lambda_init = 0.8 - 0.6 * math.exp(-0.3 * l)
    n_main = 6 * GROUP
    w_in = p['w_in']
    w_main = w_in[:, :n_main].astype(BF16)
    w_i = _pad_cols(w_in[:, n_main:n_main + MLSTM_HEADS], LANES).astype(BF16)
    w_f = _pad_cols(w_in[:, n_main + MLSTM_HEADS:], LANES).astype(BF16)
    dq, dk, dv, mqk, mv, mo, gi, gf = _in_proj(h.reshape(t, d), p['norm_mix_g'].reshape(1, d),
                                               w_main, w_i, w_f, tm=min(512, t))
    r3 = lambda a: a.reshape(b, s, a.shape[-1])
    vec = lambda a: a.reshape(1, -1)
    od = _diff_attention(r3(dq), r3(dk), r3(dv), vec(p['lambda_q1']), vec(p['lambda_k1']),
                         vec(p['lambda_q2']), vec(p['lambda_k2']), vec(p['diff_norm_g']),
                         lambda_init, bq=min(512, s))
    hm = _mlstm(r3(mqk), r3(mv), r3(gi), r3(gf), r3(mo), p['conv_w'], vec(p['conv_b']),
                _pad_cols(vec(p['b_igate']), LANES), _pad_cols(vec(p['b_fgate']), LANES),
                vec(p['mlstm_norm_g']), bb=2 if b % 2 == 0 else 1)
    kmem, vmem = _kv_proj(mem, vec(p['norm_mem_g']), p['w_ckv'].astype(BF16))
    w_out = p['w_out'].astype(BF16)
    n_experts = p['w_router'].shape[1]
    wr = _pad_cols(p['w_router'], LANES)
    wr_hi = wr.astype(BF16)
    wr = jnp.concatenate([wr_hi, (wr - wr_hi.astype(F32)).astype(BF16)], axis=1)
    br =jnp.concatenate([vec(p['b_router']), jnp.full((1, LANES - n_experts), NEG_BIG, F32)], axis=1)
    h2, xn3, te, tg, cnt = _post(od, hm, h, kmem, vmem, w_out[:GROUP], w_out[GROUP:],
                                 vec(p['norm_xattn_g']), p['w_cq'].astype(BF16), p['w_co'].astype(BF16),
                                 vec(p['norm_ffn_g']), wr, br, tm=min(512, s))
    te = te.reshape(t, LANES)
    dest, blk_expert, n_active, needs_zero = _route(te[:, :TOP_K], te[:, TOP_K:2 * TOP_K],
                                                    cnt[0, :n_experts].astype(I32), n_experts, MOE_BLOCK)
    tb_d = min(512, t)
    xs = _dispatch(needs_zero, dest.reshape(t // tb_d, 1, tb_d * TOP_K), xn3.reshape(t * (d // LANES), LANES),
                   d // LANES, MOE_BLOCK, tb_d)
    w1g, w1l = _split_w1(p['w1'], rb=256)
    y = _experts(blk_expert, n_active, xs, w1g, w1l,
                 p['b1'][:, None, 0::2], p['b1'][:, None, 1::2],
                 p['w2'].astype(BF16), p['b2'][:, None, :], MOE_BLOCK)
    tb = min(256, t)
    return y, dest.reshape(t // tb, 1, tb * TOP_K), h2.reshape(t, d), tg.reshape(t, LANES), tb


def kernel(x, mem, norm_mix_g, w_in, conv_w, conv_b, b_igate, b_fgate, mlstm_norm_g, lambda_q1, lambda_k1, lambda_q2, lambda_k2, diff_norm_g, w_out, norm_xattn_g, norm_mem_g, w_cq, w_ckv, w_co, norm_ffn_g, w_router, b_router, w1, b1, w2, b2, norm_final_g):
    stacked = dict(norm_mix_g=norm_mix_g, w_in=w_in, conv_w=conv_w, conv_b=conv_b, b_igate=b_igate,
                   b_fgate=b_fgate, mlstm_norm_g=mlstm_norm_g, lambda_q1=lambda_q1, lambda_k1=lambda_k1,
                   lambda_q2=lambda_q2, lambda_k2=lambda_k2, diff_norm_g=diff_norm_g, w_out=w_out,
                   norm_xattn_g=norm_xattn_g, norm_mem_g=norm_mem_g, w_cq=w_cq, w_ckv=w_ckv, w_co=w_co,
                   norm_ffn_g=norm_ffn_g, w_router=w_router, b_router=b_router, w1=w1, b1=b1, w2=w2, b2=b2)
    depth = w_in.shape[0]
    b, s, d = x.shape
    h = x
    for l in range(depth):
        p = {k: v[l] for k, v in stacked.items()}
        y, dest_blocks, h2, tg, tb = _layer(h, mem, l, p)
        h = _combine(dest_blocks, y, h2, tg, norm_final_g.reshape(1, d), tb,
                     final_norm=l == depth - 1).reshape(b, s, d)
    return h
```

```python
import functools
import math

import jax
import jax.numpy as jnp
from jax import lax
from jax.experimental import pallas as pl
from jax.experimental.pallas import tpu as pltpu

F32 = jnp.float32
BF16 = jnp.bfloat16
I32 = jnp.int32

RMS_EPS = 1e-5
LANES = 128
VMEM_LIMIT = 56 * 1024 * 1024

DIFF_HEADS = 4
DIFF_QK_DIM = 64
MLSTM_HEADS = 4
MLSTM_QK_DIM = 64
CONV_K = 4
CHUNK = 128
XATTN_HEADS = 4
TOP_K = 4
SWIGLU_ALPHA = 1.702
SWIGLU_LIMIT = 7.0
MOE_BLOCK = 512
GROUP = 512
NEG_BIG = -1e30


def _rms(x, g):
    return x * lax.rsqrt(jnp.mean(x * x, axis=-1, keepdims=True) + RMS_EPS) * g


def _store_row_tiles(ref, m, zero=0):
    rows, d = m.shape
    sub = d // LANES
    for c in range(sub):
        ref[pl.ds(c + zero, rows, stride=sub), :] = m[:, c * LANES:(c + 1) * LANES]


def _load_row_tiles(ref, sub):
    rows = ref.shape[0] // sub
    return jnp.concatenate([ref[pl.ds(c, rows, stride=sub), :] for c in range(sub)], axis=1)


def _params(sem, vmem=VMEM_LIMIT, flags=None):
    return pltpu.CompilerParams(dimension_semantics=sem, vmem_limit_bytes=vmem, flags=flags)


def _inproj_kernel(x_ref, g_ref, w_ref, wi_ref, wf_ref,
                   dq_ref, dk_ref, dv_ref, mqk_ref, mv_ref, mo_ref, gi_ref, gf_ref):
    xb = _rms(x_ref[...], g_ref[...]).astype(BF16)
    for n, o_ref in enumerate((dq_ref, dk_ref, dv_ref, mqk_ref, mv_ref, mo_ref)):
        o_ref[...] = jnp.dot(xb, w_ref[:, n * GROUP:(n + 1) * GROUP],
                             preferred_element_type=F32).astype(o_ref.dtype)
    gi_ref[...] = jnp.dot(xb, wi_ref[...], preferred_element_type=F32)
    gf_ref[...] = jnp.dot(xb, wf_ref[...], preferred_element_type=F32)


def _in_proj(x2, g, w_main, w_i, w_f, tm):
    t, d = x2.shape
    row = lambda i: (i, 0)
    const = lambda i: (0, 0)
    out_dtypes = (BF16, BF16, BF16, F32, BF16, F32)
    return pl.pallas_call(
        _inproj_kernel,
        grid=(t // tm,),
        in_specs=[pl.BlockSpec((tm, d), row), pl.BlockSpec((1, d), const),
                  pl.BlockSpec(w_main.shape, const), pl.BlockSpec(w_i.shape, const),
                  pl.BlockSpec(w_f.shape, const)],
        out_specs=[pl.BlockSpec((tm, GROUP), row)] * 6 + [pl.BlockSpec((tm, LANES), row)] * 2,
        out_shape=[jax.ShapeDtypeStruct((t, GROUP), dt) for dt in out_dtypes]
        + [jax.ShapeDtypeStruct((t, LANES), F32)] * 2,
        compiler_params=_params(("parallel",)),
        name="in_proj",
    )(x2, g, w_main, w_i, w_f)


def _diffattn_kernel(q_ref, k_ref, v_ref, lq1_ref, lk1_ref, lq2_ref, lk2_ref, g_ref, o_ref,
                     *, bq, lambda_init):
    s_len = q_ref.shape[1]
    lane = lax.broadcasted_iota(I32, (1, LANES), 1)
    lo = lane < DIFF_QK_DIM
    row = lax.broadcasted_iota(I32, (bq, bq), 0)
    col = lax.broadcasted_iota(I32, (bq, bq), 1)
    causal = col <= row
    lam = (jnp.exp(jnp.sum(lq1_ref[...] * lk1_ref[...], axis=-1, keepdims=True))
           - jnp.exp(jnp.sum(lq2_ref[...] * lk2_ref[...], axis=-1, keepdims=True)) + lambda_init)

    def step(qs, j, carry, masked):
        kj = k_ref[0, j * bq:(j + 1) * bq, :]
        vj = v_ref[0, j * bq:(j + 1) * bq, :]
        new = []
        for qm, (m, l, a) in zip(qs, carry):
            s = lax.dot_general(qm, kj, (((1,), (1,)), ((), ())), preferred_element_type=F32)
            if masked:
                s = jnp.where(causal, s, -jnp.inf)
            mn = jnp.maximum(m, jnp.max(s, axis=-1, keepdims=True))
            p = jnp.exp(s - mn)
            alpha = jnp.exp(m - mn)
            l = alpha * l + jnp.sum(p, axis=-1, keepdims=True)
            a = alpha * a + jnp.dot(p.astype(BF16), vj, preferred_element_type=F32)
            new.append((mn, l, a))
        return tuple(new)

    for i in range(s_len // bq):
        q = q_ref[0, i * bq:(i + 1) * bq, :] * jnp.asarray(DIFF_QK_DIM ** -0.5, BF16)
        zero = jnp.zeros_like(q)
        qs = (jnp.where(lo, q, zero), jnp.where(lo, zero, q))
        init_one = (jnp.full((bq, 1), -jnp.inf, F32), jnp.zeros((bq, 1), F32), jnp.zeros((bq, LANES), F32))
        carry = (init_one, init_one)
        for j in range(i):
            carry = step(qs, j, carry, False)
        (_, l1, a1), (_, l2, a2) = step(qs, i, carry, True)
        od = a1 / l1 - lam * (a2 / l2)
        o_ref[0, i * bq:(i + 1) * bq, :] = (_rms(od, g_ref[...]) * (1.0 - lambda_init)).astype(o_ref.dtype)


def _diff_attention(dq, dk, dv, lq1, lk1, lq2, lk2, g, lambda_init, bq):
    b, s, w = dq.shape
    blk = lambda bi, h: (bi, 0, h)
    vec = lambda bi, h: (0, 0)
    return pl.pallas_call(
        functools.partial(_diffattn_kernel, bq=bq, lambda_init=lambda_init),
        grid=(b, DIFF_HEADS),
        in_specs=[pl.BlockSpec((1, s, LANES), blk)] * 3
        + [pl.BlockSpec((1, DIFF_QK_DIM), vec)] * 4
        + [pl.BlockSpec((1, LANES), lambda bi, h: (0, h))],
        out_specs=pl.BlockSpec((1, s, LANES), blk),
        out_shape=jax.ShapeDtypeStruct((b, s, w), BF16),
        compiler_params=_params(("parallel", "parallel")),
        name="diff_attn",
    )(dq, dk, dv, lq1, lk1, lq2, lk2, g)


def _log_sigmoid(x):
    return -(jnp.maximum(-x, 0.0) + jnp.log1p(jnp.exp(-jnp.abs(x))))


def _mlstm_kernel(mqk_ref, mv_ref, gi_ref, gf_ref, mo_ref, cw_ref, cb_ref, bi_ref, bf_ref, ng_ref,
                  o_ref, conv_scr, c_scr, m_scr, *, bb):
    c = pl.program_id(1)
    L = CHUNK
    dk = MLSTM_QK_DIM
    nq = MLSTM_HEADS * dk
    tail = 8

    @pl.when(c == 0)
    def _():
        conv_scr[:, 0:tail, :] = jnp.zeros((bb, tail, 2 * nq), F32)
        c_scr[...] = jnp.zeros_like(c_scr)
        m_scr[...] = jnp.zeros_like(m_scr)

    row = lax.broadcasted_iota(I32, (L, L), 0)
    col = lax.broadcasted_iota(I32, (L, L), 1)
    causal = col <= row
    tril = causal.astype(F32)
    lane = lax.broadcasted_iota(I32, (1, LANES), 1)
    sub = lax.broadcasted_iota(I32, (LANES, 1), 0)
    ones_col = jnp.where(lane == 0, 1.0, 0.0).astype(BF16) * jnp.ones((L, 1), BF16)

    for b in range(bb):
        conv_scr[b, tail:tail + L, :] = mqk_ref[b]
        y = cb_ref[...]
        for j in range(CONV_K):
            y = y + conv_scr[b, pl.ds(tail - (CONV_K - 1) + j, L), :] * cw_ref[j:j + 1, :]
        conv_scr[b, 0:tail, :] = conv_scr[b, L:L + tail, :]
        qk = y * jax.nn.sigmoid(y)

        ig = gi_ref[b] + bi_ref[...]
        lf = _log_sigmoid(gf_ref[b] + bf_ref[...])
        bcum = jnp.dot(tril, lf, preferred_element_type=F32, precision=lax.Precision.HIGHEST)
        a_t = (ig - bcum).T

        for pair in range(MLSTM_HEADS // 2):
            q_pair = qk[:, pair * LANES:(pair + 1) * LANES] * (dk ** -0.5)
            k_pair = qk[:, nq + pair * LANES:nq + (pair + 1) * LANES]
            kt_pair = k_pair.T
            c_old = c_scr[b, pair]
            c_bf = c_old.astype(BF16)
            upd = jnp.zeros_like(c_old)
            decays = []
            for hh in range(2):
                h = 2 * pair + hh
                sel = (lane >= hh * dk) & (lane < (hh + 1) * dk)
                qm = jnp.where(sel, q_pair, 0.0).astype(BF16)
                km = jnp.where(sel, k_pair, 0.0).astype(BF16)
                selr = (sub >= hh * dk) & (sub < (hh + 1) * dk)
                ktm = jnp.where(selr, kt_pair, 0.0).astype(BF16)
                v_h = mv_ref[b, :, h * LANES:(h + 1) * LANES]
                v_aug = jnp.concatenate([v_h, ones_col], axis=1)

                m_st = m_scr[b, h, 0:1, 0:1]
                bc = bcum[:, h:h + 1]
                ic = ig[:, h:h + 1]
                a_m = jnp.where(causal, a_t[h:h + 1, :], -jnp.inf)
                inter = bc + m_st
                m_t = jnp.maximum(inter, bc + jnp.max(a_m, axis=-1, keepdims=True))
                qkt = lax.dot_general(qm, km, (((1,), (1,)), ((), ())), preferred_element_type=F32)
                w = qkt * jnp.exp(a_m + (bc - m_t))
                g = jnp.exp(inter - m_t)
                num_aug = (g * jnp.dot(qm, c_bf, preferred_element_type=F32)
                           + jnp.dot(w.astype(BF16), v_aug, preferred_element_type=F32))
                num = num_aug[:, :LANES]
                den = num_aug[:, LANES:LANES + 1]
                h_t = num / jnp.maximum(jnp.abs(den), jnp.exp(-m_t))

                b_last = bc[L - 1:L, :]
                gs = b_last - bc + ic
                m_new = jnp.maximum(b_last + m_st, jnp.max(gs, axis=0, keepdims=True))
                decays.append(jnp.exp(b_last + m_st - m_new))
                ws = jnp.exp(gs - m_new)
                wsv = (ws * v_aug.astype(F32)).astype(BF16)
                upd = upd + jnp.dot(ktm, wsv, preferred_element_type=F32)
                m_scr[b, h, 0:1, :] = jnp.broadcast_to(m_new, (1, LANES))

                hn = _rms(h_t, ng_ref[:, h * LANES:(h + 1) * LANES])
                gate = jax.nn.sigmoid(mo_ref[b, :, h * LANES:(h + 1) * LANES])
                o_ref[b, :, h * LANES:(h + 1) * LANES] = (hn * gate).astype(o_ref.dtype)
            d_rows = jnp.where(sub < dk, decays[0], decays[1])
            c_scr[b, pair] = d_rows * c_old + upd


def _mlstm(mqk, mv, gi, gf, mo, conv_w, conv_b, b_i, b_f, norm_g, bb):
    b, s, w = mqk.shape
    blk = lambda bi, c: (bi, c, 0)
    const = lambda bi, c: (0, 0)
    return pl.pallas_call(
        functools.partial(_mlstm_kernel, bb=bb),
        grid=(b // bb, s // CHUNK),
        in_specs=[pl.BlockSpec((bb, CHUNK, w), blk), pl.BlockSpec((bb, CHUNK, w), blk),
                  pl.BlockSpec((bb, CHUNK, LANES), blk), pl.BlockSpec((bb, CHUNK, LANES), blk),
                  pl.BlockSpec((bb, CHUNK, w), blk),
                  pl.BlockSpec(conv_w.shape, const), pl.BlockSpec(conv_b.shape, const),
                  pl.BlockSpec(b_i.shape, const), pl.BlockSpec(b_f.shape, const),
                  pl.BlockSpec(norm_g.shape, const)],
        out_specs=pl.BlockSpec((bb, CHUNK, w), blk),
        out_shape=jax.ShapeDtypeStruct((b, s, w), BF16),
        scratch_shapes=[pltpu.VMEM((bb, CHUNK + 8, w), F32),
                        pltpu.VMEM((bb, MLSTM_HEADS // 2, LANES, 2 * LANES), F32),
                        pltpu.VMEM((bb, MLSTM_HEADS, 8, LANES), F32)],
        compiler_params=_params(("parallel", "arbitrary")),
        name="mlstm",
    )(mqk, mv, gi, gf, mo, conv_w, conv_b, b_i, b_f, norm_g)


def _kvproj_kernel(mem_ref, g_ref, w_ref, k_ref, v_ref):
    d = mem_ref.shape[-1]
    mb = _rms(mem_ref[0], g_ref[...]).astype(BF16)
    k_ref[0] = jnp.dot(mb, w_ref[:, :d], preferred_element_type=F32).astype(k_ref.dtype)
    v_ref[0] = jnp.dot(mb, w_ref[:, d:], preferred_element_type=F32).astype(v_ref.dtype)


def _kv_proj(mem, g, w_ckv):
    b, m, d = mem.shape
    blk = lambda bi: (bi, 0, 0)
    const = lambda bi: (0, 0)
    return pl.pallas_call(
        _kvproj_kernel,
        grid=(b,),
        in_specs=[pl.BlockSpec((1, m, d), blk), pl.BlockSpec((1, d), const),
                  pl.BlockSpec(w_ckv.shape, const)],
        out_specs=[pl.BlockSpec((1, m, d), blk)] * 2,
        out_shape=[jax.ShapeDtypeStruct((b, m, d), BF16)] * 2,
        compiler_params=_params(("parallel",)),
        name="kv_proj",
    )(mem, g, w_ckv)


def _post_kernel(od_ref, hm_ref, x_ref, k_ref, v_ref, wo1_ref, wo2_ref, g2_ref, wcq_ref, wco_ref,
                 g3_ref, wr_ref, br_ref, h2_ref, xn3_ref, te_ref, tg_ref, cnt_ref, base_scr):
    d = x_ref.shape[-1]
    hd = d // XATTN_HEADS
    h1 = (x_ref[0] + jnp.dot(od_ref[0], wo1_ref[...], preferred_element_type=F32)
          + jnp.dot(hm_ref[0], wo2_ref[...], preferred_element_type=F32))
    q = jnp.dot(_rms(h1, g2_ref[...]).astype(BF16), wcq_ref[...], preferred_element_type=F32)
    q = (q * (hd ** -0.5)).astype(BF16)
    heads = []
    for h in range(XATTN_HEADS):
        sl = slice(h * hd, (h + 1) * hd)
        s = lax.dot_general(q[:, sl], k_ref[0, :, sl], (((1,), (1,)), ((), ())),
                            preferred_element_type=F32)
        e = jnp.exp(s - jnp.max(s, axis=-1, keepdims=True))
        p = e / jnp.sum(e, axis=-1, keepdims=True)
        heads.append(jnp.dot(p.astype(BF16), v_ref[0, :, sl], preferred_element_type=F32))
    o = jnp.concatenate(heads, axis=1).astype(BF16)
    h2 = h1 + jnp.dot(o, wco_ref[...], preferred_element_type=F32)
    h2_ref[0] = h2
    xn3 = _rms(h2, g3_ref[...])
    _store_row_tiles(xn3_ref.at[0], xn3)
    x_hi = xn3.astype(BF16)
    x_lo = (xn3 - x_hi.astype(F32)).astype(BF16)
    hh_hl = jnp.dot(x_hi, wr_ref[...], preferred_element_type=F32)
    lh = jnp.dot(x_lo, wr_ref[:, :LANES], preferred_element_type=F32)
    logits = hh_hl[:, :LANES] + (hh_hl[:, LANES:] + lh) + br_ref[...]
    lane = lax.broadcasted_iota(I32, logits.shape, 1)
    lane_f = lane.astype(F32)
    cur = logits
    te = jnp.zeros(logits.shape, F32)
    chosen = jnp.zeros(logits.shape, F32)
    vals, hits = [], []
    for k in range(TOP_K):
        m = jnp.max(cur, axis=-1, keepdims=True)
        idx = jnp.min(jnp.where(cur == m, lane_f, float(LANES)), axis=-1, keepdims=True)
        hit = lane_f == idx
        vals.append(m)
        hits.append(hit)
        te = jnp.where(lane == k, idx, te)
        chosen = jnp.where(hit, 1.0, chosen)
        cur = jnp.where(hit, NEG_BIG * 2.0, cur)
    es = [jnp.exp(v - vals[0]) for v in vals]
    tot = es[0] + es[1] + es[2] + es[3]
    tg = jnp.zeros(logits.shape, F32)
    for k in range(TOP_K):
        tg = jnp.where(lane == k, es[k] / tot, tg)
    tg_ref[0] = tg

    @pl.when((pl.program_id(0) == 0) & (pl.program_id(1) == 0))
    def _():
        base_scr[...] = jnp.zeros_like(base_scr)
    tm = logits.shape[0]
    earlier = (lax.broadcasted_iota(I32, (tm, tm), 1) < lax.broadcasted_iota(I32, (tm, tm), 0)).astype(BF16)
    prior = jnp.dot(earlier, chosen.astype(BF16), preferred_element_type=F32) + base_scr[...]
    for k in range(TOP_K):
        rank = jnp.sum(jnp.where(hits[k], prior, 0.0), axis=-1, keepdims=True)
        te = jnp.where(lane == TOP_K + k, rank, te)
    te_ref[0] = te.astype(I32)
    base_scr[...] = base_scr[...] + jnp.sum(chosen, axis=0, keepdims=True)
    cnt_ref[...] = base_scr[...]


def _post(od, hm, x, kmem, vmem, wo1, wo2, g2, wcq, wco, g3, wr, br, tm):
    b, s, d = x.shape
    w = od.shape[-1]
    m = kmem.shape[1]
    blk = lambda bi, i: (bi, i, 0)
    mem = lambda bi, i: (bi, 0, 0)
    const = lambda bi, i: (0, 0)
    full = lambda a: pl.BlockSpec(a.shape, const)
    return pl.pallas_call(
        _post_kernel,
        grid=(b, s // tm),
        in_specs=[pl.BlockSpec((1, tm, w), blk), pl.BlockSpec((1, tm, w), blk),
                  pl.BlockSpec((1, tm, d), blk), pl.BlockSpec((1, m, d), mem),
                  pl.BlockSpec((1, m, d), mem), full(wo1), full(wo2), full(g2), full(wcq),
                  full(wco), full(g3), full(wr), full(br)],
        out_specs=[pl.BlockSpec((1, tm, d), blk),
                   pl.BlockSpec((1, tm * (d // LANES), LANES), blk),
                   pl.BlockSpec((1, tm, LANES), blk), pl.BlockSpec((1, tm, LANES), blk),
                   pl.BlockSpec((1, LANES), const)],
        out_shape=[jax.ShapeDtypeStruct((b, s, d), F32), jax.ShapeDtypeStruct((b, s * (d // LANES), LANES), F32),
                   jax.ShapeDtypeStruct((b, s, LANES), I32), jax.ShapeDtypeStruct((b, s, LANES), F32),
                   jax.ShapeDtypeStruct((1, LANES), F32)],
        scratch_shapes=[pltpu.VMEM((1, LANES), F32)],
        compiler_params=_params(("arbitrary", "arbitrary")),
        name="post_mixer",
    )(od, hm, x, kmem, vmem, wo1, wo2, g2, wcq, wco, g3, wr, br)


def _split_kernel(w_ref, p_ref, wg_ref, wl_ref):
    wb = w_ref[0].astype(BF16)
    two = 2 * LANES
    for g in range(wb.shape[1] // two):
        r = jnp.dot(wb[:, g * two:(g + 1) * two], p_ref[...], preferred_element_type=F32)
        wg_ref[0, :, g * LANES:(g + 1) * LANES] = r[:, :LANES].astype(BF16)
        wl_ref[0, :, g * LANES:(g + 1) * LANES] = r[:, LANES:].astype(BF16)


def _split_w1(w1, rb):
    e, d, de2 = w1.shape
    two = 2 * LANES
    src = jnp.arange(two)[:, None]
    dst = jnp.arange(two)[None, :]
    perm = jnp.where(dst < LANES, src == 2 * dst, src == 2 * (dst - LANES) + 1).astype(BF16)
    blk = lambda ei, r: (ei, r, 0)
    return pl.pallas_call(
        _split_kernel,
        grid=(e, d // rb),
        in_specs=[pl.BlockSpec((1, rb, de2), blk), pl.BlockSpec((two, two), lambda ei, r: (0, 0))],
        out_specs=[pl.BlockSpec((1, rb, de2 // 2), blk)] * 2,
        out_shape=[jax.ShapeDtypeStruct((e, d, de2 // 2), BF16)] * 2,
        compiler_params=_params(("parallel", "parallel")),
        name="split_w1",
    )(w1, perm)


def _dispatch_kernel(nz_ref, dest_ref, x_ref, xs_hbm, zbuf, sem, zsem, *, tb, bm, n_blocks, sub):
    i = pl.program_id(0)
    tile = lambda r: pl.ds(pl.multiple_of(r * sub, sub), sub)

    @pl.when(i == 0)
    def _():
        zbuf[...] = jnp.zeros_like(zbuf)

        def zstart(blk, carry):
            @pl.when(nz_ref[blk] != 0)
            def _():
                pltpu.make_async_copy(zbuf, xs_hbm.at[pl.ds(pl.multiple_of(blk * bm * sub, bm * sub), bm * sub)],
                                      zsem).start()
            return carry
        lax.fori_loop(0, n_blocks, zstart, 0)

        def zwait(blk, carry):
            @pl.when(nz_ref[blk] != 0)
            def _():
                pltpu.make_async_copy(zbuf, xs_hbm.at[pl.ds(0, bm * sub)], zsem).wait()
            return carry
        lax.fori_loop(0, n_blocks, zwait, 0)

    def issue(t, carry):
        for k in range(TOP_K):
            dst = dest_ref[0, 0, t * TOP_K + k]
            pltpu.make_async_copy(x_ref.at[tile(t)], xs_hbm.at[tile(dst)], sem).start(priority=k % 2)
        return carry
    lax.fori_loop(0, tb, issue, 0)
    for k in range(TOP_K):
        pltpu.make_async_copy(x_ref, xs_hbm.at[pl.ds(0, tb * sub)], sem).wait()


def _dispatch(needs_zero, dest_blocks, xn3, sub, bm, tb):
    t = xn3.shape[0] // sub
    n_blocks = needs_zero.shape[0]
    grid_spec = pltpu.PrefetchScalarGridSpec(
        num_scalar_prefetch=1,
        grid=(t // tb,),
        in_specs=[pl.BlockSpec((1, 1, tb * TOP_K), lambda i, nz: (i, 0, 0), memory_space=pltpu.SMEM),
                  pl.BlockSpec((tb * sub, LANES), lambda i, nz: (i, 0))],
        out_specs=pl.BlockSpec(memory_space=pl.ANY),
        scratch_shapes=[pltpu.VMEM((bm * sub, LANES), F32), pltpu.SemaphoreType.DMA, pltpu.SemaphoreType.DMA],
    )
    return pl.pallas_call(
        functools.partial(_dispatch_kernel, tb=tb, bm=bm, n_blocks=n_blocks, sub=sub),
        grid_spec=grid_spec,
        out_shape=jax.ShapeDtypeStruct((n_blocks * bm * sub, LANES), F32),
        compiler_params=_params(("arbitrary",)),
        name="dispatch",
    )(needs_zero, dest_blocks, xn3)


def _scatter_rows(asg_ref, src, yt_hbm, sem, bm, sub, paced):
    tile = lambda r: pl.ds(pl.multiple_of(r * sub, sub), sub)
    if not paced:
        def issue(r, carry):
            pltpu.make_async_copy(src.at[tile(r)], yt_hbm.at[tile(asg_ref[0, 0, r])], sem).start()
            return carry
        lax.fori_loop(0, bm, issue, 0)
        return
    skew = jnp.int32(0)
    for r in range(bm):
        a = asg_ref[0, 0, r + skew]
        pltpu.make_async_copy(src.at[tile(r)], yt_hbm.at[tile(a)], sem).start(priority=r % 2)
        if r % 2:
            skew = skew + (a >> 31)
    return skew


def _expert_kernel(be_ref, na_ref, asg_ref, x_ref, w1g_ref, w1l_ref, b1g_ref, b1l_ref, w2_ref, b2_ref,
                   yt_hbm, ybuf0, ybuf1, sem, *, bm):
    i = pl.program_id(0)
    na = na_ref[0]
    sub = w1g_ref.shape[1] // LANES
    ybufs = (ybuf0, ybuf1)

    @pl.when(i == 0)
    def _():
        ybuf1[...] = jnp.zeros_like(ybuf1)

    def wait(slot):
        pltpu.make_async_copy(ybufs[slot], yt_hbm.at[pl.ds(0, bm * sub)], sem.at[slot]).wait()

    for slot in range(2):
        prev = 1 - slot

        @pl.when((i < na) & (i % 2 == slot))
        def _():
            done = _scatter_rows(asg_ref, ybufs[prev], yt_hbm, sem.at[prev], bm, sub, paced=True)
            xb = _load_row_tiles(x_ref, sub).astype(BF16)
            glu = jnp.dot(xb, w1g_ref[0], preferred_element_type=F32) + b1g_ref[0]
            lin = jnp.dot(xb, w1l_ref[0], preferred_element_type=F32) + b1l_ref[0]
            glu = jnp.minimum(glu, SWIGLU_LIMIT)
            lin = jnp.clip(lin, -SWIGLU_LIMIT, SWIGLU_LIMIT)
            act = glu * jax.nn.sigmoid(SWIGLU_ALPHA * glu) * (lin + 1.0)
            y = jnp.dot(act.astype(BF16), w2_ref[0], preferred_element_type=F32) + b2_ref[0]
            _store_row_tiles(ybufs[slot], y, zero=done)
            wait(prev)

        @pl.when((i == na) & (i % 2 == slot))
        def _():
            _scatter_rows(asg_ref, ybufs[prev], yt_hbm, sem.at[prev], bm, sub, paced=False)
            wait(prev)


def _experts(blk_expert, n_active, asg_blocks, xs, w1g, w1l, b1g, b1l, w2, b2, n_rows, bm):
    d = w1g.shape[1]
    sub = d // LANES
    n_blocks = xs.shape[0] // (bm * sub)
    de = w1g.shape[-1]
    last = lambda i, na: jnp.minimum(i, na[0] - 1)
    wmap = lambda i, be, na: (be[last(i, na)], 0, 0)
    grid_spec = pltpu.PrefetchScalarGridSpec(
        num_scalar_prefetch=2,
        grid=(n_blocks + 1,),
        in_specs=[pl.BlockSpec((1, 1, bm), lambda i, be, na: (jnp.minimum(i, na[0]), 0, 0),
                               memory_space=pltpu.SMEM),
                  pl.BlockSpec((bm * sub, LANES), lambda i, be, na: (last(i, na), 0)),
                  pl.BlockSpec((1, d, de), wmap), pl.BlockSpec((1, d, de), wmap),
                  pl.BlockSpec((1, 1, de), wmap), pl.BlockSpec((1, 1, de), wmap),
                  pl.BlockSpec((1, de, d), wmap), pl.BlockSpec((1, 1, d), wmap)],
        out_specs=pl.BlockSpec(memory_space=pl.ANY),
        scratch_shapes=[pltpu.VMEM((bm * sub, LANES), F32), pltpu.VMEM((bm * sub, LANES), F32),
                        pltpu.SemaphoreType.DMA((2,))],
    )
    return pl.pallas_call(
        functools.partial(_expert_kernel, bm=bm),
        grid_spec=grid_spec,
        out_shape=jax.ShapeDtypeStruct(((n_rows + bm) * sub, LANES), F32),
        compiler_params=_params(("arbitrary",)),
        name="experts",
    )(blk_expert, n_active, asg_blocks, xs, w1g, w1l, b1g, b1l, w2, b2)


def _combine_kernel(yt_ref, h2_ref, tg_ref, g_ref, o_ref, *, final_norm):
    tb, d = h2_ref.shape
    sub = d // LANES
    acc = h2_ref[...]
    for k in range(TOP_K):
        yk = jnp.concatenate([yt_ref[pl.ds(k * sub + c, tb, stride=TOP_K * sub), :] for c in range(sub)], axis=1)
        acc = acc + yk * tg_ref[:, k:k + 1]
    o_ref[...] = _rms(acc, g_ref[...]) if final_norm else acc


def _combine(yt, h2, tg, g, tb, final_norm):
    t, d = h2.shape
    row = lambda i: (i, 0)
    return pl.pallas_call(
        functools.partial(_combine_kernel, final_norm=final_norm),
        grid=(t // tb,),
        in_specs=[pl.BlockSpec((tb * TOP_K * (d // LANES), LANES), row),
                  pl.BlockSpec((tb, d), row), pl.BlockSpec((tb, LANES), row),
                  pl.BlockSpec((1, d), lambda i: (0, 0))],
        out_specs=pl.BlockSpec((tb, d), row),
        out_shape=jax.ShapeDtypeStruct((t, d), F32),
        compiler_params=_params(("parallel",)),
        name="combine",
    )(yt, h2, tg, g)


def _route(top_e, rank, counts, n_experts, bm):
    a = top_e.size
    padded = (counts + bm - 1) // bm * bm
    pend = jnp.cumsum(padded)
    pstart = pend - padded
    dest = (jnp.take(pstart, top_e) + rank).astype(I32)
    n_blocks = -(-a // bm) + n_experts
    blk_lo = jnp.arange(n_blocks, dtype=I32) * bm
    blk_expert = jnp.minimum(jnp.sum(blk_lo[:, None] >= pend[None, :], axis=1), n_experts - 1).astype(I32)
    n_active = (pend[-1] // bm).astype(I32)
    has_pad = jnp.any((blk_lo[:, None] + bm) == pend[None, :], axis=1)
    needs_zero = (has_pad | (jnp.arange(n_blocks) >= n_active)).astype(I32)
    order = jnp.argsort(top_e.reshape(a), stable=True).astype(I32)
    start = jnp.cumsum(counts) - counts
    slot = jnp.arange(n_blocks * bm, dtype=I32)
    e_slot = jnp.repeat(blk_expert, bm)
    q = slot - jnp.take(pstart, e_slot)
    valid = (q < jnp.take(counts, e_slot)) & (slot < pend[-1])
    src = jnp.clip(jnp.take(start, e_slot) + q, 0, a - 1)
    dump = a + slot % bm
    asg = jnp.where(valid, jnp.take(order, src), dump).astype(I32)
    asg = jnp.concatenate([dump[:bm], asg]).reshape(n_blocks + 1, 1, bm)
    return dest, blk_expert, n_active.reshape(1), needs_zero, asg


def _pad_cols(w, n):
    return jnp.pad(w, ((0, 0), (0, n - w.shape[1])))


def _layer(h, mem, l, p):
    b, s, d = h.shape
    t = b * s
    lambda_init = 0.8 - 0.6 * math.exp(-0.3 * l)
    n_main = 6 * GROUP
    w_in = p['w_in']
    w_main = w_in[:, :n_main].astype(BF16)
    w_i = _pad_cols(w_in[:, n_main:n_main + MLSTM_HEADS], LANES).astype(BF16)
    w_f = _pad_cols(w_in[:, n_main + MLSTM_HEADS:], LANES).astype(BF16)
    dq, dk, dv, mqk, mv, mo, gi, gf = _in_proj(h.reshape(t, d), p['norm_mix_g'].reshape(1, d),
                                               w_main, w_i, w_f, tm=min(512, t))
    r3 = lambda a: a.reshape(b, s, a.shape[-1])
    vec = lambda a: a.reshape(1, -1)
    od = _diff_attention(r3(dq), r3(dk), r3(dv), vec(p['lambda_q1']), vec(p['lambda_k1']),
                         vec(p['lambda_q2']), vec(p['lambda_k2']), vec(p['diff_norm_g']),
                         lambda_init, bq=min(512, s))
    hm = _mlstm(r3(mqk), r3(mv), r3(gi), r3(gf), r3(mo), p['conv_w'], vec(p['conv_b']),
                _pad_cols(vec(p['b_igate']), LANES), _pad_cols(vec(p['b_fgate']), LANES),
                vec(p['mlstm_norm_g']), bb=2 if b % 2 == 0 else 1)
    kmem, vmem = _kv_proj(mem, vec(p['norm_mem_g']), p['w_ckv'].astype(BF16))
    w_out = p['w_out'].astype(BF16)
    n_experts = p['w_router'].shape[1]
    wr = _pad_cols(p['w_router'], LANES)
    wr_hi = wr.astype(BF16)
    wr = jnp.concatenate([wr_hi, (wr - wr_hi.astype(F32)).astype(BF16)], axis=1)
    br =jnp.concatenate([vec(p['b_router']), jnp.full((1, LANES - n_experts), NEG_BIG, F32)], axis=1)
    h2, xn3, te, tg, cnt = _post(od, hm, h, kmem, vmem, w_out[:GROUP], w_out[GROUP:],
                                 vec(p['norm_xattn_g']), p['w_cq'].astype(BF16), p['w_co'].astype(BF16),
                                 vec(p['norm_ffn_g']), wr, br, tm=min(512, s))
    te = te.reshape(t, LANES)
    dest, blk_expert, n_active, needs_zero, asg = _route(te[:, :TOP_K], te[:, TOP_K:2 * TOP_K],
                                                         cnt[0, :n_experts].astype(I32), n_experts, MOE_BLOCK)
    tb_d = min(512, t)
    xs = _dispatch(needs_zero, dest.reshape(t // tb_d, 1, tb_d * TOP_K), xn3.reshape(t * (d // LANES), LANES),
                   d // LANES, MOE_BLOCK, tb_d)
    w1g, w1l = _split_w1(p['w1'], rb=256)
    yt = _experts(blk_expert, n_active, asg, xs, w1g, w1l,
                  p['b1'][:, None, 0::2], p['b1'][:, None, 1::2],
                  p['w2'].astype(BF16), p['b2'][:, None, :], t * TOP_K, MOE_BLOCK)
    return yt, h2.reshape(t, d), tg.reshape(t, LANES)


def kernel(x, mem, norm_mix_g, w_in, conv_w, conv_b, b_igate, b_fgate, mlstm_norm_g, lambda_q1, lambda_k1, lambda_q2, lambda_k2, diff_norm_g, w_out, norm_xattn_g, norm_mem_g, w_cq, w_ckv, w_co, norm_ffn_g, w_router, b_router, w1, b1, w2, b2, norm_final_g):
    stacked = dict(norm_mix_g=norm_mix_g, w_in=w_in, conv_w=conv_w, conv_b=conv_b, b_igate=b_igate,
                   b_fgate=b_fgate, mlstm_norm_g=mlstm_norm_g, lambda_q1=lambda_q1, lambda_k1=lambda_k1,
                   lambda_q2=lambda_q2, lambda_k2=lambda_k2, diff_norm_g=diff_norm_g, w_out=w_out,
                   norm_xattn_g=norm_xattn_g, norm_mem_g=norm_mem_g, w_cq=w_cq, w_ckv=w_ckv, w_co=w_co,
                   norm_ffn_g=norm_ffn_g, w_router=w_router, b_router=b_router, w1=w1, b1=b1, w2=w2, b2=b2)
    depth = w_in.shape[0]
    b, s, d = x.shape
    h = x
    for l in range(depth):
        p = {k: v[l] for k, v in stacked.items()}
        yt, h2, tg = _layer(h, mem, l, p)
        h = _combine(yt, h2, tg, norm_final_g.reshape(1, d), min(256, b * s),
                     final_norm=l == depth - 1).reshape(b, s, d)
    return h
```

```python
import functools
import math

import jax
import jax.numpy as jnp
from jax import lax
from jax.experimental import pallas as pl
from jax.experimental.pallas import tpu as pltpu

F32 = jnp.float32
BF16 = jnp.bfloat16
I32 = jnp.int32

RMS_EPS = 1e-5
LANES = 128
VMEM_LIMIT = 56 * 1024 * 1024

DIFF_HEADS = 4
DIFF_QK_DIM = 64
MLSTM_HEADS = 4
MLSTM_QK_DIM = 64
CONV_K = 4
CHUNK = 128
XATTN_HEADS = 4
TOP_K = 4
SWIGLU_ALPHA = 1.702
SWIGLU_LIMIT = 7.0
MOE_BLOCK = 512
GROUP = 512
NEG_BIG = -1e30


def _rms(x, g):
    return x * lax.rsqrt(jnp.mean(x * x, axis=-1, keepdims=True) + RMS_EPS) * g


def _store_row_tiles(ref, m):
    rows, d = m.shape
    sub = d // LANES
    for c in range(sub):
        ref[pl.ds(c, rows, stride=sub), :] = m[:, c * LANES:(c + 1) * LANES]


def _load_row_tiles(ref, sub):
    rows = ref.shape[0] // sub
    return jnp.concatenate([ref[pl.ds(c, rows, stride=sub), :] for c in range(sub)], axis=1)


def _params(sem, vmem=VMEM_LIMIT, flags=None):
    return pltpu.CompilerParams(dimension_semantics=sem, vmem_limit_bytes=vmem, flags=flags)


def _inproj_kernel(x_ref, g_ref, w_ref, wi_ref, wf_ref,
                   dq_ref, dk_ref, dv_ref, mqk_ref, mv_ref, mo_ref, gi_ref, gf_ref):
    xb = _rms(x_ref[...], g_ref[...]).astype(BF16)
    for n, o_ref in enumerate((dq_ref, dk_ref, dv_ref, mqk_ref, mv_ref, mo_ref)):
        o_ref[...] = jnp.dot(xb, w_ref[:, n * GROUP:(n + 1) * GROUP],
                             preferred_element_type=F32).astype(o_ref.dtype)
    gi_ref[...] = jnp.dot(xb, wi_ref[...], preferred_element_type=F32)
    gf_ref[...] = jnp.dot(xb, wf_ref[...], preferred_element_type=F32)


def _in_proj(x2, g, w_main, w_i, w_f, tm):
    t, d = x2.shape
    row = lambda i: (i, 0)
    const = lambda i: (0, 0)
    out_dtypes = (BF16, BF16, BF16, F32, BF16, F32)
    return pl.pallas_call(
        _inproj_kernel,
        grid=(t // tm,),
        in_specs=[pl.BlockSpec((tm, d), row), pl.BlockSpec((1, d), const),
                  pl.BlockSpec(w_main.shape, const), pl.BlockSpec(w_i.shape, const),
                  pl.BlockSpec(w_f.shape, const)],
        out_specs=[pl.BlockSpec((tm, GROUP), row)] * 6 + [pl.BlockSpec((tm, LANES), row)] * 2,
        out_shape=[jax.ShapeDtypeStruct((t, GROUP), dt) for dt in out_dtypes]
        + [jax.ShapeDtypeStruct((t, LANES), F32)] * 2,
        compiler_params=_params(("parallel",)),
        name="in_proj",
    )(x2, g, w_main, w_i, w_f)


def _diffattn_kernel(q_ref, k_ref, v_ref, lq1_ref, lk1_ref, lq2_ref, lk2_ref, g_ref, o_ref,
                     *, bq, lambda_init):
    s_len = q_ref.shape[1]
    lane = lax.broadcasted_iota(I32, (1, LANES), 1)
    lo = lane < DIFF_QK_DIM
    row = lax.broadcasted_iota(I32, (bq, bq), 0)
    col = lax.broadcasted_iota(I32, (bq, bq), 1)
    causal = col <= row
    lam = (jnp.exp(jnp.sum(lq1_ref[...] * lk1_ref[...], axis=-1, keepdims=True))
           - jnp.exp(jnp.sum(lq2_ref[...] * lk2_ref[...], axis=-1, keepdims=True)) + lambda_init)

    def step(qs, j, carry, masked):
        kj = k_ref[0, j * bq:(j + 1) * bq, :]
        vj = v_ref[0, j * bq:(j + 1) * bq, :]
        new = []
        for qm, (m, l, a) in zip(qs, carry):
            s = lax.dot_general(qm, kj, (((1,), (1,)), ((), ())), preferred_element_type=F32)
            if masked:
                s = jnp.where(causal, s, -jnp.inf)
            mn = jnp.maximum(m, jnp.max(s, axis=-1, keepdims=True))
            p = jnp.exp(s - mn)
            alpha = jnp.exp(m - mn)
            l = alpha * l + jnp.sum(p, axis=-1, keepdims=True)
            a = alpha * a + jnp.dot(p.astype(BF16), vj, preferred_element_type=F32)
            new.append((mn, l, a))
        return tuple(new)

    for i in range(s_len // bq):
        q = q_ref[0, i * bq:(i + 1) * bq, :] * jnp.asarray(DIFF_QK_DIM ** -0.5, BF16)
        zero = jnp.zeros_like(q)
        qs = (jnp.where(lo, q, zero), jnp.where(lo, zero, q))
        init_one = (jnp.full((bq, 1), -jnp.inf, F32), jnp.zeros((bq, 1), F32), jnp.zeros((bq, LANES), F32))
        carry = (init_one, init_one)
        for j in range(i):
            carry = step(qs, j, carry, False)
        (_, l1, a1), (_, l2, a2) = step(qs, i, carry, True)
        od = a1 / l1 - lam * (a2 / l2)
        o_ref[0, i * bq:(i + 1) * bq, :] = (_rms(od, g_ref[...]) * (1.0 - lambda_init)).astype(o_ref.dtype)


def _diff_attention(dq, dk, dv, lq1, lk1, lq2, lk2, g, lambda_init, bq):
    b, s, w = dq.shape
    blk = lambda bi, h: (bi, 0, h)
    vec = lambda bi, h: (0, 0)
    return pl.pallas_call(
        functools.partial(_diffattn_kernel, bq=bq, lambda_init=lambda_init),
        grid=(b, DIFF_HEADS),
        in_specs=[pl.BlockSpec((1, s, LANES), blk)] * 3
        + [pl.BlockSpec((1, DIFF_QK_DIM), vec)] * 4
        + [pl.BlockSpec((1, LANES), lambda bi, h: (0, h))],
        out_specs=pl.BlockSpec((1, s, LANES), blk),
        out_shape=jax.ShapeDtypeStruct((b, s, w), BF16),
        compiler_params=_params(("parallel", "parallel")),
        name="diff_attn",
    )(dq, dk, dv, lq1, lk1, lq2, lk2, g)


def _log_sigmoid(x):
    return -(jnp.maximum(-x, 0.0) + jnp.log1p(jnp.exp(-jnp.abs(x))))


def _mlstm_kernel(mqk_ref, mv_ref, gi_ref, gf_ref, mo_ref, cw_ref, cb_ref, bi_ref, bf_ref, ng_ref,
                  o_ref, conv_scr, c_scr, m_scr, *, bb):
    c = pl.program_id(1)
    L = CHUNK
    dk = MLSTM_QK_DIM
    nq = MLSTM_HEADS * dk
    tail = 8

    @pl.when(c == 0)
    def _():
        conv_scr[:, 0:tail, :] = jnp.zeros((bb, tail, 2 * nq), F32)
        c_scr[...] = jnp.zeros_like(c_scr)
        m_scr[...] = jnp.zeros_like(m_scr)

    row = lax.broadcasted_iota(I32, (L, L), 0)
    col = lax.broadcasted_iota(I32, (L, L), 1)
    causal = col <= row
    tril = causal.astype(F32)
    lane = lax.broadcasted_iota(I32, (1, LANES), 1)
    sub = lax.broadcasted_iota(I32, (LANES, 1), 0)
    ones_col = jnp.where(lane == 0, 1.0, 0.0).astype(BF16) * jnp.ones((L, 1), BF16)

    for b in range(bb):
        conv_scr[b, tail:tail + L, :] = mqk_ref[b]
        y = cb_ref[...]
        for j in range(CONV_K):
            y = y + conv_scr[b, pl.ds(tail - (CONV_K - 1) + j, L), :] * cw_ref[j:j + 1, :]
        conv_scr[b, 0:tail, :] = conv_scr[b, L:L + tail, :]
        qk = y * jax.nn.sigmoid(y)

        ig = gi_ref[b] + bi_ref[...]
        lf = _log_sigmoid(gf_ref[b] + bf_ref[...])
        bcum = jnp.dot(tril, lf, preferred_element_type=F32, precision=lax.Precision.HIGHEST)
        a_t = (ig - bcum).T

        for pair in range(MLSTM_HEADS // 2):
            q_pair = qk[:, pair * LANES:(pair + 1) * LANES] * (dk ** -0.5)
            k_pair = qk[:, nq + pair * LANES:nq + (pair + 1) * LANES]
            kt_pair = k_pair.T
            c_old = c_scr[b, pair]
            c_bf = c_old.astype(BF16)
            upd = jnp.zeros_like(c_old)
            decays = []
            for hh in range(2):
                h = 2 * pair + hh
                sel = (lane >= hh * dk) & (lane < (hh + 1) * dk)
                qm = jnp.where(sel, q_pair, 0.0).astype(BF16)
                km = jnp.where(sel, k_pair, 0.0).astype(BF16)
                selr = (sub >= hh * dk) & (sub < (hh + 1) * dk)
                ktm = jnp.where(selr, kt_pair, 0.0).astype(BF16)
                v_h = mv_ref[b, :, h * LANES:(h + 1) * LANES]
                v_aug = jnp.concatenate([v_h, ones_col], axis=1)

                m_st = m_scr[b, h, 0:1, 0:1]
                bc = bcum[:, h:h + 1]
                ic = ig[:, h:h + 1]
                a_m = jnp.where(causal, a_t[h:h + 1, :], -jnp.inf)
                inter = bc + m_st
                m_t = jnp.maximum(inter, bc + jnp.max(a_m, axis=-1, keepdims=True))
                qkt = lax.dot_general(qm, km, (((1,), (1,)), ((), ())), preferred_element_type=F32)
                w = qkt * jnp.exp(a_m + (bc - m_t))
                g = jnp.exp(inter - m_t)
                num_aug = (g * jnp.dot(qm, c_bf, preferred_element_type=F32)
                           + jnp.dot(w.astype(BF16), v_aug, preferred_element_type=F32))
                num = num_aug[:, :LANES]
                den = num_aug[:, LANES:LANES + 1]
                h_t = num / jnp.maximum(jnp.abs(den), jnp.exp(-m_t))

                b_last = bc[L - 1:L, :]
                gs = b_last - bc + ic
                m_new = jnp.maximum(b_last + m_st, jnp.max(gs, axis=0, keepdims=True))
                decays.append(jnp.exp(b_last + m_st - m_new))
                ws = jnp.exp(gs - m_new)
                wsv = (ws * v_aug.astype(F32)).astype(BF16)
                upd = upd + jnp.dot(ktm, wsv, preferred_element_type=F32)
                m_scr[b, h, 0:1, :] = jnp.broadcast_to(m_new, (1, LANES))

                hn = _rms(h_t, ng_ref[:, h * LANES:(h + 1) * LANES])
                gate = jax.nn.sigmoid(mo_ref[b, :, h * LANES:(h + 1) * LANES])
                o_ref[b, :, h * LANES:(h + 1) * LANES] = (hn * gate).astype(o_ref.dtype)
            d_rows = jnp.where(sub < dk, decays[0], decays[1])
            c_scr[b, pair] = d_rows * c_old + upd


def _mlstm(mqk, mv, gi, gf, mo, conv_w, conv_b, b_i, b_f, norm_g, bb):
    b, s, w = mqk.shape
    blk = lambda bi, c: (bi, c, 0)
    const = lambda bi, c: (0, 0)
    return pl.pallas_call(
        functools.partial(_mlstm_kernel, bb=bb),
        grid=(b // bb, s // CHUNK),
        in_specs=[pl.BlockSpec((bb, CHUNK, w), blk), pl.BlockSpec((bb, CHUNK, w), blk),
                  pl.BlockSpec((bb, CHUNK, LANES), blk), pl.BlockSpec((bb, CHUNK, LANES), blk),
                  pl.BlockSpec((bb, CHUNK, w), blk),
                  pl.BlockSpec(conv_w.shape, const), pl.BlockSpec(conv_b.shape, const),
                  pl.BlockSpec(b_i.shape, const), pl.BlockSpec(b_f.shape, const),
                  pl.BlockSpec(norm_g.shape, const)],
        out_specs=pl.BlockSpec((bb, CHUNK, w), blk),
        out_shape=jax.ShapeDtypeStruct((b, s, w), BF16),
        scratch_shapes=[pltpu.VMEM((bb, CHUNK + 8, w), F32),
                        pltpu.VMEM((bb, MLSTM_HEADS // 2, LANES, 2 * LANES), F32),
                        pltpu.VMEM((bb, MLSTM_HEADS, 8, LANES), F32)],
        compiler_params=_params(("parallel", "arbitrary")),
        name="mlstm",
    )(mqk, mv, gi, gf, mo, conv_w, conv_b, b_i, b_f, norm_g)


def _kvproj_kernel(mem_ref, g_ref, w_ref, k_ref, v_ref):
    d = mem_ref.shape[-1]
    mb = _rms(mem_ref[0], g_ref[...]).astype(BF16)
    k_ref[0] = jnp.dot(mb, w_ref[:, :d], preferred_element_type=F32).astype(k_ref.dtype)
    v_ref[0] = jnp.dot(mb, w_ref[:, d:], preferred_element_type=F32).astype(v_ref.dtype)


def _kv_proj(mem, g, w_ckv):
    b, m, d = mem.shape
    blk = lambda bi: (bi, 0, 0)
    const = lambda bi: (0, 0)
    return pl.pallas_call(
        _kvproj_kernel,
        grid=(b,),
        in_specs=[pl.BlockSpec((1, m, d), blk), pl.BlockSpec((1, d), const),
                  pl.BlockSpec(w_ckv.shape, const)],
        out_specs=[pl.BlockSpec((1, m, d), blk)] * 2,
        out_shape=[jax.ShapeDtypeStruct((b, m, d), BF16)] * 2,
        compiler_params=_params(("parallel",)),
        name="kv_proj",
    )(mem, g, w_ckv)


def _post_kernel(od_ref, hm_ref, x_ref, k_ref, v_ref, wo1_ref, wo2_ref, g2_ref, wcq_ref, wco_ref,
                 g3_ref, wr_ref, br_ref, h2_ref, xn3_ref, te_ref, tg_ref, cnt_ref, base_scr):
    d = x_ref.shape[-1]
    hd = d // XATTN_HEADS
    h1 = (x_ref[0] + jnp.dot(od_ref[0], wo1_ref[...], preferred_element_type=F32)
          + jnp.dot(hm_ref[0], wo2_ref[...], preferred_element_type=F32))
    q = jnp.dot(_rms(h1, g2_ref[...]).astype(BF16), wcq_ref[...], preferred_element_type=F32)
    q = (q * (hd ** -0.5)).astype(BF16)
    heads = []
    for h in range(XATTN_HEADS):
        sl = slice(h * hd, (h + 1) * hd)
        s = lax.dot_general(q[:, sl], k_ref[0, :, sl], (((1,), (1,)), ((), ())),
                            preferred_element_type=F32)
        e = jnp.exp(s - jnp.max(s, axis=-1, keepdims=True))
        p = e / jnp.sum(e, axis=-1, keepdims=True)
        heads.append(jnp.dot(p.astype(BF16), v_ref[0, :, sl], preferred_element_type=F32))
    o = jnp.concatenate(heads, axis=1).astype(BF16)
    h2 = h1 + jnp.dot(o, wco_ref[...], preferred_element_type=F32)
    h2_ref[0] = h2
    xn3 = _rms(h2, g3_ref[...])
    _store_row_tiles(xn3_ref.at[0], xn3)
    x_hi = xn3.astype(BF16)
    x_lo = (xn3 - x_hi.astype(F32)).astype(BF16)
    hh_hl = jnp.dot(x_hi, wr_ref[...], preferred_element_type=F32)
    lh = jnp.dot(x_lo, wr_ref[:, :LANES], preferred_element_type=F32)
    logits = hh_hl[:, :LANES] + (hh_hl[:, LANES:] + lh) + br_ref[...]
    lane = lax.broadcasted_iota(I32, logits.shape, 1)
    lane_f = lane.astype(F32)
    cur = logits
    te = jnp.zeros(logits.shape, F32)
    chosen = jnp.zeros(logits.shape, F32)
    vals, hits = [], []
    for k in range(TOP_K):
        m = jnp.max(cur, axis=-1, keepdims=True)
        idx = jnp.min(jnp.where(cur == m, lane_f, float(LANES)), axis=-1, keepdims=True)
        hit = lane_f == idx
        vals.append(m)
        hits.append(hit)
        te = jnp.where(lane == k, idx, te)
        chosen = jnp.where(hit, 1.0, chosen)
        cur = jnp.where(hit, NEG_BIG * 2.0, cur)
    es = [jnp.exp(v - vals[0]) for v in vals]
    tot = es[0] + es[1] + es[2] + es[3]
    tg = jnp.zeros(logits.shape, F32)
    for k in range(TOP_K):
        tg = jnp.where(lane == k, es[k] / tot, tg)
    tg_ref[0] = tg

    @pl.when((pl.program_id(0) == 0) & (pl.program_id(1) == 0))
    def _():
        base_scr[...] = jnp.zeros_like(base_scr)
    tm = logits.shape[0]
    earlier = (lax.broadcasted_iota(I32, (tm, tm), 1) < lax.broadcasted_iota(I32, (tm, tm), 0)).astype(BF16)
    prior = jnp.dot(earlier, chosen.astype(BF16), preferred_element_type=F32) + base_scr[...]
    for k in range(TOP_K):
        rank = jnp.sum(jnp.where(hits[k], prior, 0.0), axis=-1, keepdims=True)
        te = jnp.where(lane == TOP_K + k, rank, te)
    te_ref[0] = te.astype(I32)
    base_scr[...] = base_scr[...] + jnp.sum(chosen, axis=0, keepdims=True)
    cnt_ref[...] = base_scr[...]


def _post(od, hm, x, kmem, vmem, wo1, wo2, g2, wcq, wco, g3, wr, br, tm):
    b, s, d = x.shape
    w = od.shape[-1]
    m = kmem.shape[1]
    blk = lambda bi, i: (bi, i, 0)
    mem = lambda bi, i: (bi, 0, 0)
    const = lambda bi, i: (0, 0)
    full = lambda a: pl.BlockSpec(a.shape, const)
    return pl.pallas_call(
        _post_kernel,
        grid=(b, s // tm),
        in_specs=[pl.BlockSpec((1, tm, w), blk), pl.BlockSpec((1, tm, w), blk),
                  pl.BlockSpec((1, tm, d), blk), pl.BlockSpec((1, m, d), mem),
                  pl.BlockSpec((1, m, d), mem), full(wo1), full(wo2), full(g2), full(wcq),
                  full(wco), full(g3), full(wr), full(br)],
        out_specs=[pl.BlockSpec((1, tm, d), blk),
                   pl.BlockSpec((1, tm * (d // LANES), LANES), blk),
                   pl.BlockSpec((1, tm, LANES), blk), pl.BlockSpec((1, tm, LANES), blk),
                   pl.BlockSpec((1, LANES), const)],
        out_shape=[jax.ShapeDtypeStruct((b, s, d), F32), jax.ShapeDtypeStruct((b, s * (d // LANES), LANES), F32),
                   jax.ShapeDtypeStruct((b, s, LANES), I32), jax.ShapeDtypeStruct((b, s, LANES), F32),
                   jax.ShapeDtypeStruct((1, LANES), F32)],
        scratch_shapes=[pltpu.VMEM((1, LANES), F32)],
        compiler_params=_params(("arbitrary", "arbitrary")),
        name="post_mixer",
    )(od, hm, x, kmem, vmem, wo1, wo2, g2, wcq, wco, g3, wr, br)


def _split_kernel(w_ref, p_ref, wg_ref, wl_ref):
    wb = w_ref[0].astype(BF16)
    two = 2 * LANES
    for g in range(wb.shape[1] // two):
        r = jnp.dot(wb[:, g * two:(g + 1) * two], p_ref[...], preferred_element_type=F32)
        wg_ref[0, :, g * LANES:(g + 1) * LANES] = r[:, :LANES].astype(BF16)
        wl_ref[0, :, g * LANES:(g + 1) * LANES] = r[:, LANES:].astype(BF16)


def _split_w1(w1, rb):
    e, d, de2 = w1.shape
    two = 2 * LANES
    src = jnp.arange(two)[:, None]
    dst = jnp.arange(two)[None, :]
    perm = jnp.where(dst < LANES, src == 2 * dst, src == 2 * (dst - LANES) + 1).astype(BF16)
    blk = lambda ei, r: (ei, r, 0)
    return pl.pallas_call(
        _split_kernel,
        grid=(e, d // rb),
        in_specs=[pl.BlockSpec((1, rb, de2), blk), pl.BlockSpec((two, two), lambda ei, r: (0, 0))],
        out_specs=[pl.BlockSpec((1, rb, de2 // 2), blk)] * 2,
        out_shape=[jax.ShapeDtypeStruct((e, d, de2 // 2), BF16)] * 2,
        compiler_params=_params(("parallel", "parallel")),
        name="split_w1",
    )(w1, perm)


def _dispatch_kernel(nz_ref, dest_ref, x_ref, xs_hbm, zbuf, sem, zsem, *, tb, bm, n_blocks, sub):
    i = pl.program_id(0)
    tile = lambda r: pl.ds(pl.multiple_of(r * sub, sub), sub)

    @pl.when(i == 0)
    def _():
        zbuf[...] = jnp.zeros_like(zbuf)

        def zstart(blk, carry):
            @pl.when(nz_ref[blk] != 0)
            def _():
                pltpu.make_async_copy(zbuf, xs_hbm.at[pl.ds(pl.multiple_of(blk * bm * sub, bm * sub), bm * sub)],
                                      zsem).start()
            return carry
        lax.fori_loop(0, n_blocks, zstart, 0)

        def zwait(blk, carry):
            @pl.when(nz_ref[blk] != 0)
            def _():
                pltpu.make_async_copy(zbuf, xs_hbm.at[pl.ds(0, bm * sub)], zsem).wait()
            return carry
        lax.fori_loop(0, n_blocks, zwait, 0)

    def issue(t, carry):
        for k in range(TOP_K):
            dst = dest_ref[0, 0, t * TOP_K + k]
            pltpu.make_async_copy(x_ref.at[tile(t)], xs_hbm.at[tile(dst)], sem).start(priority=k % 2)
        return carry
    lax.fori_loop(0, tb, issue, 0)
    for k in range(TOP_K):
        pltpu.make_async_copy(x_ref, xs_hbm.at[pl.ds(0, tb * sub)], sem).wait()


def _dispatch(needs_zero, dest_blocks, xn3, sub, bm, tb):
    t = xn3.shape[0] // sub
    n_blocks = needs_zero.shape[0]
    grid_spec = pltpu.PrefetchScalarGridSpec(
        num_scalar_prefetch=1,
        grid=(t // tb,),
        in_specs=[pl.BlockSpec((1, 1, tb * TOP_K), lambda i, nz: (i, 0, 0), memory_space=pltpu.SMEM),
                  pl.BlockSpec((tb * sub, LANES), lambda i, nz: (i, 0))],
        out_specs=pl.BlockSpec(memory_space=pl.ANY),
        scratch_shapes=[pltpu.VMEM((bm * sub, LANES), F32), pltpu.SemaphoreType.DMA, pltpu.SemaphoreType.DMA],
    )
    return pl.pallas_call(
        functools.partial(_dispatch_kernel, tb=tb, bm=bm, n_blocks=n_blocks, sub=sub),
        grid_spec=grid_spec,
        out_shape=jax.ShapeDtypeStruct((n_blocks * bm * sub, LANES), F32),
        compiler_params=_params(("arbitrary",)),
        name="dispatch",
    )(needs_zero, dest_blocks, xn3)


def _expert_kernel(be_ref, na_ref, x_ref, w1g_ref, w1l_ref, b1g_ref, b1l_ref, w2_ref, b2_ref, y_ref):
    i = pl.program_id(0)

    @pl.when(i < na_ref[0])
    def _():
        xb = _load_row_tiles(x_ref, w1g_ref.shape[1] // LANES).astype(BF16)
        glu = jnp.dot(xb, w1g_ref[0], preferred_element_type=F32) + b1g_ref[0]
        lin = jnp.dot(xb, w1l_ref[0], preferred_element_type=F32) + b1l_ref[0]
        glu = jnp.minimum(glu, SWIGLU_LIMIT)
        lin = jnp.clip(lin, -SWIGLU_LIMIT, SWIGLU_LIMIT)
        act = glu * jax.nn.sigmoid(SWIGLU_ALPHA * glu) * (lin + 1.0)
        _store_row_tiles(y_ref, jnp.dot(act.astype(BF16), w2_ref[0], preferred_element_type=F32) + b2_ref[0])

    @pl.when(i >= na_ref[0])
    def _():
        y_ref[...] = jnp.zeros_like(y_ref)


def _experts(blk_expert, n_active, xs, w1g, w1l, b1g, b1l, w2, b2, bm):
    d = w1g.shape[1]
    sub = d // LANES
    n_pad = xs.shape[0] // sub
    n_blocks = n_pad // bm
    de = w1g.shape[-1]
    last = lambda i, na: jnp.minimum(i, na[0] - 1)
    wmap = lambda i, be, na: (be[last(i, na)], 0, 0)
    grid_spec = pltpu.PrefetchScalarGridSpec(
        num_scalar_prefetch=2,
        grid=(n_blocks,),
        in_specs=[pl.BlockSpec((bm * sub, LANES), lambda i, be, na: (last(i, na), 0)),
                  pl.BlockSpec((1, d, de), wmap), pl.BlockSpec((1, d, de), wmap),
                  pl.BlockSpec((1, 1, de), wmap), pl.BlockSpec((1, 1, de), wmap),
                  pl.BlockSpec((1, de, d), wmap), pl.BlockSpec((1, 1, d), wmap)],
        out_specs=pl.BlockSpec((bm * sub, LANES), lambda i, be, na: (i, 0)),
    )
    return pl.pallas_call(
        _expert_kernel,
        grid_spec=grid_spec,
        out_shape=jax.ShapeDtypeStruct((n_pad * sub, LANES), F32),
        compiler_params=_params(("arbitrary",)),
        name="experts",
    )(blk_expert, n_active, xs, w1g, w1l, b1g, b1l, w2, b2)


def _combine_kernel(dest_ref, dnext_ref, y_hbm, h2_ref, tg_ref, g_ref, o_ref, buf, sem, *, tb, final_norm):
    i = pl.program_id(0)
    sub = h2_ref.shape[1] // LANES
    tile = lambda r: pl.ds(pl.multiple_of(r * sub, sub), sub)
    slot = i % 2

    def gather(d_ref, s):
        def issue(t, carry):
            for k in range(TOP_K):
                dst = d_ref[0, 0, t * TOP_K + k]
                pltpu.make_async_copy(y_hbm.at[tile(dst)], buf.at[s, k, tile(t)], sem.at[s]).start(priority=k % 2)
            return carry
        lax.fori_loop(0, tb, issue, 0)

    @pl.when(i == 0)
    def _():
        gather(dest_ref, 0)

    @pl.when(i + 1 < pl.num_programs(0))
    def _():
        gather(dnext_ref, 1 - slot)

    for k in range(TOP_K):
        pltpu.make_async_copy(y_hbm.at[pl.ds(0, tb * sub)], buf.at[slot, k], sem.at[slot]).wait()
    acc = h2_ref[...]
    for k in range(TOP_K):
        acc = acc + _load_row_tiles(buf.at[slot, k], sub) * tg_ref[:, k:k + 1]
    o_ref[...] = _rms(acc, g_ref[...]) if final_norm else acc


def _combine(dest_blocks, y, h2, tg, g, tb, final_norm):
    t, d = h2.shape
    row = lambda i: (i, 0)
    nb = t // tb
    idx = lambda shift: pl.BlockSpec((1, 1, tb * TOP_K), lambda i: (jnp.minimum(i + shift, nb - 1), 0, 0),
                                     memory_space=pltpu.SMEM)
    return pl.pallas_call(
        functools.partial(_combine_kernel, tb=tb, final_norm=final_norm),
        grid=(nb,),
        in_specs=[idx(0), idx(1), pl.BlockSpec(memory_space=pl.ANY),
                  pl.BlockSpec((tb, d), row), pl.BlockSpec((tb, LANES), row),
                  pl.BlockSpec((1, d), lambda i: (0, 0))],
        out_specs=pl.BlockSpec((tb, d), row),
        out_shape=jax.ShapeDtypeStruct((t, d), F32),
        scratch_shapes=[pltpu.VMEM((2, TOP_K, tb * (d // LANES), LANES), F32), pltpu.SemaphoreType.DMA((2,))],
        compiler_params=_params(("arbitrary",)),
        name="combine",
    )(dest_blocks, dest_blocks, y, h2, tg, g)


def _route(top_e, rank, counts, n_experts, bm):
    a = top_e.size
    padded = (counts + bm - 1) // bm * bm
    pend = jnp.cumsum(padded)
    pstart = pend - padded
    dest = (jnp.take(pstart, top_e) + rank).astype(I32)
    n_blocks = -(-a // bm) + n_experts
    blk_lo = jnp.arange(n_blocks, dtype=I32) * bm
    blk_expert = jnp.minimum(jnp.sum(blk_lo[:, None] >= pend[None, :], axis=1), n_experts - 1).astype(I32)
    n_active = (pend[-1] // bm).astype(I32)
    has_pad = jnp.any((blk_lo[:, None] + bm) == pend[None, :], axis=1)
    needs_zero = (has_pad | (jnp.arange(n_blocks) >= n_active)).astype(I32)
    return dest, blk_expert, n_active.reshape(1), needs_zero


def _pad_cols(w, n):
    return jnp.pad(w, ((0, 0), (0, n - w.shape[1])))


def _layer(h, mem, l, p):
    b, s, d = h.shape
    t = b * s
    lambda_init = 0.8 - 0.6 * math.exp(-0.3 * l)
    n_main = 6 * GROUP
    w_in = p['w_in']
    w_main = w_in[:, :n_main].astype(BF16)
    w_i = _pad_cols(w_in[:, n_main:n_main + MLSTM_HEADS], LANES).astype(BF16)
    w_f = _pad_cols(w_in[:, n_main + MLSTM_HEADS:], LANES).astype(BF16)
    dq, dk, dv, mqk, mv, mo, gi, gf = _in_proj(h.reshape(t, d), p['norm_mix_g'].reshape(1, d),
                                               w_main, w_i, w_f, tm=min(512, t))
    r3 = lambda a: a.reshape(b, s, a.shape[-1])
    vec = lambda a: a.reshape(1, -1)
    od = _diff_attention(r3(dq), r3(dk), r3(dv), vec(p['lambda_q1']), vec(p['lambda_k1']),
                         vec(p['lambda_q2']), vec(p['lambda_k2']), vec(p['diff_norm_g']),
                         lambda_init, bq=min(512, s))
    hm = _mlstm(r3(mqk), r3(mv), r3(gi), r3(gf), r3(mo), p['conv_w'], vec(p['conv_b']),
                _pad_cols(vec(p['b_igate']), LANES), _pad_cols(vec(p['b_fgate']), LANES),
                vec(p['mlstm_norm_g']), bb=2 if b % 2 == 0 else 1)
    kmem, vmem = _kv_proj(mem, vec(p['norm_mem_g']), p['w_ckv'].astype(BF16))
    w_out = p['w_out'].astype(BF16)
    n_experts = p['w_router'].shape[1]
    wr = _pad_cols(p['w_router'], LANES)
    wr_hi = wr.astype(BF16)
    wr = jnp.concatenate([wr_hi, (wr - wr_hi.astype(F32)).astype(BF16)], axis=1)
    br =jnp.concatenate([vec(p['b_router']), jnp.full((1, LANES - n_experts), NEG_BIG, F32)], axis=1)
    h2, xn3, te, tg, cnt = _post(od, hm, h, kmem, vmem, w_out[:GROUP], w_out[GROUP:],
                                 vec(p['norm_xattn_g']), p['w_cq'].astype(BF16), p['w_co'].astype(BF16),
                                 vec(p['norm_ffn_g']), wr, br, tm=min(512, s))
    te = te.reshape(t, LANES)
    dest, blk_expert, n_active, needs_zero = _route(te[:, :TOP_K], te[:, TOP_K:2 * TOP_K],
                                                    cnt[0, :n_experts].astype(I32), n_experts, MOE_BLOCK)
    tb_d = min(512, t)
    xs = _dispatch(needs_zero, dest.reshape(t // tb_d, 1, tb_d * TOP_K), xn3.reshape(t * (d // LANES), LANES),
                   d // LANES, MOE_BLOCK, tb_d)
    w1g, w1l = _split_w1(p['w1'], rb=256)
    y = _experts(blk_expert, n_active, xs, w1g, w1l,
                 p['b1'][:, None, 0::2], p['b1'][:, None, 1::2],
                 p['w2'].astype(BF16), p['b2'][:, None, :], MOE_BLOCK)
    tb = min(256, t)
    return y, dest.reshape(t // tb, 1, tb * TOP_K), h2.reshape(t, d), tg.reshape(t, LANES), tb


def kernel(x, mem, norm_mix_g, w_in, conv_w, conv_b, b_igate, b_fgate, mlstm_norm_g, lambda_q1, lambda_k1, lambda_q2, lambda_k2, diff_norm_g, w_out, norm_xattn_g, norm_mem_g, w_cq, w_ckv, w_co, norm_ffn_g, w_router, b_router, w1, b1, w2, b2, norm_final_g):
    stacked = dict(norm_mix_g=norm_mix_g, w_in=w_in, conv_w=conv_w, conv_b=conv_b, b_igate=b_igate,
                   b_fgate=b_fgate, mlstm_norm_g=mlstm_norm_g, lambda_q1=lambda_q1, lambda_k1=lambda_k1,
                   lambda_q2=lambda_q2, lambda_k2=lambda_k2, diff_norm_g=diff_norm_g, w_out=w_out,
                   norm_xattn_g=norm_xattn_g, norm_mem_g=norm_mem_g, w_cq=w_cq, w_ckv=w_ckv, w_co=w_co,
                   norm_ffn_g=norm_ffn_g, w_router=w_router, b_router=b_router, w1=w1, b1=b1, w2=w2, b2=b2)
    depth = w_in.shape[0]
    b, s, d = x.shape
    h = x
    for l in range(depth):
        p = {k: v[l] for k, v in stacked.items()}
        y, dest_blocks, h2, tg, tb = _layer(h, mem, l, p)
        h = _combine(dest_blocks, y, h2, tg, norm_final_g.reshape(1, d), tb,
                     final_norm=l == depth - 1).reshape(b, s, d)
    return h
```

```python
import functools
import math

import jax
import jax.numpy as jnp
from jax import lax
from jax.experimental import pallas as pl
from jax.experimental.pallas import tpu as pltpu

F32 = jnp.float32
BF16 = jnp.bfloat16
I32 = jnp.int32

RMS_EPS = 1e-5
LANES = 128
VMEM_LIMIT = 56 * 1024 * 1024

DIFF_HEADS = 4
DIFF_QK_DIM = 64
MLSTM_HEADS = 4
MLSTM_QK_DIM = 64
CONV_K = 4
CHUNK = 128
XATTN_HEADS = 4
TOP_K = 4
SWIGLU_ALPHA = 1.702
SWIGLU_LIMIT = 7.0
MOE_BLOCK = 512
GROUP = 512
NEG_BIG = -1e30


def _rms(x, g):
    return x * lax.rsqrt(jnp.mean(x * x, axis=-1, keepdims=True) + RMS_EPS) * g


def _store_row_tiles(ref, m):
    rows, d = m.shape
    sub = d // LANES
    for c in range(sub):
        ref[pl.ds(c, rows, stride=sub), :] = m[:, c * LANES:(c + 1) * LANES]


def _load_row_tiles(ref, sub):
    rows = ref.shape[0] // sub
    return jnp.concatenate([ref[pl.ds(c, rows, stride=sub), :] for c in range(sub)], axis=1)


def _params(sem, vmem=VMEM_LIMIT, flags=None):
    return pltpu.CompilerParams(dimension_semantics=sem, vmem_limit_bytes=vmem, flags=flags)


def _inproj_kernel(x_ref, g_ref, w_ref, wi_ref, wf_ref,
                   dq_ref, dk_ref, dv_ref, mqk_ref, mv_ref, mo_ref, gi_ref, gf_ref):
    xb = _rms(x_ref[...], g_ref[...]).astype(BF16)
    for n, o_ref in enumerate((dq_ref, dk_ref, dv_ref, mqk_ref, mv_ref, mo_ref)):
        o_ref[...] = jnp.dot(xb, w_ref[:, n * GROUP:(n + 1) * GROUP],
                             preferred_element_type=F32).astype(o_ref.dtype)
    gi_ref[...] = jnp.dot(xb, wi_ref[...], preferred_element_type=F32)
    gf_ref[...] = jnp.dot(xb, wf_ref[...], preferred_element_type=F32)


def _in_proj(x2, g, w_main, w_i, w_f, tm):
    t, d = x2.shape
    row = lambda i: (i, 0)
    const = lambda i: (0, 0)
    out_dtypes = (BF16, BF16, BF16, F32, BF16, F32)
    return pl.pallas_call(
        _inproj_kernel,
        grid=(t // tm,),
        in_specs=[pl.BlockSpec((tm, d), row), pl.BlockSpec((1, d), const),
                  pl.BlockSpec(w_main.shape, const), pl.BlockSpec(w_i.shape, const),
                  pl.BlockSpec(w_f.shape, const)],
        out_specs=[pl.BlockSpec((tm, GROUP), row)] * 6 + [pl.BlockSpec((tm, LANES), row)] * 2,
        out_shape=[jax.ShapeDtypeStruct((t, GROUP), dt) for dt in out_dtypes]
        + [jax.ShapeDtypeStruct((t, LANES), F32)] * 2,
        compiler_params=_params(("parallel",)),
        name="in_proj",
    )(x2, g, w_main, w_i, w_f)


def _diffattn_kernel(q_ref, k_ref, v_ref, lq1_ref, lk1_ref, lq2_ref, lk2_ref, g_ref, o_ref,
                     *, bq, lambda_init):
    s_len = q_ref.shape[1]
    lane = lax.broadcasted_iota(I32, (1, LANES), 1)
    lo = lane < DIFF_QK_DIM
    row = lax.broadcasted_iota(I32, (bq, bq), 0)
    col = lax.broadcasted_iota(I32, (bq, bq), 1)
    causal = col <= row
    lam = (jnp.exp(jnp.sum(lq1_ref[...] * lk1_ref[...], axis=-1, keepdims=True))
           - jnp.exp(jnp.sum(lq2_ref[...] * lk2_ref[...], axis=-1, keepdims=True)) + lambda_init)

    def step(qs, j, carry, masked):
        kj = k_ref[0, j * bq:(j + 1) * bq, :]
        vj = v_ref[0, j * bq:(j + 1) * bq, :]
        new = []
        for qm, (m, l, a) in zip(qs, carry):
            s = lax.dot_general(qm, kj, (((1,), (1,)), ((), ())), preferred_element_type=F32)
            if masked:
                s = jnp.where(causal, s, -jnp.inf)
            mn = jnp.maximum(m, jnp.max(s, axis=-1, keepdims=True))
            p = jnp.exp(s - mn)
            alpha = jnp.exp(m - mn)
            l = alpha * l + jnp.sum(p, axis=-1, keepdims=True)
            a = alpha * a + jnp.dot(p.astype(BF16), vj, preferred_element_type=F32)
            new.append((mn, l, a))
        return tuple(new)

    for i in range(s_len // bq):
        q = q_ref[0, i * bq:(i + 1) * bq, :] * jnp.asarray(DIFF_QK_DIM ** -0.5, BF16)
        zero = jnp.zeros_like(q)
        qs = (jnp.where(lo, q, zero), jnp.where(lo, zero, q))
        init_one = (jnp.full((bq, 1), -jnp.inf, F32), jnp.zeros((bq, 1), F32), jnp.zeros((bq, LANES), F32))
        carry = (init_one, init_one)
        for j in range(i):
            carry = step(qs, j, carry, False)
        (_, l1, a1), (_, l2, a2) = step(qs, i, carry, True)
        od = a1 / l1 - lam * (a2 / l2)
        o_ref[0, i * bq:(i + 1) * bq, :] = (_rms(od, g_ref[...]) * (1.0 - lambda_init)).astype(o_ref.dtype)


def _diff_attention(dq, dk, dv, lq1, lk1, lq2, lk2, g, lambda_init, bq):
    b, s, w = dq.shape
    blk = lambda bi, h: (bi, 0, h)
    vec = lambda bi, h: (0, 0)
    return pl.pallas_call(
        functools.partial(_diffattn_kernel, bq=bq, lambda_init=lambda_init),
        grid=(b, DIFF_HEADS),
        in_specs=[pl.BlockSpec((1, s, LANES), blk)] * 3
        + [pl.BlockSpec((1, DIFF_QK_DIM), vec)] * 4
        + [pl.BlockSpec((1, LANES), lambda bi, h: (0, h))],
        out_specs=pl.BlockSpec((1, s, LANES), blk),
        out_shape=jax.ShapeDtypeStruct((b, s, w), BF16),
        compiler_params=_params(("parallel", "parallel")),
        name="diff_attn",
    )(dq, dk, dv, lq1, lk1, lq2, lk2, g)


def _log_sigmoid(x):
    return -(jnp.maximum(-x, 0.0) + jnp.log1p(jnp.exp(-jnp.abs(x))))


def _mlstm_kernel(mqk_ref, mv_ref, gi_ref, gf_ref, mo_ref, cw_ref, cb_ref, bi_ref, bf_ref, ng_ref,
                  o_ref, conv_scr, c_scr, m_scr, *, bb):
    c = pl.program_id(1)
    L = CHUNK
    dk = MLSTM_QK_DIM
    nq = MLSTM_HEADS * dk
    tail = 8

    @pl.when(c == 0)
    def _():
        conv_scr[:, 0:tail, :] = jnp.zeros((bb, tail, 2 * nq), F32)
        c_scr[...] = jnp.zeros_like(c_scr)
        m_scr[...] = jnp.zeros_like(m_scr)

    row = lax.broadcasted_iota(I32, (L, L), 0)
    col = lax.broadcasted_iota(I32, (L, L), 1)
    causal = col <= row
    tril = causal.astype(F32)
    lane = lax.broadcasted_iota(I32, (1, LANES), 1)
    sub = lax.broadcasted_iota(I32, (LANES, 1), 0)
    ones_col = jnp.where(lane == 0, 1.0, 0.0).astype(BF16) * jnp.ones((L, 1), BF16)

    for b in range(bb):
        conv_scr[b, tail:tail + L, :] = mqk_ref[b]
        y = cb_ref[...]
        for j in range(CONV_K):
            y = y + conv_scr[b, pl.ds(tail - (CONV_K - 1) + j, L), :] * cw_ref[j:j + 1, :]
        conv_scr[b, 0:tail, :] = conv_scr[b, L:L + tail, :]
        qk = y * jax.nn.sigmoid(y)

        ig = gi_ref[b] + bi_ref[...]
        lf = _log_sigmoid(gf_ref[b] + bf_ref[...])
        bcum = jnp.dot(tril, lf, preferred_element_type=F32, precision=lax.Precision.HIGHEST)
        a_t = (ig - bcum).T

        for pair in range(MLSTM_HEADS // 2):
            q_pair = qk[:, pair * LANES:(pair + 1) * LANES] * (dk ** -0.5)
            k_pair = qk[:, nq + pair * LANES:nq + (pair + 1) * LANES]
            kt_pair = k_pair.T
            c_old = c_scr[b, pair]
            c_bf = c_old.astype(BF16)
            upd = jnp.zeros_like(c_old)
            decays = []
            for hh in range(2):
                h = 2 * pair + hh
                sel = (lane >= hh * dk) & (lane < (hh + 1) * dk)
                qm = jnp.where(sel, q_pair, 0.0).astype(BF16)
                km = jnp.where(sel, k_pair, 0.0).astype(BF16)
                selr = (sub >= hh * dk) & (sub < (hh + 1) * dk)
                ktm = jnp.where(selr, kt_pair, 0.0).astype(BF16)
                v_h = mv_ref[b, :, h * LANES:(h + 1) * LANES]
                v_aug = jnp.concatenate([v_h, ones_col], axis=1)

                m_st = m_scr[b, h, 0:1, 0:1]
                bc = bcum[:, h:h + 1]
                ic = ig[:, h:h + 1]
                a_m = jnp.where(causal, a_t[h:h + 1, :], -jnp.inf)
                inter = bc + m_st
                m_t = jnp.maximum(inter, bc + jnp.max(a_m, axis=-1, keepdims=True))
                qkt = lax.dot_general(qm, km, (((1,), (1,)), ((), ())), preferred_element_type=F32)
                w = qkt * jnp.exp(a_m + (bc - m_t))
                g = jnp.exp(inter - m_t)
                num_aug = (g * jnp.dot(qm, c_bf, preferred_element_type=F32)
                           + jnp.dot(w.astype(BF16), v_aug, preferred_element_type=F32))
                num = num_aug[:, :LANES]
                den = num_aug[:, LANES:LANES + 1]
                h_t = num / jnp.maximum(jnp.abs(den), jnp.exp(-m_t))

                b_last = bc[L - 1:L, :]
                gs = b_last - bc + ic
                m_new = jnp.maximum(b_last + m_st, jnp.max(gs, axis=0, keepdims=True))
                decays.append(jnp.exp(b_last + m_st - m_new))
                ws = jnp.exp(gs - m_new)
                wsv = (ws * v_aug.astype(F32)).astype(BF16)
                upd = upd + jnp.dot(ktm, wsv, preferred_element_type=F32)
                m_scr[b, h, 0:1, :] = jnp.broadcast_to(m_new, (1, LANES))

                hn = _rms(h_t, ng_ref[:, h * LANES:(h + 1) * LANES])
                gate = jax.nn.sigmoid(mo_ref[b, :, h * LANES:(h + 1) * LANES])
                o_ref[b, :, h * LANES:(h + 1) * LANES] = (hn * gate).astype(o_ref.dtype)
            d_rows = jnp.where(sub < dk, decays[0], decays[1])
            c_scr[b, pair] = d_rows * c_old + upd


def _mlstm(mqk, mv, gi, gf, mo, conv_w, conv_b, b_i, b_f, norm_g, bb):
    b, s, w = mqk.shape
    blk = lambda bi, c: (bi, c, 0)
    const = lambda bi, c: (0, 0)
    return pl.pallas_call(
        functools.partial(_mlstm_kernel, bb=bb),
        grid=(b // bb, s // CHUNK),
        in_specs=[pl.BlockSpec((bb, CHUNK, w), blk), pl.BlockSpec((bb, CHUNK, w), blk),
                  pl.BlockSpec((bb, CHUNK, LANES), blk), pl.BlockSpec((bb, CHUNK, LANES), blk),
                  pl.BlockSpec((bb, CHUNK, w), blk),
                  pl.BlockSpec(conv_w.shape, const), pl.BlockSpec(conv_b.shape, const),
                  pl.BlockSpec(b_i.shape, const), pl.BlockSpec(b_f.shape, const),
                  pl.BlockSpec(norm_g.shape, const)],
        out_specs=pl.BlockSpec((bb, CHUNK, w), blk),
        out_shape=jax.ShapeDtypeStruct((b, s, w), BF16),
        scratch_shapes=[pltpu.VMEM((bb, CHUNK + 8, w), F32),
                        pltpu.VMEM((bb, MLSTM_HEADS // 2, LANES, 2 * LANES), F32),
                        pltpu.VMEM((bb, MLSTM_HEADS, 8, LANES), F32)],
        compiler_params=_params(("parallel", "arbitrary")),
        name="mlstm",
    )(mqk, mv, gi, gf, mo, conv_w, conv_b, b_i, b_f, norm_g)


def _kvproj_kernel(mem_ref, g_ref, w_ref, k_ref, v_ref):
    d = mem_ref.shape[-1]
    mb = _rms(mem_ref[0], g_ref[...]).astype(BF16)
    k_ref[0] = jnp.dot(mb, w_ref[:, :d], preferred_element_type=F32).astype(k_ref.dtype)
    v_ref[0] = jnp.dot(mb, w_ref[:, d:], preferred_element_type=F32).astype(v_ref.dtype)


def _kv_proj(mem, g, w_ckv):
    b, m, d = mem.shape
    blk = lambda bi: (bi, 0, 0)
    const = lambda bi: (0, 0)
    return pl.pallas_call(
        _kvproj_kernel,
        grid=(b,),
        in_specs=[pl.BlockSpec((1, m, d), blk), pl.BlockSpec((1, d), const),
                  pl.BlockSpec(w_ckv.shape, const)],
        out_specs=[pl.BlockSpec((1, m, d), blk)] * 2,
        out_shape=[jax.ShapeDtypeStruct((b, m, d), BF16)] * 2,
        compiler_params=_params(("parallel",)),
        name="kv_proj",
    )(mem, g, w_ckv)


def _post_kernel(od_ref, hm_ref, x_ref, k_ref, v_ref, wo1_ref, wo2_ref, g2_ref, wcq_ref, wco_ref,
                 g3_ref, wr_ref, br_ref, h2_ref, xn3_ref, te_ref, tg_ref, cnt_ref, cstart_ref, base_scr):
    d = x_ref.shape[-1]
    hd = d // XATTN_HEADS
    h1 = (x_ref[0] + jnp.dot(od_ref[0], wo1_ref[...], preferred_element_type=F32)
          + jnp.dot(hm_ref[0], wo2_ref[...], preferred_element_type=F32))
    q = jnp.dot(_rms(h1, g2_ref[...]).astype(BF16), wcq_ref[...], preferred_element_type=F32)
    q = (q * (hd ** -0.5)).astype(BF16)
    heads = []
    for h in range(XATTN_HEADS):
        sl = slice(h * hd, (h + 1) * hd)
        s = lax.dot_general(q[:, sl], k_ref[0, :, sl], (((1,), (1,)), ((), ())),
                            preferred_element_type=F32)
        e = jnp.exp(s - jnp.max(s, axis=-1, keepdims=True))
        p = e / jnp.sum(e, axis=-1, keepdims=True)
        heads.append(jnp.dot(p.astype(BF16), v_ref[0, :, sl], preferred_element_type=F32))
    o = jnp.concatenate(heads, axis=1).astype(BF16)
    h2 = h1 + jnp.dot(o, wco_ref[...], preferred_element_type=F32)
    h2_ref[0] = h2
    xn3 = _rms(h2, g3_ref[...])
    _store_row_tiles(xn3_ref.at[0], xn3)
    x_hi = xn3.astype(BF16)
    x_lo = (xn3 - x_hi.astype(F32)).astype(BF16)
    hh_hl = jnp.dot(x_hi, wr_ref[...], preferred_element_type=F32)
    lh = jnp.dot(x_lo, wr_ref[:, :LANES], preferred_element_type=F32)
    logits = hh_hl[:, :LANES] + (hh_hl[:, LANES:] + lh) + br_ref[...]
    lane = lax.broadcasted_iota(I32, logits.shape, 1)
    lane_f = lane.astype(F32)
    cur = logits
    te = jnp.zeros(logits.shape, F32)
    chosen = jnp.zeros(logits.shape, F32)
    vals, hits = [], []
    for k in range(TOP_K):
        m = jnp.max(cur, axis=-1, keepdims=True)
        idx = jnp.min(jnp.where(cur == m, lane_f, float(LANES)), axis=-1, keepdims=True)
        hit = lane_f == idx
        vals.append(m)
        hits.append(hit)
        te = jnp.where(lane == k, idx, te)
        chosen = jnp.where(hit, 1.0, chosen)
        cur = jnp.where(hit, NEG_BIG * 2.0, cur)
    es = [jnp.exp(v - vals[0]) for v in vals]
    tot = es[0] + es[1] + es[2] + es[3]
    tg = jnp.zeros(logits.shape, F32)
    for k in range(TOP_K):
        tg = jnp.where(lane == k, es[k] / tot, tg)
    tg_ref[0] = tg

    @pl.when((pl.program_id(0) == 0) & (pl.program_id(1) == 0))
    def _():
        base_scr[...] = jnp.zeros_like(base_scr)
    tm = logits.shape[0]
    earlier = (lax.broadcasted_iota(I32, (tm, tm), 1) < lax.broadcasted_iota(I32, (tm, tm), 0)).astype(BF16)
    prior = jnp.dot(earlier, chosen.astype(BF16), preferred_element_type=F32) + base_scr[...]
    for k in range(TOP_K):
        rank = jnp.sum(jnp.where(hits[k], prior, 0.0), axis=-1, keepdims=True)
        te = jnp.where(lane == TOP_K + k, rank, te)
    te_ref[0] = te.astype(I32)
    cstart_ref[0] = jnp.broadcast_to(base_scr[...], cstart_ref.shape[1:])
    base_scr[...] = base_scr[...] + jnp.sum(chosen, axis=0, keepdims=True)
    cnt_ref[...] = base_scr[...]


def _post(od, hm, x, kmem, vmem, wo1, wo2, g2, wcq, wco, g3, wr, br, tm):
    b, s, d = x.shape
    w = od.shape[-1]
    m = kmem.shape[1]
    blk = lambda bi, i: (bi, i, 0)
    mem = lambda bi, i: (bi, 0, 0)
    const = lambda bi, i: (0, 0)
    full = lambda a: pl.BlockSpec(a.shape, const)
    return pl.pallas_call(
        _post_kernel,
        grid=(b, s // tm),
        in_specs=[pl.BlockSpec((1, tm, w), blk), pl.BlockSpec((1, tm, w), blk),
                  pl.BlockSpec((1, tm, d), blk), pl.BlockSpec((1, m, d), mem),
                  pl.BlockSpec((1, m, d), mem), full(wo1), full(wo2), full(g2), full(wcq),
                  full(wco), full(g3), full(wr), full(br)],
        out_specs=[pl.BlockSpec((1, tm, d), blk),
                   pl.BlockSpec((1, tm * (d // LANES), LANES), blk),
                   pl.BlockSpec((1, tm, LANES), blk), pl.BlockSpec((1, tm, LANES), blk),
                   pl.BlockSpec((1, LANES), const),
                   pl.BlockSpec((1, 8, LANES), lambda bi, i: (bi * (s // tm) + i, 0, 0))],
        out_shape=[jax.ShapeDtypeStruct((b, s, d), F32), jax.ShapeDtypeStruct((b, s * (d // LANES), LANES), F32),
                   jax.ShapeDtypeStruct((b, s, LANES), I32), jax.ShapeDtypeStruct((b, s, LANES), F32),
                   jax.ShapeDtypeStruct((1, LANES), F32),
                   jax.ShapeDtypeStruct((b * (s // tm), 8, LANES), F32)],
        scratch_shapes=[pltpu.VMEM((1, LANES), F32)],
        compiler_params=_params(("arbitrary", "arbitrary")),
        name="post_mixer",
    )(od, hm, x, kmem, vmem, wo1, wo2, g2, wcq, wco, g3, wr, br)


def _split_kernel(w_ref, p_ref, wg_ref, wl_ref):
    wb = w_ref[0].astype(BF16)
    two = 2 * LANES
    for g in range(wb.shape[1] // two):
        r = jnp.dot(wb[:, g * two:(g + 1) * two], p_ref[...], preferred_element_type=F32)
        wg_ref[0, :, g * LANES:(g + 1) * LANES] = r[:, :LANES].astype(BF16)
        wl_ref[0, :, g * LANES:(g + 1) * LANES] = r[:, LANES:].astype(BF16)


def _split_w1(w1, rb):
    e, d, de2 = w1.shape
    two = 2 * LANES
    src = jnp.arange(two)[:, None]
    dst = jnp.arange(two)[None, :]
    perm = jnp.where(dst < LANES, src == 2 * dst, src == 2 * (dst - LANES) + 1).astype(BF16)
    blk = lambda ei, r: (ei, r, 0)
    return pl.pallas_call(
        _split_kernel,
        grid=(e, d // rb),
        in_specs=[pl.BlockSpec((1, rb, de2), blk), pl.BlockSpec((two, two), lambda ei, r: (0, 0))],
        out_specs=[pl.BlockSpec((1, rb, de2 // 2), blk)] * 2,
        out_shape=[jax.ShapeDtypeStruct((e, d, de2 // 2), BF16)] * 2,
        compiler_params=_params(("parallel", "parallel")),
        name="split_w1",
    )(w1, perm)


def _dispatch_kernel(nz_ref, dest_ref, x_ref, xs_hbm, zbuf, sem, zsem, *, tb, bm, n_blocks, sub):
    i = pl.program_id(0)
    tile = lambda r: pl.ds(pl.multiple_of(r * sub, sub), sub)

    @pl.when(i == 0)
    def _():
        zbuf[...] = jnp.zeros_like(zbuf)

        def zstart(blk, carry):
            @pl.when(nz_ref[blk] != 0)
            def _():
                pltpu.make_async_copy(zbuf, xs_hbm.at[pl.ds(pl.multiple_of(blk * bm * sub, bm * sub), bm * sub)],
                                      zsem).start()
            return carry
        lax.fori_loop(0, n_blocks, zstart, 0)

        def zwait(blk, carry):
            @pl.when(nz_ref[blk] != 0)
            def _():
                pltpu.make_async_copy(zbuf, xs_hbm.at[pl.ds(0, bm * sub)], zsem).wait()
            return carry
        lax.fori_loop(0, n_blocks, zwait, 0)

    def issue(t, carry):
        for k in range(TOP_K):
            dst = dest_ref[0, 0, t * TOP_K + k]
            pltpu.make_async_copy(x_ref.at[tile(t)], xs_hbm.at[tile(dst)], sem).start(priority=k % 2)
        return carry
    lax.fori_loop(0, tb, issue, 0)
    for k in range(TOP_K):
        pltpu.make_async_copy(x_ref, xs_hbm.at[pl.ds(0, tb * sub)], sem).wait()


def _dispatch(needs_zero, dest_blocks, xn3, sub, bm, tb):
    t = xn3.shape[0] // sub
    n_blocks = needs_zero.shape[0]
    grid_spec = pltpu.PrefetchScalarGridSpec(
        num_scalar_prefetch=1,
        grid=(t // tb,),
        in_specs=[pl.BlockSpec((1, 1, tb * TOP_K), lambda i, nz: (i, 0, 0), memory_space=pltpu.SMEM),
                  pl.BlockSpec((tb * sub, LANES), lambda i, nz: (i, 0))],
        out_specs=pl.BlockSpec(memory_space=pl.ANY),
        scratch_shapes=[pltpu.VMEM((bm * sub, LANES), F32), pltpu.SemaphoreType.DMA, pltpu.SemaphoreType.DMA],
    )
    return pl.pallas_call(
        functools.partial(_dispatch_kernel, tb=tb, bm=bm, n_blocks=n_blocks, sub=sub),
        grid_spec=grid_spec,
        out_shape=jax.ShapeDtypeStruct((n_blocks * bm * sub, LANES), F32),
        compiler_params=_params(("arbitrary",)),
        name="dispatch",
    )(needs_zero, dest_blocks, xn3)


def _expert_kernel(be_ref, na_ref, x_ref, w1g_ref, w1l_ref, b1g_ref, b1l_ref, w2_ref, b2_ref, y_ref):
    i = pl.program_id(0)

    @pl.when(i < na_ref[0])
    def _():
        xb = _load_row_tiles(x_ref, w1g_ref.shape[1] // LANES).astype(BF16)
        glu = jnp.dot(xb, w1g_ref[0], preferred_element_type=F32) + b1g_ref[0]
        lin = jnp.dot(xb, w1l_ref[0], preferred_element_type=F32) + b1l_ref[0]
        glu = jnp.minimum(glu, SWIGLU_LIMIT)
        lin = jnp.clip(lin, -SWIGLU_LIMIT, SWIGLU_LIMIT)
        act = glu * jax.nn.sigmoid(SWIGLU_ALPHA * glu) * (lin + 1.0)
        _store_row_tiles(y_ref, jnp.dot(act.astype(BF16), w2_ref[0], preferred_element_type=F32) + b2_ref[0])

    @pl.when(i >= na_ref[0])
    def _():
        y_ref[...] = jnp.zeros_like(y_ref)


def _experts(blk_expert, n_active, xs, w1g, w1l, b1g, b1l, w2, b2, bm):
    d = w1g.shape[1]
    sub = d // LANES
    n_pad = xs.shape[0] // sub
    n_blocks = n_pad // bm
    de = w1g.shape[-1]
    last = lambda i, na: jnp.minimum(i, na[0] - 1)
    wmap = lambda i, be, na: (be[last(i, na)], 0, 0)
    grid_spec = pltpu.PrefetchScalarGridSpec(
        num_scalar_prefetch=2,
        grid=(n_blocks,),
        in_specs=[pl.BlockSpec((bm * sub, LANES), lambda i, be, na: (last(i, na), 0)),
                  pl.BlockSpec((1, d, de), wmap), pl.BlockSpec((1, d, de), wmap),
                  pl.BlockSpec((1, 1, de), wmap), pl.BlockSpec((1, 1, de), wmap),
                  pl.BlockSpec((1, de, d), wmap), pl.BlockSpec((1, 1, d), wmap)],
        out_specs=pl.BlockSpec((bm * sub, LANES), lambda i, be, na: (i, 0)),
    )
    return pl.pallas_call(
        _expert_kernel,
        grid_spec=grid_spec,
        out_shape=jax.ShapeDtypeStruct((n_pad * sub, LANES), F32),
        compiler_params=_params(("arbitrary",)),
        name="experts",
    )(blk_expert, n_active, xs, w1g, w1l, b1g, b1l, w2, b2)


COMBINE_CHUNK = 8


def _combine_plan(dest, top_e, cstart, counts, pstart, tb, sub):
    ch = COMBINE_CHUNK
    nb, n_experts = cstart.shape
    t = dest.shape[0]
    cnt = jnp.concatenate([cstart[1:], counts[None, :]], axis=0) - cstart
    run_lo = pstart[None, :] + cstart
    lo = run_lo // ch * ch
    nch = jnp.where(cnt > 0, (run_lo + cnt - lo + ch - 1) // ch, 0)
    cum = jnp.cumsum(nch, axis=1)
    first = cum - nch
    max_ch = tb * TOP_K // ch + 2 * n_experts
    c = jnp.arange(max_ch, dtype=I32)
    e_of_c = jnp.minimum(jnp.sum(c[None, :, None] >= cum[:, None, :], axis=-1), n_experts - 1)
    take = lambda tab: jnp.take_along_axis(tab, e_of_c, axis=1)
    src = jnp.where(c[None, :] < cum[:, -1:], take(lo) + (c[None, :] - take(first)) * ch, 0).astype(I32)
    blk = (jnp.arange(t, dtype=I32) // tb)[:, None]
    loc = (first[blk, top_e] * ch + dest - lo[blk, top_e]).astype(I32)
    return (src * sub).reshape(nb, 1, max_ch), (loc * sub).reshape(nb, 1, tb * TOP_K), max_ch


def _combine_kernel(src_ref, snext_ref, loc_ref, y_hbm, h2_ref, tg_ref, g_ref, o_ref,
                    buf, rows_scr, sem, *, tb, final_norm):
    i = pl.program_id(0)
    sub = h2_ref.shape[1] // LANES
    n_chunks = src_ref.shape[-1]
    rows = COMBINE_CHUNK * sub
    slot = i % 2

    def fetch(s_ref, s):
        def issue(c, carry):
            lo = pl.multiple_of(s_ref[0, 0, c], sub)
            pltpu.make_async_copy(y_hbm.at[pl.ds(lo, rows)], buf.at[s, pl.ds(pl.multiple_of(c * rows, rows), rows)],
                                  sem.at[s]).start(priority=1)
            return carry
        lax.fori_loop(0, n_chunks, issue, 0, unroll=4)

    @pl.when(i == 0)
    def _():
        fetch(src_ref, 0)

    @pl.when(i + 1 < pl.num_programs(0))
    def _():
        fetch(snext_ref, 1 - slot)

    pltpu.make_async_copy(y_hbm.at[pl.ds(0, n_chunks * rows)], buf.at[slot], sem.at[slot]).wait()

    def token(t, carry):
        dst = pl.ds(pl.multiple_of(t * sub, sub), sub)
        for k in range(TOP_K):
            r = pl.multiple_of(loc_ref[0, 0, t * TOP_K + k], sub)
            rows_scr[k, dst, :] = buf[slot, pl.ds(r, sub), :]
        return carry
    lax.fori_loop(0, tb, token, 0, unroll=4)

    acc = h2_ref[...]
    for k in range(TOP_K):
        acc = acc + _load_row_tiles(rows_scr.at[k], sub) * tg_ref[:, k:k + 1]
    o_ref[...] = _rms(acc, g_ref[...]) if final_norm else acc


def _combine(src, loc, max_ch, y, h2, tg, g, tb, final_norm):
    t, d = h2.shape
    sub = d // LANES
    nb = t // tb
    row = lambda i: (i, 0)
    smem = lambda n, shift: pl.BlockSpec((1, 1, n), lambda i: (jnp.minimum(i + shift, nb - 1), 0, 0),
                                         memory_space=pltpu.SMEM)
    return pl.pallas_call(
        functools.partial(_combine_kernel, tb=tb, final_norm=final_norm),
        grid=(nb,),
        in_specs=[smem(max_ch, 0), smem(max_ch, 1), smem(tb * TOP_K, 0),
                  pl.BlockSpec(memory_space=pl.ANY),
                  pl.BlockSpec((tb, d), row), pl.BlockSpec((tb, LANES), row),
                  pl.BlockSpec((1, d), lambda i: (0, 0))],
        out_specs=pl.BlockSpec((tb, d), row),
        out_shape=jax.ShapeDtypeStruct((t, d), F32),
        scratch_shapes=[pltpu.VMEM((2, max_ch * COMBINE_CHUNK * sub, LANES), F32),
                        pltpu.VMEM((TOP_K, tb * sub, LANES), F32), pltpu.SemaphoreType.DMA((2,))],
        compiler_params=_params(("arbitrary",)),
        name="combine",
    )(src, src, loc, y, h2, tg, g)


def _route(top_e, rank, counts, n_experts, bm):
    a = top_e.size
    padded = (counts + bm - 1) // bm * bm
    pend = jnp.cumsum(padded)
    pstart = pend - padded
    dest = (jnp.take(pstart, top_e) + rank).astype(I32)
    n_blocks = -(-a // bm) + n_experts
    blk_lo = jnp.arange(n_blocks, dtype=I32) * bm
    blk_expert = jnp.minimum(jnp.sum(blk_lo[:, None] >= pend[None, :], axis=1), n_experts - 1).astype(I32)
    n_active = (pend[-1] // bm).astype(I32)
    has_pad = jnp.any((blk_lo[:, None] + bm) == pend[None, :], axis=1)
    needs_zero = (has_pad | (jnp.arange(n_blocks) >= n_active)).astype(I32)
    return dest, blk_expert, n_active.reshape(1), needs_zero, pstart.astype(I32)


def _pad_cols(w, n):
    return jnp.pad(w, ((0, 0), (0, n - w.shape[1])))


def _layer(h, mem, l, p):
    b, s, d = h.shape
    t = b * s
    lambda_init = 0.8 - 0.6 * math.exp(-0.3 * l)
    n_main = 6 * GROUP
    w_in = p['w_in']
    w_main = w_in[:, :n_main].astype(BF16)
    w_i = _pad_cols(w_in[:, n_main:n_main + MLSTM_HEADS], LANES).astype(BF16)
    w_f = _pad_cols(w_in[:, n_main + MLSTM_HEADS:], LANES).astype(BF16)
    dq, dk, dv, mqk, mv, mo, gi, gf = _in_proj(h.reshape(t, d), p['norm_mix_g'].reshape(1, d),
                                               w_main, w_i, w_f, tm=min(512, t))
    r3 = lambda a: a.reshape(b, s, a.shape[-1])
    vec = lambda a: a.reshape(1, -1)
    od = _diff_attention(r3(dq), r3(dk), r3(dv), vec(p['lambda_q1']), vec(p['lambda_k1']),
                         vec(p['lambda_q2']), vec(p['lambda_k2']), vec(p['diff_norm_g']),
                         lambda_init, bq=min(512, s))
    hm = _mlstm(r3(mqk), r3(mv), r3(gi), r3(gf), r3(mo), p['conv_w'], vec(p['conv_b']),
                _pad_cols(vec(p['b_igate']), LANES), _pad_cols(vec(p['b_fgate']), LANES),
                vec(p['mlstm_norm_g']), bb=2 if b % 2 == 0 else 1)
    kmem, vmem = _kv_proj(mem, vec(p['norm_mem_g']), p['w_ckv'].astype(BF16))
    w_out = p['w_out'].astype(BF16)
    n_experts = p['w_router'].shape[1]
    wr = _pad_cols(p['w_router'], LANES)
    wr_hi = wr.astype(BF16)
    wr = jnp.concatenate([wr_hi, (wr - wr_hi.astype(F32)).astype(BF16)], axis=1)
    br =jnp.concatenate([vec(p['b_router']), jnp.full((1, LANES - n_experts), NEG_BIG, F32)], axis=1)
    tm = min(512, s)
    h2, xn3, te, tg, cnt, cstart = _post(od, hm, h, kmem, vmem, w_out[:GROUP], w_out[GROUP:],
                                         vec(p['norm_xattn_g']), p['w_cq'].astype(BF16), p['w_co'].astype(BF16),
                                         vec(p['norm_ffn_g']), wr, br, tm=tm)
    te = te.reshape(t, LANES)
    counts = cnt[0, :n_experts].astype(I32)
    dest, blk_expert, n_active, needs_zero, pstart = _route(te[:, :TOP_K], te[:, TOP_K:2 * TOP_K],
                                                            counts, n_experts, MOE_BLOCK)
    tb_d = min(512, t)
    xs = _dispatch(needs_zero, dest.reshape(t // tb_d, 1, tb_d * TOP_K), xn3.reshape(t * (d // LANES), LANES),
                   d // LANES, MOE_BLOCK, tb_d)
    w1g, w1l = _split_w1(p['w1'], rb=256)
    y = _experts(blk_expert, n_active, xs, w1g, w1l,
                 p['b1'][:, None, 0::2], p['b1'][:, None, 1::2],
                 p['w2'].astype(BF16), p['b2'][:, None, :], MOE_BLOCK)
    src, loc, max_ch = _combine_plan(dest, te[:, :TOP_K], cstart[:, 0, :n_experts].astype(I32),
                                     counts, pstart, tm, d // LANES)
    return functools.partial(_combine, src, loc, max_ch, y, h2.reshape(t, d), tg.reshape(t, LANES), tb=tm)


def kernel(x, mem, norm_mix_g, w_in, conv_w, conv_b, b_igate, b_fgate, mlstm_norm_g, lambda_q1, lambda_k1, lambda_q2, lambda_k2, diff_norm_g, w_out, norm_xattn_g, norm_mem_g, w_cq, w_ckv, w_co, norm_ffn_g, w_router, b_router, w1, b1, w2, b2, norm_final_g):
    stacked = dict(norm_mix_g=norm_mix_g, w_in=w_in, conv_w=conv_w, conv_b=conv_b, b_igate=b_igate,
                   b_fgate=b_fgate, mlstm_norm_g=mlstm_norm_g, lambda_q1=lambda_q1, lambda_k1=lambda_k1,
                   lambda_q2=lambda_q2, lambda_k2=lambda_k2, diff_norm_g=diff_norm_g, w_out=w_out,
                   norm_xattn_g=norm_xattn_g, norm_mem_g=norm_mem_g, w_cq=w_cq, w_ckv=w_ckv, w_co=w_co,
                   norm_ffn_g=norm_ffn_g, w_router=w_router, b_router=b_router, w1=w1, b1=b1, w2=w2, b2=b2)
    depth = w_in.shape[0]
    b, s, d = x.shape
    h = x
    for l in range(depth):
        p = {k: v[l] for k, v in stacked.items()}
        combine = _layer(h, mem, l, p)
        h = combine(norm_final_g.reshape(1, d), final_norm=l == depth - 1).reshape(b, s, d)
    return h
```

```python
import functools
import math

import jax
import jax.numpy as jnp
from jax import lax
from jax.experimental import pallas as pl
from jax.experimental.pallas import tpu as pltpu

F32 = jnp.float32
BF16 = jnp.bfloat16
I32 = jnp.int32

RMS_EPS = 1e-5
LANES = 128
VMEM_LIMIT = 56 * 1024 * 1024

DIFF_HEADS = 4
DIFF_QK_DIM = 64
MLSTM_HEADS = 4
MLSTM_QK_DIM = 64
CONV_K = 4
CHUNK = 128
XATTN_HEADS = 4
TOP_K = 4
SWIGLU_ALPHA = 1.702
SWIGLU_LIMIT = 7.0
MOE_BLOCK = 512
GROUP = 512
NEG_BIG = -1e30


def _rms(x, g):
    return x * lax.rsqrt(jnp.mean(x * x, axis=-1, keepdims=True) + RMS_EPS) * g


def _store_row_tiles(ref, m):
    rows, d = m.shape
    sub = d // LANES
    for c in range(sub):
        ref[pl.ds(c, rows, stride=sub), :] = m[:, c * LANES:(c + 1) * LANES]


def _load_row_tiles(ref, sub):
    rows = ref.shape[0] // sub
    return jnp.concatenate([ref[pl.ds(c, rows, stride=sub), :] for c in range(sub)], axis=1)


def _params(sem, vmem=VMEM_LIMIT, flags=None):
    return pltpu.CompilerParams(dimension_semantics=sem, vmem_limit_bytes=vmem, flags=flags)


def _inproj_kernel(x_ref, g_ref, w_ref, wi_ref, wf_ref,
                   dq_ref, dk_ref, dv_ref, mqk_ref, mv_ref, mo_ref, gi_ref, gf_ref):
    xb = _rms(x_ref[...], g_ref[...]).astype(BF16)
    for n, o_ref in enumerate((dq_ref, dk_ref, dv_ref, mqk_ref, mv_ref, mo_ref)):
        o_ref[...] = jnp.dot(xb, w_ref[:, n * GROUP:(n + 1) * GROUP],
                             preferred_element_type=F32).astype(o_ref.dtype)
    gi_ref[...] = jnp.dot(xb, wi_ref[...], preferred_element_type=F32)
    gf_ref[...] = jnp.dot(xb, wf_ref[...], preferred_element_type=F32)


def _in_proj(x2, g, w_main, w_i, w_f, tm):
    t, d = x2.shape
    row = lambda i: (i, 0)
    const = lambda i: (0, 0)
    out_dtypes = (BF16, BF16, BF16, F32, BF16, F32)
    return pl.pallas_call(
        _inproj_kernel,
        grid=(t // tm,),
        in_specs=[pl.BlockSpec((tm, d), row), pl.BlockSpec((1, d), const),
                  pl.BlockSpec(w_main.shape, const), pl.BlockSpec(w_i.shape, const),
                  pl.BlockSpec(w_f.shape, const)],
        out_specs=[pl.BlockSpec((tm, GROUP), row)] * 6 + [pl.BlockSpec((tm, LANES), row)] * 2,
        out_shape=[jax.ShapeDtypeStruct((t, GROUP), dt) for dt in out_dtypes]
        + [jax.ShapeDtypeStruct((t, LANES), F32)] * 2,
        compiler_params=_params(("parallel",)),
        name="in_proj",
    )(x2, g, w_main, w_i, w_f)


def _diffattn_kernel(q_ref, k_ref, v_ref, lq1_ref, lk1_ref, lq2_ref, lk2_ref, g_ref, o_ref,
                     *, bq, lambda_init):
    s_len = q_ref.shape[1]
    lane = lax.broadcasted_iota(I32, (1, LANES), 1)
    lo = lane < DIFF_QK_DIM
    row = lax.broadcasted_iota(I32, (bq, bq), 0)
    col = lax.broadcasted_iota(I32, (bq, bq), 1)
    causal = col <= row
    lam = (jnp.exp(jnp.sum(lq1_ref[...] * lk1_ref[...], axis=-1, keepdims=True))
           - jnp.exp(jnp.sum(lq2_ref[...] * lk2_ref[...], axis=-1, keepdims=True)) + lambda_init)

    def step(qs, j, carry, masked):
        kj = k_ref[0, j * bq:(j + 1) * bq, :]
        vj = v_ref[0, j * bq:(j + 1) * bq, :]
        new = []
        for qm, (m, l, a) in zip(qs, carry):
            s = lax.dot_general(qm, kj, (((1,), (1,)), ((), ())), preferred_element_type=F32)
            if masked:
                s = jnp.where(causal, s, -jnp.inf)
            mn = jnp.maximum(m, jnp.max(s, axis=-1, keepdims=True))
            p = jnp.exp(s - mn)
            alpha = jnp.exp(m - mn)
            l = alpha * l + jnp.sum(p, axis=-1, keepdims=True)
            a = alpha * a + jnp.dot(p.astype(BF16), vj, preferred_element_type=F32)
            new.append((mn, l, a))
        return tuple(new)

    for i in range(s_len // bq):
        q = q_ref[0, i * bq:(i + 1) * bq, :] * jnp.asarray(DIFF_QK_DIM ** -0.5, BF16)
        zero = jnp.zeros_like(q)
        qs = (jnp.where(lo, q, zero), jnp.where(lo, zero, q))
        init_one = (jnp.full((bq, 1), -jnp.inf, F32), jnp.zeros((bq, 1), F32), jnp.zeros((bq, LANES), F32))
        carry = (init_one, init_one)
        for j in range(i):
            carry = step(qs, j, carry, False)
        (_, l1, a1), (_, l2, a2) = step(qs, i, carry, True)
        od = a1 / l1 - lam * (a2 / l2)
        o_ref[0, i * bq:(i + 1) * bq, :] = (_rms(od, g_ref[...]) * (1.0 - lambda_init)).astype(o_ref.dtype)


def _diff_attention(dq, dk, dv, lq1, lk1, lq2, lk2, g, lambda_init, bq):
    b, s, w = dq.shape
    blk = lambda bi, h: (bi, 0, h)
    vec = lambda bi, h: (0, 0)
    return pl.pallas_call(
        functools.partial(_diffattn_kernel, bq=bq, lambda_init=lambda_init),
        grid=(b, DIFF_HEADS),
        in_specs=[pl.BlockSpec((1, s, LANES), blk)] * 3
        + [pl.BlockSpec((1, DIFF_QK_DIM), vec)] * 4
        + [pl.BlockSpec((1, LANES), lambda bi, h: (0, h))],
        out_specs=pl.BlockSpec((1, s, LANES), blk),
        out_shape=jax.ShapeDtypeStruct((b, s, w), BF16),
        compiler_params=_params(("parallel", "parallel")),
        name="diff_attn",
    )(dq, dk, dv, lq1, lk1, lq2, lk2, g)


def _log_sigmoid(x):
    return -(jnp.maximum(-x, 0.0) + jnp.log1p(jnp.exp(-jnp.abs(x))))


def _mlstm_kernel(mqk_ref, mv_ref, gi_ref, gf_ref, mo_ref, cw_ref, cb_ref, bi_ref, bf_ref, ng_ref,
                  o_ref, conv_scr, c_scr, m_scr, *, bb):
    c = pl.program_id(1)
    L = CHUNK
    dk = MLSTM_QK_DIM
    nq = MLSTM_HEADS * dk
    tail = 8

    @pl.when(c == 0)
    def _():
        conv_scr[:, 0:tail, :] = jnp.zeros((bb, tail, 2 * nq), F32)
        c_scr[...] = jnp.zeros_like(c_scr)
        m_scr[...] = jnp.zeros_like(m_scr)

    row = lax.broadcasted_iota(I32, (L, L), 0)
    col = lax.broadcasted_iota(I32, (L, L), 1)
    causal = col <= row
    tril = causal.astype(F32)
    lane = lax.broadcasted_iota(I32, (1, LANES), 1)
    sub = lax.broadcasted_iota(I32, (LANES, 1), 0)
    ones_col = jnp.where(lane == 0, 1.0, 0.0).astype(BF16) * jnp.ones((L, 1), BF16)

    for b in range(bb):
        conv_scr[b, tail:tail + L, :] = mqk_ref[b]
        y = cb_ref[...]
        for j in range(CONV_K):
            y = y + conv_scr[b, pl.ds(tail - (CONV_K - 1) + j, L), :] * cw_ref[j:j + 1, :]
        conv_scr[b, 0:tail, :] = conv_scr[b, L:L + tail, :]
        qk = y * jax.nn.sigmoid(y)

        ig = gi_ref[b] + bi_ref[...]
        lf = _log_sigmoid(gf_ref[b] + bf_ref[...])
        bcum = jnp.dot(tril, lf, preferred_element_type=F32, precision=lax.Precision.HIGHEST)
        a_t = (ig - bcum).T

        for pair in range(MLSTM_HEADS // 2):
            q_pair = qk[:, pair * LANES:(pair + 1) * LANES] * (dk ** -0.5)
            k_pair = qk[:, nq + pair * LANES:nq + (pair + 1) * LANES]
            kt_pair = k_pair.T
            c_old = c_scr[b, pair]
            c_bf = c_old.astype(BF16)
            upd = jnp.zeros_like(c_old)
            decays = []
            for hh in range(2):
                h = 2 * pair + hh
                sel = (lane >= hh * dk) & (lane < (hh + 1) * dk)
                qm = jnp.where(sel, q_pair, 0.0).astype(BF16)
                km = jnp.where(sel, k_pair, 0.0).astype(BF16)
                selr = (sub >= hh * dk) & (sub < (hh + 1) * dk)
                ktm = jnp.where(selr, kt_pair, 0.0).astype(BF16)
                v_h = mv_ref[b, :, h * LANES:(h + 1) * LANES]
                v_aug = jnp.concatenate([v_h, ones_col], axis=1)

                m_st = m_scr[b, h, 0:1, 0:1]
                bc = bcum[:, h:h + 1]
                ic = ig[:, h:h + 1]
                a_m = jnp.where(causal, a_t[h:h + 1, :], -jnp.inf)
                inter = bc + m_st
                m_t = jnp.maximum(inter, bc + jnp.max(a_m, axis=-1, keepdims=True))
                qkt = lax.dot_general(qm, km, (((1,), (1,)), ((), ())), preferred_element_type=F32)
                w = qkt * jnp.exp(a_m + (bc - m_t))
                g = jnp.exp(inter - m_t)
                num_aug = (g * jnp.dot(qm, c_bf, preferred_element_type=F32)
                           + jnp.dot(w.astype(BF16), v_aug, preferred_element_type=F32))
                num = num_aug[:, :LANES]
                den = num_aug[:, LANES:LANES + 1]
                h_t = num / jnp.maximum(jnp.abs(den), jnp.exp(-m_t))

                b_last = bc[L - 1:L, :]
                gs = b_last - bc + ic
                m_new = jnp.maximum(b_last + m_st, jnp.max(gs, axis=0, keepdims=True))
                decays.append(jnp.exp(b_last + m_st - m_new))
                ws = jnp.exp(gs - m_new)
                wsv = (ws * v_aug.astype(F32)).astype(BF16)
                upd = upd + jnp.dot(ktm, wsv, preferred_element_type=F32)
                m_scr[b, h, 0:1, :] = jnp.broadcast_to(m_new, (1, LANES))

                hn = _rms(h_t, ng_ref[:, h * LANES:(h + 1) * LANES])
                gate = jax.nn.sigmoid(mo_ref[b, :, h * LANES:(h + 1) * LANES])
                o_ref[b, :, h * LANES:(h + 1) * LANES] = (hn * gate).astype(o_ref.dtype)
            d_rows = jnp.where(sub < dk, decays[0], decays[1])
            c_scr[b, pair] = d_rows * c_old + upd


def _mlstm(mqk, mv, gi, gf, mo, conv_w, conv_b, b_i, b_f, norm_g, bb):
    b, s, w = mqk.shape
    blk = lambda bi, c: (bi, c, 0)
    const = lambda bi, c: (0, 0)
    return pl.pallas_call(
        functools.partial(_mlstm_kernel, bb=bb),
        grid=(b // bb, s // CHUNK),
        in_specs=[pl.BlockSpec((bb, CHUNK, w), blk), pl.BlockSpec((bb, CHUNK, w), blk),
                  pl.BlockSpec((bb, CHUNK, LANES), blk), pl.BlockSpec((bb, CHUNK, LANES), blk),
                  pl.BlockSpec((bb, CHUNK, w), blk),
                  pl.BlockSpec(conv_w.shape, const), pl.BlockSpec(conv_b.shape, const),
                  pl.BlockSpec(b_i.shape, const), pl.BlockSpec(b_f.shape, const),
                  pl.BlockSpec(norm_g.shape, const)],
        out_specs=pl.BlockSpec((bb, CHUNK, w), blk),
        out_shape=jax.ShapeDtypeStruct((b, s, w), BF16),
        scratch_shapes=[pltpu.VMEM((bb, CHUNK + 8, w), F32),
                        pltpu.VMEM((bb, MLSTM_HEADS // 2, LANES, 2 * LANES), F32),
                        pltpu.VMEM((bb, MLSTM_HEADS, 8, LANES), F32)],
        compiler_params=_params(("parallel", "arbitrary")),
        name="mlstm",
    )(mqk, mv, gi, gf, mo, conv_w, conv_b, b_i, b_f, norm_g)


def _kvproj_kernel(mem_ref, g_ref, w_ref, k_ref, v_ref):
    d = mem_ref.shape[-1]
    mb = _rms(mem_ref[0], g_ref[...]).astype(BF16)
    k_ref[0] = jnp.dot(mb, w_ref[:, :d], preferred_element_type=F32).astype(k_ref.dtype)
    v_ref[0] = jnp.dot(mb, w_ref[:, d:], preferred_element_type=F32).astype(v_ref.dtype)


def _kv_proj(mem, g, w_ckv):
    b, m, d = mem.shape
    blk = lambda bi: (bi, 0, 0)
    const = lambda bi: (0, 0)
    return pl.pallas_call(
        _kvproj_kernel,
        grid=(b,),
        in_specs=[pl.BlockSpec((1, m, d), blk), pl.BlockSpec((1, d), const),
                  pl.BlockSpec(w_ckv.shape, const)],
        out_specs=[pl.BlockSpec((1, m, d), blk)] * 2,
        out_shape=[jax.ShapeDtypeStruct((b, m, d), BF16)] * 2,
        compiler_params=_params(("parallel",)),
        name="kv_proj",
    )(mem, g, w_ckv)


def _post_kernel(od_ref, hm_ref, x_ref, k_ref, v_ref, wo1_ref, wo2_ref, g2_ref, wcq_ref, wco_ref,
                 g3_ref, wr_ref, br_ref, h2_ref, xn3_ref, te_ref, tg_ref, cnt_ref, cstart_ref, base_scr):
    d = x_ref.shape[-1]
    hd = d // XATTN_HEADS
    h1 = (x_ref[0] + jnp.dot(od_ref[0], wo1_ref[...], preferred_element_type=F32)
          + jnp.dot(hm_ref[0], wo2_ref[...], preferred_element_type=F32))
    q = jnp.dot(_rms(h1, g2_ref[...]).astype(BF16), wcq_ref[...], preferred_element_type=F32)
    q = (q * (hd ** -0.5)).astype(BF16)
    heads = []
    for h in range(XATTN_HEADS):
        sl = slice(h * hd, (h + 1) * hd)
        s = lax.dot_general(q[:, sl], k_ref[0, :, sl], (((1,), (1,)), ((), ())),
                            preferred_element_type=F32)
        e = jnp.exp(s - jnp.max(s, axis=-1, keepdims=True))
        p = e / jnp.sum(e, axis=-1, keepdims=True)
        heads.append(jnp.dot(p.astype(BF16), v_ref[0, :, sl], preferred_element_type=F32))
    o = jnp.concatenate(heads, axis=1).astype(BF16)
    h2 = h1 + jnp.dot(o, wco_ref[...], preferred_element_type=F32)
    h2_ref[0] = h2
    xn3 = _rms(h2, g3_ref[...])
    _store_row_tiles(xn3_ref.at[0], xn3)
    x_hi = xn3.astype(BF16)
    x_lo = (xn3 - x_hi.astype(F32)).astype(BF16)
    hh_hl = jnp.dot(x_hi, wr_ref[...], preferred_element_type=F32)
    lh = jnp.dot(x_lo, wr_ref[:, :LANES], preferred_element_type=F32)
    logits = hh_hl[:, :LANES] + (hh_hl[:, LANES:] + lh) + br_ref[...]
    lane = lax.broadcasted_iota(I32, logits.shape, 1)
    lane_f = lane.astype(F32)
    cur = logits
    te = jnp.zeros(logits.shape, F32)
    chosen = jnp.zeros(logits.shape, F32)
    vals, hits = [], []
    for k in range(TOP_K):
        m = jnp.max(cur, axis=-1, keepdims=True)
        idx = jnp.min(jnp.where(cur == m, lane_f, float(LANES)), axis=-1, keepdims=True)
        hit = lane_f == idx
        vals.append(m)
        hits.append(hit)
        te = jnp.where(lane == k, idx, te)
        chosen = jnp.where(hit, 1.0, chosen)
        cur = jnp.where(hit, NEG_BIG * 2.0, cur)
    es = [jnp.exp(v - vals[0]) for v in vals]
    tot = es[0] + es[1] + es[2] + es[3]
    tg = jnp.zeros(logits.shape, F32)
    for k in range(TOP_K):
        tg = jnp.where(lane == k, es[k] / tot, tg)
    tg_ref[0] = tg

    @pl.when((pl.program_id(0) == 0) & (pl.program_id(1) == 0))
    def _():
        base_scr[...] = jnp.zeros_like(base_scr)
    tm = logits.shape[0]
    earlier = (lax.broadcasted_iota(I32, (tm, tm), 1) < lax.broadcasted_iota(I32, (tm, tm), 0)).astype(BF16)
    prior = jnp.dot(earlier, chosen.astype(BF16), preferred_element_type=F32) + base_scr[...]
    for k in range(TOP_K):
        rank = jnp.sum(jnp.where(hits[k], prior, 0.0), axis=-1, keepdims=True)
        te = jnp.where(lane == TOP_K + k, rank, te)
    te_ref[0] = te.astype(I32)
    cstart_ref[0] = jnp.broadcast_to(base_scr[...], cstart_ref.shape[1:])
    base_scr[...] = base_scr[...] + jnp.sum(chosen, axis=0, keepdims=True)
    cnt_ref[...] = base_scr[...]


def _post(od, hm, x, kmem, vmem, wo1, wo2, g2, wcq, wco, g3, wr, br, tm):
    b, s, d = x.shape
    w = od.shape[-1]
    m = kmem.shape[1]
    blk = lambda bi, i: (bi, i, 0)
    mem = lambda bi, i: (bi, 0, 0)
    const = lambda bi, i: (0, 0)
    full = lambda a: pl.BlockSpec(a.shape, const)
    return pl.pallas_call(
        _post_kernel,
        grid=(b, s // tm),
        in_specs=[pl.BlockSpec((1, tm, w), blk), pl.BlockSpec((1, tm, w), blk),
                  pl.BlockSpec((1, tm, d), blk), pl.BlockSpec((1, m, d), mem),
                  pl.BlockSpec((1, m, d), mem), full(wo1), full(wo2), full(g2), full(wcq),
                  full(wco), full(g3), full(wr), full(br)],
        out_specs=[pl.BlockSpec((1, tm, d), blk),
                   pl.BlockSpec((1, tm * (d // LANES), LANES), blk),
                   pl.BlockSpec((1, tm, LANES), blk), pl.BlockSpec((1, tm, LANES), blk),
                   pl.BlockSpec((1, LANES), const),
                   pl.BlockSpec((1, 8, LANES), lambda bi, i: (bi * (s // tm) + i, 0, 0))],
        out_shape=[jax.ShapeDtypeStruct((b, s, d), F32), jax.ShapeDtypeStruct((b, s * (d // LANES), LANES), F32),
                   jax.ShapeDtypeStruct((b, s, LANES), I32), jax.ShapeDtypeStruct((b, s, LANES), F32),
                   jax.ShapeDtypeStruct((1, LANES), F32),
                   jax.ShapeDtypeStruct((b * (s // tm), 8, LANES), F32)],
        scratch_shapes=[pltpu.VMEM((1, LANES), F32)],
        compiler_params=_params(("arbitrary", "arbitrary")),
        name="post_mixer",
    )(od, hm, x, kmem, vmem, wo1, wo2, g2, wcq, wco, g3, wr, br)


def _split_kernel(w_ref, p_ref, wg_ref, wl_ref):
    wb = w_ref[0].astype(BF16)
    two = 2 * LANES
    for g in range(wb.shape[1] // two):
        r = jnp.dot(wb[:, g * two:(g + 1) * two], p_ref[...], preferred_element_type=F32)
        wg_ref[0, :, g * LANES:(g + 1) * LANES] = r[:, :LANES].astype(BF16)
        wl_ref[0, :, g * LANES:(g + 1) * LANES] = r[:, LANES:].astype(BF16)


def _split_w1(w1, rb):
    e, d, de2 = w1.shape
    two = 2 * LANES
    src = jnp.arange(two)[:, None]
    dst = jnp.arange(two)[None, :]
    perm = jnp.where(dst < LANES, src == 2 * dst, src == 2 * (dst - LANES) + 1).astype(BF16)
    blk = lambda ei, r: (ei, r, 0)
    return pl.pallas_call(
        _split_kernel,
        grid=(e, d // rb),
        in_specs=[pl.BlockSpec((1, rb, de2), blk), pl.BlockSpec((two, two), lambda ei, r: (0, 0))],
        out_specs=[pl.BlockSpec((1, rb, de2 // 2), blk)] * 2,
        out_shape=[jax.ShapeDtypeStruct((e, d, de2 // 2), BF16)] * 2,
        compiler_params=_params(("parallel", "parallel")),
        name="split_w1",
    )(w1, perm)


def _dispatch_kernel(nz_ref, dest_ref, x_ref, xs_hbm, zbuf, sem, zsem, *, tb, bm, n_blocks, sub):
    i = pl.program_id(0)
    tile = lambda r: pl.ds(pl.multiple_of(r * sub, sub), sub)

    @pl.when(i == 0)
    def _():
        zbuf[...] = jnp.zeros_like(zbuf)

        def zstart(blk, carry):
            @pl.when(nz_ref[blk] != 0)
            def _():
                pltpu.make_async_copy(zbuf, xs_hbm.at[pl.ds(pl.multiple_of(blk * bm * sub, bm * sub), bm * sub)],
                                      zsem).start()
            return carry
        lax.fori_loop(0, n_blocks, zstart, 0)

        def zwait(blk, carry):
            @pl.when(nz_ref[blk] != 0)
            def _():
                pltpu.make_async_copy(zbuf, xs_hbm.at[pl.ds(0, bm * sub)], zsem).wait()
            return carry
        lax.fori_loop(0, n_blocks, zwait, 0)

    def issue(t, carry):
        for k in range(TOP_K):
            dst = dest_ref[0, 0, t * TOP_K + k]
            pltpu.make_async_copy(x_ref.at[tile(t)], xs_hbm.at[tile(dst)], sem).start(priority=k % 2)
        return carry
    lax.fori_loop(0, tb, issue, 0)
    for k in range(TOP_K):
        pltpu.make_async_copy(x_ref, xs_hbm.at[pl.ds(0, tb * sub)], sem).wait()


def _dispatch(needs_zero, dest_blocks, xn3, sub, bm, tb):
    t = xn3.shape[0] // sub
    n_blocks = needs_zero.shape[0]
    grid_spec = pltpu.PrefetchScalarGridSpec(
        num_scalar_prefetch=1,
        grid=(t // tb,),
        in_specs=[pl.BlockSpec((1, 1, tb * TOP_K), lambda i, nz: (i, 0, 0), memory_space=pltpu.SMEM),
                  pl.BlockSpec((tb * sub, LANES), lambda i, nz: (i, 0))],
        out_specs=pl.BlockSpec(memory_space=pl.ANY),
        scratch_shapes=[pltpu.VMEM((bm * sub, LANES), F32), pltpu.SemaphoreType.DMA, pltpu.SemaphoreType.DMA],
    )
    return pl.pallas_call(
        functools.partial(_dispatch_kernel, tb=tb, bm=bm, n_blocks=n_blocks, sub=sub),
        grid_spec=grid_spec,
        out_shape=jax.ShapeDtypeStruct((n_blocks * bm * sub, LANES), F32),
        compiler_params=_params(("arbitrary",)),
        name="dispatch",
    )(needs_zero, dest_blocks, xn3)


def _expert_kernel(be_ref, na_ref, x_ref, w1g_ref, w1l_ref, b1g_ref, b1l_ref, w2_ref, b2_ref, y_ref):
    i = pl.program_id(0)

    @pl.when(i < na_ref[0])
    def _():
        xb = _load_row_tiles(x_ref, w1g_ref.shape[1] // LANES).astype(BF16)
        glu = jnp.dot(xb, w1g_ref[0], preferred_element_type=F32) + b1g_ref[0]
        lin = jnp.dot(xb, w1l_ref[0], preferred_element_type=F32) + b1l_ref[0]
        glu = jnp.minimum(glu, SWIGLU_LIMIT)
        lin = jnp.clip(lin, -SWIGLU_LIMIT, SWIGLU_LIMIT)
        act = glu * jax.nn.sigmoid(SWIGLU_ALPHA * glu) * (lin + 1.0)
        _store_row_tiles(y_ref, jnp.dot(act.astype(BF16), w2_ref[0], preferred_element_type=F32) + b2_ref[0])

    @pl.when(i >= na_ref[0])
    def _():
        y_ref[...] = jnp.zeros_like(y_ref)


def _experts(blk_expert, n_active, xs, w1g, w1l, b1g, b1l, w2, b2, bm):
    d = w1g.shape[1]
    sub = d // LANES
    n_pad = xs.shape[0] // sub
    n_blocks = n_pad // bm
    de = w1g.shape[-1]
    last = lambda i, na: jnp.minimum(i, na[0] - 1)
    wmap = lambda i, be, na: (be[last(i, na)], 0, 0)
    grid_spec = pltpu.PrefetchScalarGridSpec(
        num_scalar_prefetch=2,
        grid=(n_blocks,),
        in_specs=[pl.BlockSpec((bm * sub, LANES), lambda i, be, na: (last(i, na), 0)),
                  pl.BlockSpec((1, d, de), wmap), pl.BlockSpec((1, d, de), wmap),
                  pl.BlockSpec((1, 1, de), wmap), pl.BlockSpec((1, 1, de), wmap),
                  pl.BlockSpec((1, de, d), wmap), pl.BlockSpec((1, 1, d), wmap)],
        out_specs=pl.BlockSpec((bm * sub, LANES), lambda i, be, na: (i, 0)),
    )
    return pl.pallas_call(
        _expert_kernel,
        grid_spec=grid_spec,
        out_shape=jax.ShapeDtypeStruct((n_pad * sub, LANES), F32),
        compiler_params=_params(("arbitrary",)),
        name="experts",
    )(blk_expert, n_active, xs, w1g, w1l, b1g, b1l, w2, b2)


COMBINE_CHUNK = 8


def _combine_plan(dest, top_e, cstart, counts, pstart, tb, sub):
    ch = COMBINE_CHUNK
    nb, n_experts = cstart.shape
    t = dest.shape[0]
    cnt = jnp.concatenate([cstart[1:], counts[None, :]], axis=0) - cstart
    run_lo = pstart[None, :] + cstart
    lo = run_lo // ch * ch
    nch = jnp.where(cnt > 0, (run_lo + cnt - lo + ch - 1) // ch, 0)
    cum = jnp.cumsum(nch, axis=1)
    first = cum - nch
    max_ch = tb * TOP_K // ch + 2 * n_experts
    c = jnp.arange(max_ch, dtype=I32)
    e_of_c = jnp.minimum(jnp.sum(c[None, :, None] >= cum[:, None, :], axis=-1), n_experts - 1)
    experts = jnp.arange(n_experts, dtype=I32)
    pick_c = lambda tab: jnp.sum(jnp.where(e_of_c[:, :, None] == experts, tab[:, None, :], 0), axis=-1)
    src = jnp.where(c[None, :] < cum[:, -1:], pick_c(lo - first * ch) + c[None, :] * ch, 0).astype(I32)
    shift = (first * ch - lo)[:, None, None, :]
    hot = top_e.reshape(nb, tb, TOP_K)[..., None] == experts
    loc = (dest.reshape(nb, tb, TOP_K) + jnp.sum(jnp.where(hot, shift, 0), axis=-1)).astype(I32)
    return (src * sub).reshape(nb, 1, max_ch), (loc * sub).reshape(nb, 1, tb * TOP_K), max_ch


def _combine_kernel(src_ref, snext_ref, loc_ref, y_hbm, h2_ref, tg_ref, g_ref, o_ref,
                    buf, rows_scr, sem, *, tb, final_norm):
    i = pl.program_id(0)
    sub = h2_ref.shape[1] // LANES
    n_chunks = src_ref.shape[-1]
    rows = COMBINE_CHUNK * sub
    slot = i % 2

    def fetch(s_ref, s):
        def issue(c, carry):
            lo = pl.multiple_of(s_ref[0, 0, c], sub)
            pltpu.make_async_copy(y_hbm.at[pl.ds(lo, rows)], buf.at[s, pl.ds(pl.multiple_of(c * rows, rows), rows)],
                                  sem.at[s]).start(priority=1)
            return carry
        lax.fori_loop(0, n_chunks, issue, 0, unroll=4)

    @pl.when(i == 0)
    def _():
        fetch(src_ref, 0)

    @pl.when(i + 1 < pl.num_programs(0))
    def _():
        fetch(snext_ref, 1 - slot)

    pltpu.make_async_copy(y_hbm.at[pl.ds(0, n_chunks * rows)], buf.at[slot], sem.at[slot]).wait()

    def token(t, carry):
        dst = pl.ds(pl.multiple_of(t * sub, sub), sub)
        for k in range(TOP_K):
            r = pl.multiple_of(loc_ref[0, 0, t * TOP_K + k], sub)
            rows_scr[k, dst, :] = buf[slot, pl.ds(r, sub), :]
        return carry
    lax.fori_loop(0, tb, token, 0, unroll=4)

    acc = h2_ref[...]
    for k in range(TOP_K):
        acc = acc + _load_row_tiles(rows_scr.at[k], sub) * tg_ref[:, k:k + 1]
    o_ref[...] = _rms(acc, g_ref[...]) if final_norm else acc


def _combine(src, loc, max_ch, y, h2, tg, g, tb, final_norm):
    t, d = h2.shape
    sub = d // LANES
    nb = t // tb
    row = lambda i: (i, 0)
    smem = lambda n, shift: pl.BlockSpec((1, 1, n), lambda i: (jnp.minimum(i + shift, nb - 1), 0, 0),
                                         memory_space=pltpu.SMEM)
    return pl.pallas_call(
        functools.partial(_combine_kernel, tb=tb, final_norm=final_norm),
        grid=(nb,),
        in_specs=[smem(max_ch, 0), smem(max_ch, 1), smem(tb * TOP_K, 0),
                  pl.BlockSpec(memory_space=pl.ANY),
                  pl.BlockSpec((tb, d), row), pl.BlockSpec((tb, LANES), row),
                  pl.BlockSpec((1, d), lambda i: (0, 0))],
        out_specs=pl.BlockSpec((tb, d), row),
        out_shape=jax.ShapeDtypeStruct((t, d), F32),
        scratch_shapes=[pltpu.VMEM((2, max_ch * COMBINE_CHUNK * sub, LANES), F32),
                        pltpu.VMEM((TOP_K, tb * sub, LANES), F32), pltpu.SemaphoreType.DMA((2,))],
        compiler_params=_params(("arbitrary",)),
        name="combine",
    )(src, src, loc, y, h2, tg, g)


def _route(top_e, rank, counts, n_experts, bm):
    a = top_e.size
    padded = (counts + bm - 1) // bm * bm
    pend = jnp.cumsum(padded)
    pstart = pend - padded
    hot = top_e[..., None] == jnp.arange(n_experts, dtype=I32)
    dest = (jnp.sum(jnp.where(hot, pstart, 0), axis=-1) + rank).astype(I32)
    n_blocks = -(-a // bm) + n_experts
    blk_lo = jnp.arange(n_blocks, dtype=I32) * bm
    blk_expert = jnp.minimum(jnp.sum(blk_lo[:, None] >= pend[None, :], axis=1), n_experts - 1).astype(I32)
    n_active = (pend[-1] // bm).astype(I32)
    has_pad = jnp.any((blk_lo[:, None] + bm) == pend[None, :], axis=1)
    needs_zero = (has_pad | (jnp.arange(n_blocks) >= n_active)).astype(I32)
    return dest, blk_expert, n_active.reshape(1), needs_zero, pstart.astype(I32)


def _pad_cols(w, n):
    return jnp.pad(w, ((0, 0), (0, n - w.shape[1])))


def _layer(h, mem, l, p):
    b, s, d = h.shape
    t = b * s
    lambda_init = 0.8 - 0.6 * math.exp(-0.3 * l)
    n_main = 6 * GROUP
    w_in = p['w_in']
    w_main = w_in[:, :n_main].astype(BF16)
    w_i = _pad_cols(w_in[:, n_main:n_main + MLSTM_HEADS], LANES).astype(BF16)
    w_f = _pad_cols(w_in[:, n_main + MLSTM_HEADS:], LANES).astype(BF16)
    dq, dk, dv, mqk, mv, mo, gi, gf = _in_proj(h.reshape(t, d), p['norm_mix_g'].reshape(1, d),
                                               w_main, w_i, w_f, tm=min(512, t))
    r3 = lambda a: a.reshape(b, s, a.shape[-1])
    vec = lambda a: a.reshape(1, -1)
    od = _diff_attention(r3(dq), r3(dk), r3(dv), vec(p['lambda_q1']), vec(p['lambda_k1']),
                         vec(p['lambda_q2']), vec(p['lambda_k2']), vec(p['diff_norm_g']),
                         lambda_init, bq=min(512, s))
    hm = _mlstm(r3(mqk), r3(mv), r3(gi), r3(gf), r3(mo), p['conv_w'], vec(p['conv_b']),
                _pad_cols(vec(p['b_igate']), LANES), _pad_cols(vec(p['b_fgate']), LANES),
                vec(p['mlstm_norm_g']), bb=2 if b % 2 == 0 else 1)
    kmem, vmem = _kv_proj(mem, vec(p['norm_mem_g']), p['w_ckv'].astype(BF16))
    w_out = p['w_out'].astype(BF16)
    n_experts = p['w_router'].shape[1]
    wr = _pad_cols(p['w_router'], LANES)
    wr_hi = wr.astype(BF16)
    wr = jnp.concatenate([wr_hi, (wr - wr_hi.astype(F32)).astype(BF16)], axis=1)
    br =jnp.concatenate([vec(p['b_router']), jnp.full((1, LANES - n_experts), NEG_BIG, F32)], axis=1)
    tm = min(512, s)
    h2, xn3, te, tg, cnt, cstart = _post(od, hm, h, kmem, vmem, w_out[:GROUP], w_out[GROUP:],
                                         vec(p['norm_xattn_g']), p['w_cq'].astype(BF16), p['w_co'].astype(BF16),
                                         vec(p['norm_ffn_g']), wr, br, tm=tm)
    te = te.reshape(t, LANES)
    counts = cnt[0, :n_experts].astype(I32)
    dest, blk_expert, n_active, needs_zero, pstart = _route(te[:, :TOP_K], te[:, TOP_K:2 * TOP_K],
                                                            counts, n_experts, MOE_BLOCK)
    tb_d = min(512, t)
    xs = _dispatch(needs_zero, dest.reshape(t // tb_d, 1, tb_d * TOP_K), xn3.reshape(t * (d // LANES), LANES),
                   d // LANES, MOE_BLOCK, tb_d)
    w1g, w1l = _split_w1(p['w1'], rb=256)
    y = _experts(blk_expert, n_active, xs, w1g, w1l,
                 p['b1'][:, None, 0::2], p['b1'][:, None, 1::2],
                 p['w2'].astype(BF16), p['b2'][:, None, :], MOE_BLOCK)
    src, loc, max_ch = _combine_plan(dest, te[:, :TOP_K], cstart[:, 0, :n_experts].astype(I32),
                                     counts, pstart, tm, d // LANES)
    return functools.partial(_combine, src, loc, max_ch, y, h2.reshape(t, d), tg.reshape(t, LANES), tb=tm)


def kernel(x, mem, norm_mix_g, w_in, conv_w, conv_b, b_igate, b_fgate, mlstm_norm_g, lambda_q1, lambda_k1, lambda_q2, lambda_k2, diff_norm_g, w_out, norm_xattn_g, norm_mem_g, w_cq, w_ckv, w_co, norm_ffn_g, w_router, b_router, w1, b1, w2, b2, norm_final_g):
    stacked = dict(norm_mix_g=norm_mix_g, w_in=w_in, conv_w=conv_w, conv_b=conv_b, b_igate=b_igate,
                   b_fgate=b_fgate, mlstm_norm_g=mlstm_norm_g, lambda_q1=lambda_q1, lambda_k1=lambda_k1,
                   lambda_q2=lambda_q2, lambda_k2=lambda_k2, diff_norm_g=diff_norm_g, w_out=w_out,
                   norm_xattn_g=norm_xattn_g, norm_mem_g=norm_mem_g, w_cq=w_cq, w_ckv=w_ckv, w_co=w_co,
                   norm_ffn_g=norm_ffn_g, w_router=w_router, b_router=b_router, w1=w1, b1=b1, w2=w2, b2=b2)
    depth = w_in.shape[0]
    b, s, d = x.shape
    h = x
    for l in range(depth):
        p = {k: v[l] for k, v in stacked.items()}
        combine = _layer(h, mem, l, p)
        h = combine(norm_final_g.reshape(1, d), final_norm=l == depth - 1).reshape(b, s, d)
    return h
```

```python
import functools
import math

import jax
import jax.numpy as jnp
from jax import lax
from jax.experimental import pallas as pl
from jax.experimental.pallas import tpu as pltpu

F32 = jnp.float32
BF16 = jnp.bfloat16
I32 = jnp.int32

RMS_EPS = 1e-5
LANES = 128
VMEM_LIMIT = 56 * 1024 * 1024

DIFF_HEADS = 4
DIFF_QK_DIM = 64
MLSTM_HEADS = 4
MLSTM_QK_DIM = 64
CONV_K = 4
CHUNK = 128
XATTN_HEADS = 4
TOP_K = 4
SWIGLU_ALPHA = 1.702
SWIGLU_LIMIT = 7.0
MOE_BLOCK = 1024
GROUP = 512
NEG_BIG = -1e30


def _rms(x, g):
    return x * lax.rsqrt(jnp.mean(x * x, axis=-1, keepdims=True) + RMS_EPS) * g


def _store_row_tiles(ref, m):
    rows, d = m.shape
    sub = d // LANES
    for c in range(sub):
        ref[pl.ds(c, rows, stride=sub), :] = m[:, c * LANES:(c + 1) * LANES]


def _load_row_tiles(ref, sub):
    rows = ref.shape[0] // sub
    return jnp.concatenate([ref[pl.ds(c, rows, stride=sub), :] for c in range(sub)], axis=1)


def _params(sem, vmem=VMEM_LIMIT, flags=None):
    return pltpu.CompilerParams(dimension_semantics=sem, vmem_limit_bytes=vmem, flags=flags)


def _inproj_kernel(x_ref, g_ref, w_ref, wi_ref, wf_ref,
                   dq_ref, dk_ref, dv_ref, mqk_ref, mv_ref, mo_ref, gi_ref, gf_ref):
    xb = _rms(x_ref[...], g_ref[...]).astype(BF16)
    for n, o_ref in enumerate((dq_ref, dk_ref, dv_ref, mqk_ref, mv_ref, mo_ref)):
        o_ref[...] = jnp.dot(xb, w_ref[:, n * GROUP:(n + 1) * GROUP],
                             preferred_element_type=F32).astype(o_ref.dtype)
    gi_ref[...] = jnp.dot(xb, wi_ref[...], preferred_element_type=F32)
    gf_ref[...] = jnp.dot(xb, wf_ref[...], preferred_element_type=F32)


def _in_proj(x2, g, w_main, w_i, w_f, tm):
    t, d = x2.shape
    row = lambda i: (i, 0)
    const = lambda i: (0, 0)
    out_dtypes = (BF16, BF16, BF16, F32, BF16, F32)
    return pl.pallas_call(
        _inproj_kernel,
        grid=(t // tm,),
        in_specs=[pl.BlockSpec((tm, d), row), pl.BlockSpec((1, d), const),
                  pl.BlockSpec(w_main.shape, const), pl.BlockSpec(w_i.shape, const),
                  pl.BlockSpec(w_f.shape, const)],
        out_specs=[pl.BlockSpec((tm, GROUP), row)] * 6 + [pl.BlockSpec((tm, LANES), row)] * 2,
        out_shape=[jax.ShapeDtypeStruct((t, GROUP), dt) for dt in out_dtypes]
        + [jax.ShapeDtypeStruct((t, LANES), F32)] * 2,
        compiler_params=_params(("parallel",)),
        name="in_proj",
    )(x2, g, w_main, w_i, w_f)


def _diffattn_kernel(q_ref, k_ref, v_ref, lq1_ref, lk1_ref, lq2_ref, lk2_ref, g_ref, o_ref,
                     *, bq, lambda_init):
    s_len = q_ref.shape[1]
    lane = lax.broadcasted_iota(I32, (1, LANES), 1)
    lo = lane < DIFF_QK_DIM
    row = lax.broadcasted_iota(I32, (bq, bq), 0)
    col = lax.broadcasted_iota(I32, (bq, bq), 1)
    causal = col <= row
    lam = (jnp.exp(jnp.sum(lq1_ref[...] * lk1_ref[...], axis=-1, keepdims=True))
           - jnp.exp(jnp.sum(lq2_ref[...] * lk2_ref[...], axis=-1, keepdims=True)) + lambda_init)

    def scores(qm, i, j):
        kj = k_ref[0, j * bq:(j + 1) * bq, :]
        s = lax.dot_general(qm, kj, (((1,), (1,)), ((), ())), preferred_element_type=F32)
        return jnp.where(causal, s, -jnp.inf) if j == i else s

    def lane_fold(x, op):
        out = x[:, :LANES]
        for c in range(1, bq // LANES):
            out = op(out, x[:, c * LANES:(c + 1) * LANES])
        return out

    def softmax_av(qm, i):
        mx = lane_fold(scores(qm, i, 0), jnp.maximum)
        for j in range(1, i + 1):
            mx = jnp.maximum(mx, lane_fold(scores(qm, i, j), jnp.maximum))
        m = jnp.max(mx, axis=-1, keepdims=True)
        lsum, acc = None, None
        for j in range(i + 1):
            p = jnp.exp(scores(qm, i, j) - m)
            pv = jnp.dot(p.astype(BF16), v_ref[0, j * bq:(j + 1) * bq, :], preferred_element_type=F32)
            ps = lane_fold(p, jnp.add)
            lsum, acc = (ps, pv) if j == 0 else (lsum + ps, acc + pv)
        return acc / jnp.sum(lsum, axis=-1, keepdims=True)

    for i in range(s_len // bq):
        q = q_ref[0, i * bq:(i + 1) * bq, :] * jnp.asarray(DIFF_QK_DIM ** -0.5, BF16)
        zero = jnp.zeros_like(q)
        od = softmax_av(jnp.where(lo, q, zero), i) - lam * softmax_av(jnp.where(lo, zero, q), i)
        o_ref[0, i * bq:(i + 1) * bq, :] = (_rms(od, g_ref[...]) * (1.0 - lambda_init)).astype(o_ref.dtype)


def _diff_attention(dq, dk, dv, lq1, lk1, lq2, lk2, g, lambda_init, bq):
    b, s, w = dq.shape
    blk = lambda bi, h: (bi, 0, h)
    vec = lambda bi, h: (0, 0)
    return pl.pallas_call(
        functools.partial(_diffattn_kernel, bq=bq, lambda_init=lambda_init),
        grid=(b, DIFF_HEADS),
        in_specs=[pl.BlockSpec((1, s, LANES), blk)] * 3
        + [pl.BlockSpec((1, DIFF_QK_DIM), vec)] * 4
        + [pl.BlockSpec((1, LANES), lambda bi, h: (0, h))],
        out_specs=pl.BlockSpec((1, s, LANES), blk),
        out_shape=jax.ShapeDtypeStruct((b, s, w), BF16),
        compiler_params=_params(("parallel", "parallel")),
        name="diff_attn",
    )(dq, dk, dv, lq1, lk1, lq2, lk2, g)


def _log_sigmoid(x):
    return -(jnp.maximum(-x, 0.0) + jnp.log1p(jnp.exp(-jnp.abs(x))))


def _mlstm_kernel(mqk_ref, mv_ref, gi_ref, gf_ref, mo_ref, cw_ref, cb_ref, bi_ref, bf_ref, ng_ref,
                  o_ref, conv_scr, c_scr, m_scr, *, bb):
    c = pl.program_id(1)
    L = CHUNK
    dk = MLSTM_QK_DIM
    nq = MLSTM_HEADS * dk
    tail = 8

    @pl.when(c == 0)
    def _():
        conv_scr[:, 0:tail, :] = jnp.zeros((bb, tail, 2 * nq), F32)
        c_scr[...] = jnp.zeros_like(c_scr)
        m_scr[...] = jnp.zeros_like(m_scr)

    row = lax.broadcasted_iota(I32, (L, L), 0)
    col = lax.broadcasted_iota(I32, (L, L), 1)
    causal = col <= row
    tril = causal.astype(F32)
    lane = lax.broadcasted_iota(I32, (1, LANES), 1)
    sub = lax.broadcasted_iota(I32, (LANES, 1), 0)
    ones_col = jnp.where(lane == 0, 1.0, 0.0).astype(BF16) * jnp.ones((L, 1), BF16)

    for b in range(bb):
        conv_scr[b, tail:tail + L, :] = mqk_ref[b]
        y = cb_ref[...]
        for j in range(CONV_K):
            y = y + conv_scr[b, pl.ds(tail - (CONV_K - 1) + j, L), :] * cw_ref[j:j + 1, :]
        conv_scr[b, 0:tail, :] = conv_scr[b, L:L + tail, :]
        qk = y * jax.nn.sigmoid(y)

        ig = gi_ref[b] + bi_ref[...]
        lf = _log_sigmoid(gf_ref[b] + bf_ref[...])
        bcum = jnp.dot(tril, lf, preferred_element_type=F32, precision=lax.Precision.HIGHEST)
        a_t = (ig - bcum).T

        for pair in range(MLSTM_HEADS // 2):
            q_pair = qk[:, pair * LANES:(pair + 1) * LANES] * (dk ** -0.5)
            k_pair = qk[:, nq + pair * LANES:nq + (pair + 1) * LANES]
            kt_pair = k_pair.T
            c_old = c_scr[b, pair]
            c_bf = c_old.astype(BF16)
            upd = jnp.zeros_like(c_old)
            decays = []
            for hh in range(2):
                h = 2 * pair + hh
                sel = (lane >= hh * dk) & (lane < (hh + 1) * dk)
                qm = jnp.where(sel, q_pair, 0.0).astype(BF16)
                km = jnp.where(sel, k_pair, 0.0).astype(BF16)
                selr = (sub >= hh * dk) & (sub < (hh + 1) * dk)
                ktm = jnp.where(selr, kt_pair, 0.0).astype(BF16)
                v_h = mv_ref[b, :, h * LANES:(h + 1) * LANES]
                v_aug = jnp.concatenate([v_h, ones_col], axis=1)

                m_st = m_scr[b, h, 0:1, 0:1]
                bc = bcum[:, h:h + 1]
                ic = ig[:, h:h + 1]
                a_m = jnp.where(causal, a_t[h:h + 1, :], -jnp.inf)
                inter = bc + m_st
                m_t = jnp.maximum(inter, bc + jnp.max(a_m, axis=-1, keepdims=True))
                qkt = lax.dot_general(qm, km, (((1,), (1,)), ((), ())), preferred_element_type=F32)
                w = qkt * jnp.exp(a_m + (bc - m_t))
                g = jnp.exp(inter - m_t)
                num_aug = (g * jnp.dot(qm, c_bf, preferred_element_type=F32)
                           + jnp.dot(w.astype(BF16), v_aug, preferred_element_type=F32))
                num = num_aug[:, :LANES]
                den = num_aug[:, LANES:LANES + 1]
                h_t = num / jnp.maximum(jnp.abs(den), jnp.exp(-m_t))

                b_last = bc[L - 1:L, :]
                gs = b_last - bc + ic
                m_new = jnp.maximum(b_last + m_st, jnp.max(gs, axis=0, keepdims=True))
                decays.append(jnp.exp(b_last + m_st - m_new))
                ws = jnp.exp(gs - m_new)
                wsv = (ws * v_aug.astype(F32)).astype(BF16)
                upd = upd + jnp.dot(ktm, wsv, preferred_element_type=F32)
                m_scr[b, h, 0:1, :] = jnp.broadcast_to(m_new, (1, LANES))

                hn = _rms(h_t, ng_ref[:, h * LANES:(h + 1) * LANES])
                gate = jax.nn.sigmoid(mo_ref[b, :, h * LANES:(h + 1) * LANES])
                o_ref[b, :, h * LANES:(h + 1) * LANES] = (hn * gate).astype(o_ref.dtype)
            d_rows = jnp.where(sub < dk, decays[0], decays[1])
            c_scr[b, pair] = d_rows * c_old + upd


def _mlstm(mqk, mv, gi, gf, mo, conv_w, conv_b, b_i, b_f, norm_g, bb):
    b, s, w = mqk.shape
    blk = lambda bi, c: (bi, c, 0)
    const = lambda bi, c: (0, 0)
    return pl.pallas_call(
        functools.partial(_mlstm_kernel, bb=bb),
        grid=(b // bb, s // CHUNK),
        in_specs=[pl.BlockSpec((bb, CHUNK, w), blk), pl.BlockSpec((bb, CHUNK, w), blk),
                  pl.BlockSpec((bb, CHUNK, LANES), blk), pl.BlockSpec((bb, CHUNK, LANES), blk),
                  pl.BlockSpec((bb, CHUNK, w), blk),
                  pl.BlockSpec(conv_w.shape, const), pl.BlockSpec(conv_b.shape, const),
                  pl.BlockSpec(b_i.shape, const), pl.BlockSpec(b_f.shape, const),
                  pl.BlockSpec(norm_g.shape, const)],
        out_specs=pl.BlockSpec((bb, CHUNK, w), blk),
        out_shape=jax.ShapeDtypeStruct((b, s, w), BF16),
        scratch_shapes=[pltpu.VMEM((bb, CHUNK + 8, w), F32),
                        pltpu.VMEM((bb, MLSTM_HEADS // 2, LANES, 2 * LANES), F32),
                        pltpu.VMEM((bb, MLSTM_HEADS, 8, LANES), F32)],
        compiler_params=_params(("parallel", "arbitrary")),
        name="mlstm",
    )(mqk, mv, gi, gf, mo, conv_w, conv_b, b_i, b_f, norm_g)


def _kvproj_kernel(mem_ref, g_ref, w_ref, k_ref, v_ref):
    d = mem_ref.shape[-1]
    mb = _rms(mem_ref[0], g_ref[...]).astype(BF16)
    k_ref[0] = jnp.dot(mb, w_ref[:, :d], preferred_element_type=F32).astype(k_ref.dtype)
    v_ref[0] = jnp.dot(mb, w_ref[:, d:], preferred_element_type=F32).astype(v_ref.dtype)


def _kv_proj(mem, g, w_ckv):
    b, m, d = mem.shape
    blk = lambda bi: (bi, 0, 0)
    const = lambda bi: (0, 0)
    return pl.pallas_call(
        _kvproj_kernel,
        grid=(b,),
        in_specs=[pl.BlockSpec((1, m, d), blk), pl.BlockSpec((1, d), const),
                  pl.BlockSpec(w_ckv.shape, const)],
        out_specs=[pl.BlockSpec((1, m, d), blk)] * 2,
        out_shape=[jax.ShapeDtypeStruct((b, m, d), BF16)] * 2,
        compiler_params=_params(("parallel",)),
        name="kv_proj",
    )(mem, g, w_ckv)


def _post_kernel(od_ref, hm_ref, x_ref, k_ref, v_ref, wo1_ref, wo2_ref, g2_ref, wcq_ref, wco_ref,
                 g3_ref, wr_ref, br_ref, h2_ref, xn3_ref, te_ref, tg_ref, cnt_ref, cstart_ref, base_scr):
    d = x_ref.shape[-1]
    hd = d // XATTN_HEADS
    h1 = (x_ref[0] + jnp.dot(od_ref[0], wo1_ref[...], preferred_element_type=F32)
          + jnp.dot(hm_ref[0], wo2_ref[...], preferred_element_type=F32))
    q = jnp.dot(_rms(h1, g2_ref[...]).astype(BF16), wcq_ref[...], preferred_element_type=F32)
    q = (q * (hd ** -0.5)).astype(BF16)
    heads = []
    for h in range(XATTN_HEADS):
        sl = slice(h * hd, (h + 1) * hd)
        s = lax.dot_general(q[:, sl], k_ref[0, :, sl], (((1,), (1,)), ((), ())),
                            preferred_element_type=F32)
        e = jnp.exp(s - jnp.max(s, axis=-1, keepdims=True))
        p = e / jnp.sum(e, axis=-1, keepdims=True)
        heads.append(jnp.dot(p.astype(BF16), v_ref[0, :, sl], preferred_element_type=F32))
    o = jnp.concatenate(heads, axis=1).astype(BF16)
    h2 = h1 + jnp.dot(o, wco_ref[...], preferred_element_type=F32)
    h2_ref[0] = h2
    xn3 = _rms(h2, g3_ref[...])
    _store_row_tiles(xn3_ref.at[0], xn3)
    x_hi = xn3.astype(BF16)
    x_lo = (xn3 - x_hi.astype(F32)).astype(BF16)
    hh_hl = jnp.dot(x_hi, wr_ref[...], preferred_element_type=F32)
    lh = jnp.dot(x_lo, wr_ref[:, :LANES], preferred_element_type=F32)
    logits = hh_hl[:, :LANES] + (hh_hl[:, LANES:] + lh) + br_ref[...]
    lane = lax.broadcasted_iota(I32, logits.shape, 1)
    lane_f = lane.astype(F32)
    cur = logits
    te = jnp.zeros(logits.shape, F32)
    chosen = jnp.zeros(logits.shape, F32)
    vals, hits = [], []
    for k in range(TOP_K):
        m = jnp.max(cur, axis=-1, keepdims=True)
        idx = jnp.min(jnp.where(cur == m, lane_f, float(LANES)), axis=-1, keepdims=True)
        hit = lane_f == idx
        vals.append(m)
        hits.append(hit)
        te = jnp.where(lane == k, idx, te)
        chosen = jnp.where(hit, 1.0, chosen)
        cur = jnp.where(hit, NEG_BIG * 2.0, cur)
    es = [jnp.exp(v - vals[0]) for v in vals]
    tot = es[0] + es[1] + es[2] + es[3]
    tg = jnp.zeros(logits.shape, F32)
    for k in range(TOP_K):
        tg = jnp.where(lane == k, es[k] / tot, tg)
    tg_ref[0] = tg

    @pl.when((pl.program_id(0) == 0) & (pl.program_id(1) == 0))
    def _():
        base_scr[...] = jnp.zeros_like(base_scr)
    tm = logits.shape[0]
    earlier = (lax.broadcasted_iota(I32, (tm, tm), 1) < lax.broadcasted_iota(I32, (tm, tm), 0)).astype(BF16)
    prior = jnp.dot(earlier, chosen.astype(BF16), preferred_element_type=F32) + base_scr[...]
    for k in range(TOP_K):
        rank = jnp.sum(jnp.where(hits[k], prior, 0.0), axis=-1, keepdims=True)
        te = jnp.where(lane == TOP_K + k, rank, te)
    te_ref[0] = te.astype(I32)
    cstart_ref[0] = jnp.broadcast_to(base_scr[...], cstart_ref.shape[1:])
    base_scr[...] = base_scr[...] + jnp.sum(chosen, axis=0, keepdims=True)
    cnt_ref[...] = base_scr[...]


def _post(od, hm, x, kmem, vmem, wo1, wo2, g2, wcq, wco, g3, wr, br, tm):
    b, s, d = x.shape
    w = od.shape[-1]
    m = kmem.shape[1]
    blk = lambda bi, i: (bi, i, 0)
    mem = lambda bi, i: (bi, 0, 0)
    const = lambda bi, i: (0, 0)
    full = lambda a: pl.BlockSpec(a.shape, const)
    return pl.pallas_call(
        _post_kernel,
        grid=(b, s // tm),
        in_specs=[pl.BlockSpec((1, tm, w), blk), pl.BlockSpec((1, tm, w), blk),
                  pl.BlockSpec((1, tm, d), blk), pl.BlockSpec((1, m, d), mem),
                  pl.BlockSpec((1, m, d), mem), full(wo1), full(wo2), full(g2), full(wcq),
                  full(wco), full(g3), full(wr), full(br)],
        out_specs=[pl.BlockSpec((1, tm, d), blk),
                   pl.BlockSpec((1, tm * (d // LANES), LANES), blk),
                   pl.BlockSpec((1, tm, LANES), blk), pl.BlockSpec((1, tm, LANES), blk),
                   pl.BlockSpec((1, LANES), const),
                   pl.BlockSpec((1, 8, LANES), lambda bi, i: (bi * (s // tm) + i, 0, 0))],
        out_shape=[jax.ShapeDtypeStruct((b, s, d), F32), jax.ShapeDtypeStruct((b, s * (d // LANES), LANES), F32),
                   jax.ShapeDtypeStruct((b, s, LANES), I32), jax.ShapeDtypeStruct((b, s, LANES), F32),
                   jax.ShapeDtypeStruct((1, LANES), F32),
                   jax.ShapeDtypeStruct((b * (s // tm), 8, LANES), F32)],
        scratch_shapes=[pltpu.VMEM((1, LANES), F32)],
        compiler_params=_params(("arbitrary", "arbitrary")),
        name="post_mixer",
    )(od, hm, x, kmem, vmem, wo1, wo2, g2, wcq, wco, g3, wr, br)


def _split_kernel(w_ref, p_ref, wg_ref, wl_ref):
    wb = w_ref[0].astype(BF16)
    two = 2 * LANES
    for g in range(wb.shape[1] // two):
        r = jnp.dot(wb[:, g * two:(g + 1) * two], p_ref[...], preferred_element_type=F32)
        wg_ref[0, :, g * LANES:(g + 1) * LANES] = r[:, :LANES].astype(BF16)
        wl_ref[0, :, g * LANES:(g + 1) * LANES] = r[:, LANES:].astype(BF16)


def _split_w1(w1, rb):
    e, d, de2 = w1.shape
    two = 2 * LANES
    src = jnp.arange(two)[:, None]
    dst = jnp.arange(two)[None, :]
    perm = jnp.where(dst < LANES, src == 2 * dst, src == 2 * (dst - LANES) + 1).astype(BF16)
    blk = lambda ei, r: (ei, r, 0)
    return pl.pallas_call(
        _split_kernel,
        grid=(e, d // rb),
        in_specs=[pl.BlockSpec((1, rb, de2), blk), pl.BlockSpec((two, two), lambda ei, r: (0, 0))],
        out_specs=[pl.BlockSpec((1, rb, de2 // 2), blk)] * 2,
        out_shape=[jax.ShapeDtypeStruct((e, d, de2 // 2), BF16)] * 2,
        compiler_params=_params(("parallel", "parallel")),
        name="split_w1",
    )(w1, perm)


def _dispatch_kernel(nz_ref, dest_ref, x_ref, xs_hbm, zbuf, sem, zsem, *, tb, bm, n_blocks, sub):
    i = pl.program_id(0)
    tile = lambda r: pl.ds(pl.multiple_of(r * sub, sub), sub)

    @pl.when(i == 0)
    def _():
        zbuf[...] = jnp.zeros_like(zbuf)

        def zstart(blk, carry):
            @pl.when(nz_ref[blk] != 0)
            def _():
                pltpu.make_async_copy(zbuf, xs_hbm.at[pl.ds(pl.multiple_of(blk * bm * sub, bm * sub), bm * sub)],
                                      zsem).start()
            return carry
        lax.fori_loop(0, n_blocks, zstart, 0)

        def zwait(blk, carry):
            @pl.when(nz_ref[blk] != 0)
            def _():
                pltpu.make_async_copy(zbuf, xs_hbm.at[pl.ds(0, bm * sub)], zsem).wait()
            return carry
        lax.fori_loop(0, n_blocks, zwait, 0)

    def issue(t, carry):
        for k in range(TOP_K):
            dst = dest_ref[0, 0, t * TOP_K + k]
            pltpu.make_async_copy(x_ref.at[tile(t)], xs_hbm.at[tile(dst)], sem).start(priority=k % 2)
        return carry
    lax.fori_loop(0, tb, issue, 0)
    for k in range(TOP_K):
        pltpu.make_async_copy(x_ref, xs_hbm.at[pl.ds(0, tb * sub)], sem).wait()


def _dispatch(needs_zero, dest_blocks, xn3, sub, bm, tb):
    t = xn3.shape[0] // sub
    n_blocks = needs_zero.shape[0]
    grid_spec = pltpu.PrefetchScalarGridSpec(
        num_scalar_prefetch=1,
        grid=(t // tb,),
        in_specs=[pl.BlockSpec((1, 1, tb * TOP_K), lambda i, nz: (i, 0, 0), memory_space=pltpu.SMEM),
                  pl.BlockSpec((tb * sub, LANES), lambda i, nz: (i, 0))],
        out_specs=pl.BlockSpec(memory_space=pl.ANY),
        scratch_shapes=[pltpu.VMEM((bm * sub, LANES), F32), pltpu.SemaphoreType.DMA, pltpu.SemaphoreType.DMA],
    )
    return pl.pallas_call(
        functools.partial(_dispatch_kernel, tb=tb, bm=bm, n_blocks=n_blocks, sub=sub),
        grid_spec=grid_spec,
        out_shape=jax.ShapeDtypeStruct((n_blocks * bm * sub, LANES), F32),
        compiler_params=_params(("arbitrary",)),
        name="dispatch",
    )(needs_zero, dest_blocks, xn3)


def _expert_kernel(be_ref, na_ref, x_ref, w1g_ref, w1l_ref, b1g_ref, b1l_ref, w2_ref, b2_ref, y_ref):
    i = pl.program_id(0)

    @pl.when(i < na_ref[0])
    def _():
        xb = _load_row_tiles(x_ref, w1g_ref.shape[1] // LANES).astype(BF16)
        glu = jnp.dot(xb, w1g_ref[0], preferred_element_type=F32) + b1g_ref[0]
        lin = jnp.dot(xb, w1l_ref[0], preferred_element_type=F32) + b1l_ref[0]
        glu = jnp.minimum(glu, SWIGLU_LIMIT)
        lin = jnp.clip(lin, -SWIGLU_LIMIT, SWIGLU_LIMIT)
        act = glu * jax.nn.sigmoid(SWIGLU_ALPHA * glu) * (lin + 1.0)
        _store_row_tiles(y_ref, jnp.dot(act.astype(BF16), w2_ref[0], preferred_element_type=F32) + b2_ref[0])

    @pl.when(i >= na_ref[0])
    def _():
        y_ref[...] = jnp.zeros_like(y_ref)


def _experts(blk_expert, n_active, xs, w1g, w1l, b1g, b1l, w2, b2, bm):
    d = w1g.shape[1]
    sub = d // LANES
    n_pad = xs.shape[0] // sub
    n_blocks = n_pad // bm
    de = w1g.shape[-1]
    last = lambda i, na: jnp.minimum(i, na[0] - 1)
    wmap = lambda i, be, na: (be[last(i, na)], 0, 0)
    grid_spec = pltpu.PrefetchScalarGridSpec(
        num_scalar_prefetch=2,
        grid=(n_blocks,),
        in_specs=[pl.BlockSpec((bm * sub, LANES), lambda i, be, na: (last(i, na), 0)),
                  pl.BlockSpec((1, d, de), wmap), pl.BlockSpec((1, d, de), wmap),
                  pl.BlockSpec((1, 1, de), wmap), pl.BlockSpec((1, 1, de), wmap),
                  pl.BlockSpec((1, de, d), wmap), pl.BlockSpec((1, 1, d), wmap)],
        out_specs=pl.BlockSpec((bm * sub, LANES), lambda i, be, na: (i, 0)),
    )
    return pl.pallas_call(
        _expert_kernel,
        grid_spec=grid_spec,
        out_shape=jax.ShapeDtypeStruct((n_pad * sub, LANES), F32),
        compiler_params=_params(("arbitrary",)),
        name="experts",
    )(blk_expert, n_active, xs, w1g, w1l, b1g, b1l, w2, b2)


COMBINE_CHUNK = 8


def _combine_plan(dest, top_e, cstart, counts, pstart, tb, sub):
    ch = COMBINE_CHUNK
    nb, n_experts = cstart.shape
    t = dest.shape[0]
    cnt = jnp.concatenate([cstart[1:], counts[None, :]], axis=0) - cstart
    run_lo = pstart[None, :] + cstart
    lo = run_lo // ch * ch
    nch = jnp.where(cnt > 0, (run_lo + cnt - lo + ch - 1) // ch, 0)
    cum = jnp.cumsum(nch, axis=1)
    first = cum - nch
    max_ch = tb * TOP_K // ch + 2 * n_experts
    c = jnp.arange(max_ch, dtype=I32)
    e_of_c = jnp.minimum(jnp.sum(c[None, :, None] >= cum[:, None, :], axis=-1), n_experts - 1)
    experts = jnp.arange(n_experts, dtype=I32)
    pick_c = lambda tab: jnp.sum(jnp.where(e_of_c[:, :, None] == experts, tab[:, None, :], 0), axis=-1)
    src = jnp.where(c[None, :] < cum[:, -1:], pick_c(lo - first * ch) + c[None, :] * ch, 0).astype(I32)
    shift = (first * ch - lo)[:, None, None, :]
    hot = top_e.reshape(nb, tb, TOP_K)[..., None] == experts
    loc = (dest.reshape(nb, tb, TOP_K) + jnp.sum(jnp.where(hot, shift, 0), axis=-1)).astype(I32)
    return (src * sub).reshape(nb, 1, max_ch), (loc * sub).reshape(nb, 1, tb * TOP_K), max_ch


def _combine_kernel(src_ref, snext_ref, loc_ref, y_hbm, h2_ref, tg_ref, g_ref, o_ref,
                    buf, rows_scr, sem, *, tb, final_norm):
    i = pl.program_id(0)
    sub = h2_ref.shape[1] // LANES
    n_chunks = src_ref.shape[-1]
    rows = COMBINE_CHUNK * sub
    slot = i % 2

    def fetch(s_ref, s):
        def issue(c, carry):
            lo = pl.multiple_of(s_ref[0, 0, c], sub)
            pltpu.make_async_copy(y_hbm.at[pl.ds(lo, rows)], buf.at[s, pl.ds(pl.multiple_of(c * rows, rows), rows)],
                                  sem.at[s]).start(priority=1)
            return carry
        lax.fori_loop(0, n_chunks, issue, 0, unroll=4)

    @pl.when(i == 0)
    def _():
        fetch(src_ref, 0)

    @pl.when(i + 1 < pl.num_programs(0))
    def _():
        fetch(snext_ref, 1 - slot)

    pltpu.make_async_copy(y_hbm.at[pl.ds(0, n_chunks * rows)], buf.at[slot], sem.at[slot]).wait()

    def token(t, carry):
        dst = pl.ds(pl.multiple_of(t * sub, sub), sub)
        for k in range(TOP_K):
            r = pl.multiple_of(loc_ref[0, 0, t * TOP_K + k], sub)
            rows_scr[k, dst, :] = buf[slot, pl.ds(r, sub), :]
        return carry
    lax.fori_loop(0, tb, token, 0, unroll=4)

    acc = h2_ref[...]
    for k in range(TOP_K):
        acc = acc + _load_row_tiles(rows_scr.at[k], sub) * tg_ref[:, k:k + 1]
    o_ref[...] = _rms(acc, g_ref[...]) if final_norm else acc


def _combine(src, loc, max_ch, y, h2, tg, g, tb, final_norm):
    t, d = h2.shape
    sub = d // LANES
    nb = t // tb
    row = lambda i: (i, 0)
    smem = lambda n, shift: pl.BlockSpec((1, 1, n), lambda i: (jnp.minimum(i + shift, nb - 1), 0, 0),
                                         memory_space=pltpu.SMEM)
    return pl.pallas_call(
        functools.partial(_combine_kernel, tb=tb, final_norm=final_norm),
        grid=(nb,),
        in_specs=[smem(max_ch, 0), smem(max_ch, 1), smem(tb * TOP_K, 0),
                  pl.BlockSpec(memory_space=pl.ANY),
                  pl.BlockSpec((tb, d), row), pl.BlockSpec((tb, LANES), row),
                  pl.BlockSpec((1, d), lambda i: (0, 0))],
        out_specs=pl.BlockSpec((tb, d), row),
        out_shape=jax.ShapeDtypeStruct((t, d), F32),
        scratch_shapes=[pltpu.VMEM((2, max_ch * COMBINE_CHUNK * sub, LANES), F32),
                        pltpu.VMEM((TOP_K, tb * sub, LANES), F32), pltpu.SemaphoreType.DMA((2,))],
        compiler_params=_params(("arbitrary",)),
        name="combine",
    )(src, src, loc, y, h2, tg, g)


def _route(top_e, rank, counts, n_experts, bm):
    a = top_e.size
    padded = (counts + bm - 1) // bm * bm
    pend = jnp.cumsum(padded)
    pstart = pend - padded
    hot = top_e[..., None] == jnp.arange(n_experts, dtype=I32)
    dest = (jnp.sum(jnp.where(hot, pstart, 0), axis=-1) + rank).astype(I32)
    n_blocks = -(-a // bm) + n_experts
    blk_lo = jnp.arange(n_blocks, dtype=I32) * bm
    blk_expert = jnp.minimum(jnp.sum(blk_lo[:, None] >= pend[None, :], axis=1), n_experts - 1).astype(I32)
    n_active = (pend[-1] // bm).astype(I32)
    has_pad = jnp.any((blk_lo[:, None] + bm) == pend[None, :], axis=1)
    needs_zero = (has_pad | (jnp.arange(n_blocks) >= n_active)).astype(I32)
    return dest, blk_expert, n_active.reshape(1), needs_zero, pstart.astype(I32)


def _pad_cols(w, n):
    return jnp.pad(w, ((0, 0), (0, n - w.shape[1])))


def _layer(h, mem, l, p):
    b, s, d = h.shape
    t = b * s
    lambda_init = 0.8 - 0.6 * math.exp(-0.3 * l)
    n_main = 6 * GROUP
    w_in = p['w_in']
    w_main = w_in[:, :n_main].astype(BF16)
    w_i = _pad_cols(w_in[:, n_main:n_main + MLSTM_HEADS], LANES).astype(BF16)
    w_f = _pad_cols(w_in[:, n_main + MLSTM_HEADS:], LANES).astype(BF16)
    dq, dk, dv, mqk, mv, mo, gi, gf = _in_proj(h.reshape(t, d), p['norm_mix_g'].reshape(1, d),
                                               w_main, w_i, w_f, tm=min(512, t))
    r3 = lambda a: a.reshape(b, s, a.shape[-1])
    vec = lambda a: a.reshape(1, -1)
    od = _diff_attention(r3(dq), r3(dk), r3(dv), vec(p['lambda_q1']), vec(p['lambda_k1']),
                         vec(p['lambda_q2']), vec(p['lambda_k2']), vec(p['diff_norm_g']),
                         lambda_init, bq=min(512, s))
    hm = _mlstm(r3(mqk), r3(mv), r3(gi), r3(gf), r3(mo), p['conv_w'], vec(p['conv_b']),
                _pad_cols(vec(p['b_igate']), LANES), _pad_cols(vec(p['b_fgate']), LANES),
                vec(p['mlstm_norm_g']), bb=2 if b % 2 == 0 else 1)
    kmem, vmem = _kv_proj(mem, vec(p['norm_mem_g']), p['w_ckv'].astype(BF16))
    w_out = p['w_out'].astype(BF16)
    n_experts = p['w_router'].shape[1]
    wr = _pad_cols(p['w_router'], LANES)
    wr_hi = wr.astype(BF16)
    wr = jnp.concatenate([wr_hi, (wr - wr_hi.astype(F32)).astype(BF16)], axis=1)
    br =jnp.concatenate([vec(p['b_router']), jnp.full((1, LANES - n_experts), NEG_BIG, F32)], axis=1)
    tm = min(512, s)
    h2, xn3, te, tg, cnt, cstart = _post(od, hm, h, kmem, vmem, w_out[:GROUP], w_out[GROUP:],
                                         vec(p['norm_xattn_g']), p['w_cq'].astype(BF16), p['w_co'].astype(BF16),
                                         vec(p['norm_ffn_g']), wr, br, tm=tm)
    te = te.reshape(t, LANES)
    counts = cnt[0, :n_experts].astype(I32)
    dest, blk_expert, n_active, needs_zero, pstart = _route(te[:, :TOP_K], te[:, TOP_K:2 * TOP_K],
                                                            counts, n_experts, MOE_BLOCK)
    tb_d = min(512, t)
    xs = _dispatch(needs_zero, dest.reshape(t // tb_d, 1, tb_d * TOP_K), xn3.reshape(t * (d // LANES), LANES),
                   d // LANES, MOE_BLOCK, tb_d)
    w1g, w1l = _split_w1(p['w1'], rb=256)
    y = _experts(blk_expert, n_active, xs, w1g, w1l,
                 p['b1'][:, None, 0::2], p['b1'][:, None, 1::2],
                 p['w2'].astype(BF16), p['b2'][:, None, :], MOE_BLOCK)
    src, loc, max_ch = _combine_plan(dest, te[:, :TOP_K], cstart[:, 0, :n_experts].astype(I32),
                                     counts, pstart, tm, d // LANES)
    return functools.partial(_combine, src, loc, max_ch, y, h2.reshape(t, d), tg.reshape(t, LANES), tb=tm)


def kernel(x, mem, norm_mix_g, w_in, conv_w, conv_b, b_igate, b_fgate, mlstm_norm_g, lambda_q1, lambda_k1, lambda_q2, lambda_k2, diff_norm_g, w_out, norm_xattn_g, norm_mem_g, w_cq, w_ckv, w_co, norm_ffn_g, w_router, b_router, w1, b1, w2, b2, norm_final_g):
    stacked = dict(norm_mix_g=norm_mix_g, w_in=w_in, conv_w=conv_w, conv_b=conv_b, b_igate=b_igate,
                   b_fgate=b_fgate, mlstm_norm_g=mlstm_norm_g, lambda_q1=lambda_q1, lambda_k1=lambda_k1,
                   lambda_q2=lambda_q2, lambda_k2=lambda_k2, diff_norm_g=diff_norm_g, w_out=w_out,
                   norm_xattn_g=norm_xattn_g, norm_mem_g=norm_mem_g, w_cq=w_cq, w_ckv=w_ckv, w_co=w_co,
                   norm_ffn_g=norm_ffn_g, w_router=w_router, b_router=b_router, w1=w1, b1=b1, w2=w2, b2=b2)
    depth = w_in.shape[0]
    b, s, d = x.shape
    h = x
    for l in range(depth):
        p = {k: v[l] for k, v in stacked.items()}
        combine = _layer(h, mem, l, p)
        h = combine(norm_final_g.reshape(1, d), final_norm=l == depth - 1).reshape(b, s, d)
    return h
```

```python
import functools
import math

import jax
import jax.numpy as jnp
from jax import lax
from jax.experimental import pallas as pl
from jax.experimental.pallas import tpu as pltpu

F32 = jnp.float32
BF16 = jnp.bfloat16
I32 = jnp.int32

RMS_EPS = 1e-5
LANES = 128
VMEM_LIMIT = 56 * 1024 * 1024

DIFF_HEADS = 4
DIFF_QK_DIM = 64
MLSTM_HEADS = 4
MLSTM_QK_DIM = 64
CONV_K = 4
CHUNK = 128
XATTN_HEADS = 4
TOP_K = 4
SWIGLU_ALPHA = 1.702
SWIGLU_LIMIT = 7.0
MOE_BLOCK = 1024
GROUP = 512
NEG_INF = float("-inf")


def _rms(x, g):
    return x * lax.rsqrt(jnp.mean(x * x, axis=-1, keepdims=True) + RMS_EPS) * g


def _store_row_tiles(ref, m):
    rows, d = m.shape
    sub = d // LANES
    for c in range(sub):
        ref[pl.ds(c, rows, stride=sub), :] = m[:, c * LANES:(c + 1) * LANES]


def _load_row_tiles(ref, sub):
    rows = ref.shape[0] // sub
    return jnp.concatenate([ref[pl.ds(c, rows, stride=sub), :] for c in range(sub)], axis=1)


def _params(sem, vmem=VMEM_LIMIT, flags=None):
    return pltpu.CompilerParams(dimension_semantics=sem, vmem_limit_bytes=vmem, flags=flags)


def _inproj_kernel(x_ref, g_ref, w_ref, wi_ref, wf_ref,
                   dq_ref, dk_ref, dv_ref, mqk_ref, mv_ref, mo_ref, gi_ref, gf_ref):
    xb = _rms(x_ref[...], g_ref[...]).astype(BF16)
    for n, o_ref in enumerate((dq_ref, dk_ref, dv_ref, mqk_ref, mv_ref, mo_ref)):
        o_ref[...] = jnp.dot(xb, w_ref[:, n * GROUP:(n + 1) * GROUP],
                             preferred_element_type=F32).astype(o_ref.dtype)
    gi_ref[...] = jnp.dot(xb, wi_ref[...], preferred_element_type=F32)
    gf_ref[...] = jnp.dot(xb, wf_ref[...], preferred_element_type=F32)


def _in_proj(x2, g, w_main, w_i, w_f, tm):
    t, d = x2.shape
    row = lambda i: (i, 0)
    const = lambda i: (0, 0)
    out_dtypes = (BF16, BF16, BF16, F32, BF16, F32)
    return pl.pallas_call(
        _inproj_kernel,
        grid=(t // tm,),
        in_specs=[pl.BlockSpec((tm, d), row), pl.BlockSpec((1, d), const),
                  pl.BlockSpec(w_main.shape, const), pl.BlockSpec(w_i.shape, const),
                  pl.BlockSpec(w_f.shape, const)],
        out_specs=[pl.BlockSpec((tm, GROUP), row)] * 6 + [pl.BlockSpec((tm, LANES), row)] * 2,
        out_shape=[jax.ShapeDtypeStruct((t, GROUP), dt) for dt in out_dtypes]
        + [jax.ShapeDtypeStruct((t, LANES), F32)] * 2,
        compiler_params=_params(("parallel",)),
        name="in_proj",
    )(x2, g, w_main, w_i, w_f)


def _diffattn_kernel(q_ref, k_ref, v_ref, lq1_ref, lk1_ref, lq2_ref, lk2_ref, g_ref, o_ref,
                     *, bq, lambda_init):
    s_len = q_ref.shape[1]
    lane = lax.broadcasted_iota(I32, (1, LANES), 1)
    lo = lane < DIFF_QK_DIM
    row = lax.broadcasted_iota(I32, (bq, bq), 0)
    col = lax.broadcasted_iota(I32, (bq, bq), 1)
    causal = col <= row
    lam = (jnp.exp(jnp.sum(lq1_ref[...] * lk1_ref[...], axis=-1, keepdims=True))
           - jnp.exp(jnp.sum(lq2_ref[...] * lk2_ref[...], axis=-1, keepdims=True)) + lambda_init)

    def scores(qm, i, j):
        kj = k_ref[0, j * bq:(j + 1) * bq, :]
        s = lax.dot_general(qm, kj, (((1,), (1,)), ((), ())), preferred_element_type=F32)
        return jnp.where(causal, s, -jnp.inf) if j == i else s

    def lane_fold(x, op):
        out = x[:, :LANES]
        for c in range(1, bq // LANES):
            out = op(out, x[:, c * LANES:(c + 1) * LANES])
        return out

    def softmax_av(qm, i):
        mx = lane_fold(scores(qm, i, 0), jnp.maximum)
        for j in range(1, i + 1):
            mx = jnp.maximum(mx, lane_fold(scores(qm, i, j), jnp.maximum))
        m = jnp.max(mx, axis=-1, keepdims=True)
        lsum, acc = None, None
        for j in range(i + 1):
            p = jnp.exp(scores(qm, i, j) - m)
            pv = jnp.dot(p.astype(BF16), v_ref[0, j * bq:(j + 1) * bq, :], preferred_element_type=F32)
            ps = lane_fold(p, jnp.add)
            lsum, acc = (ps, pv) if j == 0 else (lsum + ps, acc + pv)
        return acc / jnp.sum(lsum, axis=-1, keepdims=True)

    for i in range(s_len // bq):
        q = q_ref[0, i * bq:(i + 1) * bq, :] * jnp.asarray(DIFF_QK_DIM ** -0.5, BF16)
        zero = jnp.zeros_like(q)
        od = softmax_av(jnp.where(lo, q, zero), i) - lam * softmax_av(jnp.where(lo, zero, q), i)
        o_ref[0, i * bq:(i + 1) * bq, :] = (_rms(od, g_ref[...]) * (1.0 - lambda_init)).astype(o_ref.dtype)


def _diff_attention(dq, dk, dv, lq1, lk1, lq2, lk2, g, lambda_init, bq):
    b, s, w = dq.shape
    blk = lambda bi, h: (bi, 0, h)
    vec = lambda bi, h: (0, 0)
    return pl.pallas_call(
        functools.partial(_diffattn_kernel, bq=bq, lambda_init=lambda_init),
        grid=(b, DIFF_HEADS),
        in_specs=[pl.BlockSpec((1, s, LANES), blk)] * 3
        + [pl.BlockSpec((1, DIFF_QK_DIM), vec)] * 4
        + [pl.BlockSpec((1, LANES), lambda bi, h: (0, h))],
        out_specs=pl.BlockSpec((1, s, LANES), blk),
        out_shape=jax.ShapeDtypeStruct((b, s, w), BF16),
        compiler_params=_params(("parallel", "parallel")),
        name="diff_attn",
    )(dq, dk, dv, lq1, lk1, lq2, lk2, g)


def _log_sigmoid(x):
    return -(jnp.maximum(-x, 0.0) + jnp.log1p(jnp.exp(-jnp.abs(x))))


def _mlstm_kernel(mqk_ref, mv_ref, gi_ref, gf_ref, mo_ref, cw_ref, cb_ref, bi_ref, bf_ref, ng_ref,
                  o_ref, conv_scr, c_scr, m_scr, *, bb):
    c = pl.program_id(1)
    L = CHUNK
    dk = MLSTM_QK_DIM
    nq = MLSTM_HEADS * dk
    tail = 8

    @pl.when(c == 0)
    def _():
        conv_scr[:, 0:tail, :] = jnp.zeros((bb, tail, 2 * nq), F32)
        c_scr[...] = jnp.zeros_like(c_scr)
        m_scr[...] = jnp.zeros_like(m_scr)

    row = lax.broadcasted_iota(I32, (L, L), 0)
    col = lax.broadcasted_iota(I32, (L, L), 1)
    causal = col <= row
    tril = causal.astype(F32)
    lane = lax.broadcasted_iota(I32, (1, LANES), 1)
    sub = lax.broadcasted_iota(I32, (LANES, 1), 0)
    ones_col = jnp.where(lane == 0, 1.0, 0.0).astype(BF16) * jnp.ones((L, 1), BF16)

    for b in range(bb):
        conv_scr[b, tail:tail + L, :] = mqk_ref[b]
        y = cb_ref[...]
        for j in range(CONV_K):
            y = y + conv_scr[b, pl.ds(tail - (CONV_K - 1) + j, L), :] * cw_ref[j:j + 1, :]
        conv_scr[b, 0:tail, :] = conv_scr[b, L:L + tail, :]
        qk = y * jax.nn.sigmoid(y)

        ig = gi_ref[b] + bi_ref[...]
        lf = _log_sigmoid(gf_ref[b] + bf_ref[...])
        bcum = jnp.dot(tril, lf, preferred_element_type=F32, precision=lax.Precision.HIGHEST)
        a_t = (ig - bcum).T

        for pair in range(MLSTM_HEADS // 2):
            q_pair = qk[:, pair * LANES:(pair + 1) * LANES] * (dk ** -0.5)
            k_pair = qk[:, nq + pair * LANES:nq + (pair + 1) * LANES]
            kt_pair = k_pair.T
            c_old = c_scr[b, pair]
            c_bf = c_old.astype(BF16)
            upd = jnp.zeros_like(c_old)
            decays = []
            for hh in range(2):
                h = 2 * pair + hh
                sel = (lane >= hh * dk) & (lane < (hh + 1) * dk)
                qm = jnp.where(sel, q_pair, 0.0).astype(BF16)
                km = jnp.where(sel, k_pair, 0.0).astype(BF16)
                selr = (sub >= hh * dk) & (sub < (hh + 1) * dk)
                ktm = jnp.where(selr, kt_pair, 0.0).astype(BF16)
                v_h = mv_ref[b, :, h * LANES:(h + 1) * LANES]
                v_aug = jnp.concatenate([v_h, ones_col], axis=1)

                m_st = m_scr[b, h, 0:1, 0:1]
                bc = bcum[:, h:h + 1]
                ic = ig[:, h:h + 1]
                a_m = jnp.where(causal, a_t[h:h + 1, :], -jnp.inf)
                inter = bc + m_st
                m_t = jnp.maximum(inter, bc + jnp.max(a_m, axis=-1, keepdims=True))
                qkt = lax.dot_general(qm, km, (((1,), (1,)), ((), ())), preferred_element_type=F32)
                w = qkt * jnp.exp(a_m + (bc - m_t))
                g = jnp.exp(inter - m_t)
                num_aug = (g * jnp.dot(qm, c_bf, preferred_element_type=F32)
                           + jnp.dot(w.astype(BF16), v_aug, preferred_element_type=F32))
                num = num_aug[:, :LANES]
                den = num_aug[:, LANES:LANES + 1]
                h_t = num / jnp.maximum(jnp.abs(den), jnp.exp(-m_t))

                b_last = bc[L - 1:L, :]
                gs = b_last - bc + ic
                m_new = jnp.maximum(b_last + m_st, jnp.max(gs, axis=0, keepdims=True))
                decays.append(jnp.exp(b_last + m_st - m_new))
                ws = jnp.exp(gs - m_new)
                wsv = (ws * v_aug.astype(F32)).astype(BF16)
                upd = upd + jnp.dot(ktm, wsv, preferred_element_type=F32)
                m_scr[b, h, 0:1, :] = jnp.broadcast_to(m_new, (1, LANES))

                hn = _rms(h_t, ng_ref[:, h * LANES:(h + 1) * LANES])
                gate = jax.nn.sigmoid(mo_ref[b, :, h * LANES:(h + 1) * LANES])
                o_ref[b, :, h * LANES:(h + 1) * LANES] = (hn * gate).astype(o_ref.dtype)
            d_rows = jnp.where(sub < dk, decays[0], decays[1])
            c_scr[b, pair] = d_rows * c_old + upd


def _mlstm(mqk, mv, gi, gf, mo, conv_w, conv_b, b_i, b_f, norm_g, bb):
    b, s, w = mqk.shape
    blk = lambda bi, c: (bi, c, 0)
    const = lambda bi, c: (0, 0)
    return pl.pallas_call(
        functools.partial(_mlstm_kernel, bb=bb),
        grid=(b // bb, s // CHUNK),
        in_specs=[pl.BlockSpec((bb, CHUNK, w), blk), pl.BlockSpec((bb, CHUNK, w), blk),
                  pl.BlockSpec((bb, CHUNK, LANES), blk), pl.BlockSpec((bb, CHUNK, LANES), blk),
                  pl.BlockSpec((bb, CHUNK, w), blk),
                  pl.BlockSpec(conv_w.shape, const), pl.BlockSpec(conv_b.shape, const),
                  pl.BlockSpec(b_i.shape, const), pl.BlockSpec(b_f.shape, const),
                  pl.BlockSpec(norm_g.shape, const)],
        out_specs=pl.BlockSpec((bb, CHUNK, w), blk),
        out_shape=jax.ShapeDtypeStruct((b, s, w), BF16),
        scratch_shapes=[pltpu.VMEM((bb, CHUNK + 8, w), F32),
                        pltpu.VMEM((bb, MLSTM_HEADS // 2, LANES, 2 * LANES), F32),
                        pltpu.VMEM((bb, MLSTM_HEADS, 8, LANES), F32)],
        compiler_params=_params(("parallel", "arbitrary")),
        name="mlstm",
    )(mqk, mv, gi, gf, mo, conv_w, conv_b, b_i, b_f, norm_g)


def _kvproj_kernel(mem_ref, g_ref, w_ref, k_ref, v_ref):
    d = mem_ref.shape[-1]
    mb = _rms(mem_ref[0], g_ref[...]).astype(BF16)
    k_ref[0] = jnp.dot(mb, w_ref[:, :d], preferred_element_type=F32).astype(k_ref.dtype)
    v_ref[0] = jnp.dot(mb, w_ref[:, d:], preferred_element_type=F32).astype(v_ref.dtype)


def _kv_proj(mem, g, w_ckv):
    b, m, d = mem.shape
    blk = lambda bi: (bi, 0, 0)
    const = lambda bi: (0, 0)
    return pl.pallas_call(
        _kvproj_kernel,
        grid=(b,),
        in_specs=[pl.BlockSpec((1, m, d), blk), pl.BlockSpec((1, d), const),
                  pl.BlockSpec(w_ckv.shape, const)],
        out_specs=[pl.BlockSpec((1, m, d), blk)] * 2,
        out_shape=[jax.ShapeDtypeStruct((b, m, d), BF16)] * 2,
        compiler_params=_params(("parallel",)),
        name="kv_proj",
    )(mem, g, w_ckv)


def _post_kernel(od_ref, hm_ref, x_ref, k_ref, v_ref, wo1_ref, wo2_ref, g2_ref, wcq_ref, wco_ref,
                 g3_ref, wr_ref, br_ref, h2_ref, xn3_ref, te_ref, tg_ref, cnt_ref, cstart_ref, base_scr):
    d = x_ref.shape[-1]
    hd = d // XATTN_HEADS
    h1 = (x_ref[0] + jnp.dot(od_ref[0], wo1_ref[...], preferred_element_type=F32)
          + jnp.dot(hm_ref[0], wo2_ref[...], preferred_element_type=F32))
    q = jnp.dot(_rms(h1, g2_ref[...]).astype(BF16), wcq_ref[...], preferred_element_type=F32)
    q = (q * (hd ** -0.5)).astype(BF16)
    heads = []
    for h in range(XATTN_HEADS):
        sl = slice(h * hd, (h + 1) * hd)
        s = lax.dot_general(q[:, sl], k_ref[0, :, sl], (((1,), (1,)), ((), ())),
                            preferred_element_type=F32)
        e = jnp.exp(s - jnp.max(s, axis=-1, keepdims=True))
        p = e / jnp.sum(e, axis=-1, keepdims=True)
        heads.append(jnp.dot(p.astype(BF16), v_ref[0, :, sl], preferred_element_type=F32))
    o = jnp.concatenate(heads, axis=1).astype(BF16)
    h2 = h1 + jnp.dot(o, wco_ref[...], preferred_element_type=F32)
    h2_ref[0] = h2
    xn3 = _rms(h2, g3_ref[...])
    _store_row_tiles(xn3_ref.at[0], xn3)
    x_hi = xn3.astype(BF16)
    x_lo = (xn3 - x_hi.astype(F32)).astype(BF16)
    hh_hl = jnp.dot(x_hi, wr_ref[...], preferred_element_type=F32)
    lh = jnp.dot(x_lo, wr_ref[:, :LANES], preferred_element_type=F32)
    logits = hh_hl[:, :LANES] + (hh_hl[:, LANES:] + lh) + br_ref[...]
    lane = lax.broadcasted_iota(I32, logits.shape, 1)
    lane_f = lane.astype(F32)
    cur = logits
    te = jnp.zeros(logits.shape, F32)
    chosen = jnp.zeros(logits.shape, F32)
    vals, hits = [], []
    for k in range(TOP_K):
        m = jnp.max(cur, axis=-1, keepdims=True)
        idx = jnp.min(jnp.where(cur == m, lane_f, float(LANES)), axis=-1, keepdims=True)
        hit = lane_f == idx
        vals.append(m)
        hits.append(hit)
        te = jnp.where(lane == k, idx, te)
        chosen = jnp.where(hit, 1.0, chosen)
        cur = jnp.where(hit, NEG_INF, cur)
    es = [jnp.exp(v - vals[0]) for v in vals]
    tot = es[0] + es[1] + es[2] + es[3]
    tg = jnp.zeros(logits.shape, F32)
    for k in range(TOP_K):
        tg = jnp.where(lane == k, es[k] / tot, tg)
    tg_ref[0] = tg

    @pl.when((pl.program_id(0) == 0) & (pl.program_id(1) == 0))
    def _():
        base_scr[...] = jnp.zeros_like(base_scr)
    tm = logits.shape[0]
    earlier = (lax.broadcasted_iota(I32, (tm, tm), 1) < lax.broadcasted_iota(I32, (tm, tm), 0)).astype(BF16)
    prior = jnp.dot(earlier, chosen.astype(BF16), preferred_element_type=F32) + base_scr[...]
    for k in range(TOP_K):
        rank = jnp.sum(jnp.where(hits[k], prior, 0.0), axis=-1, keepdims=True)
        te = jnp.where(lane == TOP_K + k, rank, te)
    te_ref[0] = te.astype(I32)
    cstart_ref[0] = jnp.broadcast_to(base_scr[...], cstart_ref.shape[1:])
    base_scr[...] = base_scr[...] + jnp.sum(chosen, axis=0, keepdims=True)
    cnt_ref[...] = base_scr[...]


def _post(od, hm, x, kmem, vmem, wo1, wo2, g2, wcq, wco, g3, wr, br, tm):
    b, s, d = x.shape
    w = od.shape[-1]
    m = kmem.shape[1]
    blk = lambda bi, i: (bi, i, 0)
    mem = lambda bi, i: (bi, 0, 0)
    const = lambda bi, i: (0, 0)
    full = lambda a: pl.BlockSpec(a.shape, const)
    return pl.pallas_call(
        _post_kernel,
        grid=(b, s // tm),
        in_specs=[pl.BlockSpec((1, tm, w), blk), pl.BlockSpec((1, tm, w), blk),
                  pl.BlockSpec((1, tm, d), blk), pl.BlockSpec((1, m, d), mem),
                  pl.BlockSpec((1, m, d), mem), full(wo1), full(wo2), full(g2), full(wcq),
                  full(wco), full(g3), full(wr), full(br)],
        out_specs=[pl.BlockSpec((1, tm, d), blk),
                   pl.BlockSpec((1, tm * (d // LANES), LANES), blk),
                   pl.BlockSpec((1, tm, LANES), blk), pl.BlockSpec((1, tm, LANES), blk),
                   pl.BlockSpec((1, LANES), const),
                   pl.BlockSpec((1, 8, LANES), lambda bi, i: (bi * (s // tm) + i, 0, 0))],
        out_shape=[jax.ShapeDtypeStruct((b, s, d), F32), jax.ShapeDtypeStruct((b, s * (d // LANES), LANES), F32),
                   jax.ShapeDtypeStruct((b, s, LANES), I32), jax.ShapeDtypeStruct((b, s, LANES), F32),
                   jax.ShapeDtypeStruct((1, LANES), F32),
                   jax.ShapeDtypeStruct((b * (s // tm), 8, LANES), F32)],
        scratch_shapes=[pltpu.VMEM((1, LANES), F32)],
        compiler_params=_params(("arbitrary", "arbitrary")),
        name="post_mixer",
    )(od, hm, x, kmem, vmem, wo1, wo2, g2, wcq, wco, g3, wr, br)


def _split_kernel(w_ref, p_ref, wg_ref, wl_ref):
    wb = w_ref[0].astype(BF16)
    two = 2 * LANES
    for g in range(wb.shape[1] // two):
        r = jnp.dot(wb[:, g * two:(g + 1) * two], p_ref[...], preferred_element_type=F32)
        wg_ref[0, :, g * LANES:(g + 1) * LANES] = r[:, :LANES].astype(BF16)
        wl_ref[0, :, g * LANES:(g + 1) * LANES] = r[:, LANES:].astype(BF16)


def _split_w1(w1, rb):
    e, d, de2 = w1.shape
    two = 2 * LANES
    src = jnp.arange(two)[:, None]
    dst = jnp.arange(two)[None, :]
    perm = jnp.where(dst < LANES, src == 2 * dst, src == 2 * (dst - LANES) + 1).astype(BF16)
    blk = lambda ei, r: (ei, r, 0)
    return pl.pallas_call(
        _split_kernel,
        grid=(e, d // rb),
        in_specs=[pl.BlockSpec((1, rb, de2), blk), pl.BlockSpec((two, two), lambda ei, r: (0, 0))],
        out_specs=[pl.BlockSpec((1, rb, de2 // 2), blk)] * 2,
        out_shape=[jax.ShapeDtypeStruct((e, d, de2 // 2), BF16)] * 2,
        compiler_params=_params(("parallel", "parallel")),
        name="split_w1",
    )(w1, perm)


def _dispatch_kernel(nz_ref, dest_ref, x_ref, xs_hbm, zbuf, sem, zsem, *, tb, bm, n_blocks, sub):
    i = pl.program_id(0)
    tile = lambda r: pl.ds(pl.multiple_of(r * sub, sub), sub)

    @pl.when(i == 0)
    def _():
        zbuf[...] = jnp.zeros_like(zbuf)

        def zstart(blk, carry):
            @pl.when(nz_ref[blk] != 0)
            def _():
                pltpu.make_async_copy(zbuf, xs_hbm.at[pl.ds(pl.multiple_of(blk * bm * sub, bm * sub), bm * sub)],
                                      zsem).start()
            return carry
        lax.fori_loop(0, n_blocks, zstart, 0)

        def zwait(blk, carry):
            @pl.when(nz_ref[blk] != 0)
            def _():
                pltpu.make_async_copy(zbuf, xs_hbm.at[pl.ds(0, bm * sub)], zsem).wait()
            return carry
        lax.fori_loop(0, n_blocks, zwait, 0)

    def issue(t, carry):
        for k in range(TOP_K):
            dst = dest_ref[0, 0, t * TOP_K + k]
            pltpu.make_async_copy(x_ref.at[tile(t)], xs_hbm.at[tile(dst)], sem).start(priority=k % 2)
        return carry
    lax.fori_loop(0, tb, issue, 0)
    for k in range(TOP_K):
        pltpu.make_async_copy(x_ref, xs_hbm.at[pl.ds(0, tb * sub)], sem).wait()


def _dispatch(needs_zero, dest_blocks, xn3, sub, bm, tb):
    t = xn3.shape[0] // sub
    n_blocks = needs_zero.shape[0]
    grid_spec = pltpu.PrefetchScalarGridSpec(
        num_scalar_prefetch=1,
        grid=(t // tb,),
        in_specs=[pl.BlockSpec((1, 1, tb * TOP_K), lambda i, nz: (i, 0, 0), memory_space=pltpu.SMEM),
                  pl.BlockSpec((tb * sub, LANES), lambda i, nz: (i, 0))],
        out_specs=pl.BlockSpec(memory_space=pl.ANY),
        scratch_shapes=[pltpu.VMEM((bm * sub, LANES), F32), pltpu.SemaphoreType.DMA, pltpu.SemaphoreType.DMA],
    )
    return pl.pallas_call(
        functools.partial(_dispatch_kernel, tb=tb, bm=bm, n_blocks=n_blocks, sub=sub),
        grid_spec=grid_spec,
        out_shape=jax.ShapeDtypeStruct((n_blocks * bm * sub, LANES), F32),
        compiler_params=_params(("arbitrary",)),
        name="dispatch",
    )(needs_zero, dest_blocks, xn3)


def _expert_kernel(be_ref, na_ref, x_ref, w1g_ref, w1l_ref, b1g_ref, b1l_ref, w2_ref, b2_ref, y_ref):
    i = pl.program_id(0)

    @pl.when(i < na_ref[0])
    def _():
        xb = _load_row_tiles(x_ref, w1g_ref.shape[1] // LANES).astype(BF16)
        glu = jnp.dot(xb, w1g_ref[0], preferred_element_type=F32) + b1g_ref[0]
        lin = jnp.dot(xb, w1l_ref[0], preferred_element_type=F32) + b1l_ref[0]
        glu = jnp.minimum(glu, SWIGLU_LIMIT)
        lin = jnp.clip(lin, -SWIGLU_LIMIT, SWIGLU_LIMIT)
        act = glu * jax.nn.sigmoid(SWIGLU_ALPHA * glu) * (lin + 1.0)
        _store_row_tiles(y_ref, jnp.dot(act.astype(BF16), w2_ref[0], preferred_element_type=F32) + b2_ref[0])

    @pl.when(i >= na_ref[0])
    def _():
        y_ref[...] = jnp.zeros_like(y_ref)


def _experts(blk_expert, n_active, xs, w1g, w1l, b1g, b1l, w2, b2, bm):
    d = w1g.shape[1]
    sub = d // LANES
    n_pad = xs.shape[0] // sub
    n_blocks = n_pad // bm
    de = w1g.shape[-1]
    last = lambda i, na: jnp.minimum(i, na[0] - 1)
    wmap = lambda i, be, na: (be[last(i, na)], 0, 0)
    grid_spec = pltpu.PrefetchScalarGridSpec(
        num_scalar_prefetch=2,
        grid=(n_blocks,),
        in_specs=[pl.BlockSpec((bm * sub, LANES), lambda i, be, na: (last(i, na), 0)),
                  pl.BlockSpec((1, d, de), wmap), pl.BlockSpec((1, d, de), wmap),
                  pl.BlockSpec((1, 1, de), wmap), pl.BlockSpec((1, 1, de), wmap),
                  pl.BlockSpec((1, de, d), wmap), pl.BlockSpec((1, 1, d), wmap)],
        out_specs=pl.BlockSpec((bm * sub, LANES), lambda i, be, na: (i, 0)),
    )
    return pl.pallas_call(
        _expert_kernel,
        grid_spec=grid_spec,
        out_shape=jax.ShapeDtypeStruct((n_pad * sub, LANES), F32),
        compiler_params=_params(("arbitrary",)),
        name="experts",
    )(blk_expert, n_active, xs, w1g, w1l, b1g, b1l, w2, b2)


COMBINE_CHUNK = 8


def _combine_plan(dest, top_e, cstart, counts, pstart, tb, sub):
    ch = COMBINE_CHUNK
    nb, n_experts = cstart.shape
    t = dest.shape[0]
    cnt = jnp.concatenate([cstart[1:], counts[None, :]], axis=0) - cstart
    run_lo = pstart[None, :] + cstart
    lo = run_lo // ch * ch
    nch = jnp.where(cnt > 0, (run_lo + cnt - lo + ch - 1) // ch, 0)
    cum = jnp.cumsum(nch, axis=1)
    first = cum - nch
    max_ch = tb * TOP_K // ch + 2 * n_experts
    c = jnp.arange(max_ch, dtype=I32)
    e_of_c = jnp.minimum(jnp.sum(c[None, :, None] >= cum[:, None, :], axis=-1), n_experts - 1)
    experts = jnp.arange(n_experts, dtype=I32)
    pick_c = lambda tab: jnp.sum(jnp.where(e_of_c[:, :, None] == experts, tab[:, None, :], 0), axis=-1)
    src = jnp.where(c[None, :] < cum[:, -1:], pick_c(lo - first * ch) + c[None, :] * ch, 0).astype(I32)
    shift = (first * ch - lo)[:, None, None, :]
    hot = top_e.reshape(nb, tb, TOP_K)[..., None] == experts
    loc = (dest.reshape(nb, tb, TOP_K) + jnp.sum(jnp.where(hot, shift, 0), axis=-1)).astype(I32)
    return (src * sub).reshape(nb, 1, max_ch), (loc * sub).reshape(nb, 1, tb * TOP_K), max_ch


def _combine_kernel(src_ref, snext_ref, loc_ref, y_hbm, h2_ref, tg_ref, g_ref, o_ref,
                    buf, rows_scr, sem, *, tb, final_norm):
    i = pl.program_id(0)
    sub = h2_ref.shape[1] // LANES
    n_chunks = src_ref.shape[-1]
    rows = COMBINE_CHUNK * sub
    slot = i % 2

    def fetch(s_ref, s):
        def issue(c, carry):
            lo = pl.multiple_of(s_ref[0, 0, c], sub)
            pltpu.make_async_copy(y_hbm.at[pl.ds(lo, rows)], buf.at[s, pl.ds(pl.multiple_of(c * rows, rows), rows)],
                                  sem.at[s]).start(priority=1)
            return carry
        lax.fori_loop(0, n_chunks, issue, 0, unroll=4)

    @pl.when(i == 0)
    def _():
        fetch(src_ref, 0)

    @pl.when(i + 1 < pl.num_programs(0))
    def _():
        fetch(snext_ref, 1 - slot)

    pltpu.make_async_copy(y_hbm.at[pl.ds(0, n_chunks * rows)], buf.at[slot], sem.at[slot]).wait()

    def token(t, carry):
        dst = pl.ds(pl.multiple_of(t * sub, sub), sub)
        for k in range(TOP_K):
            r = pl.multiple_of(loc_ref[0, 0, t * TOP_K + k], sub)
            rows_scr[k, dst, :] = buf[slot, pl.ds(r, sub), :]
        return carry
    lax.fori_loop(0, tb, token, 0, unroll=8)

    acc = h2_ref[...]
    for k in range(TOP_K):
        acc = acc + _load_row_tiles(rows_scr.at[k], sub) * tg_ref[:, k:k + 1]
    o_ref[...] = _rms(acc, g_ref[...]) if final_norm else acc


def _combine(src, loc, max_ch, y, h2, tg, g, tb, final_norm):
    t, d = h2.shape
    sub = d // LANES
    nb = t // tb
    row = lambda i: (i, 0)
    smem = lambda n, shift: pl.BlockSpec((1, 1, n), lambda i: (jnp.minimum(i + shift, nb - 1), 0, 0),
                                         memory_space=pltpu.SMEM)
    return pl.pallas_call(
        functools.partial(_combine_kernel, tb=tb, final_norm=final_norm),
        grid=(nb,),
        in_specs=[smem(max_ch, 0), smem(max_ch, 1), smem(tb * TOP_K, 0),
                  pl.BlockSpec(memory_space=pl.ANY),
                  pl.BlockSpec((tb, d), row), pl.BlockSpec((tb, LANES), row),
                  pl.BlockSpec((1, d), lambda i: (0, 0))],
        out_specs=pl.BlockSpec((tb, d), row),
        out_shape=jax.ShapeDtypeStruct((t, d), F32),
        scratch_shapes=[pltpu.VMEM((2, max_ch * COMBINE_CHUNK * sub, LANES), F32),
                        pltpu.VMEM((TOP_K, tb * sub, LANES), F32), pltpu.SemaphoreType.DMA((2,))],
        compiler_params=_params(("arbitrary",)),
        name="combine",
    )(src, src, loc, y, h2, tg, g)


def _route(top_e, rank, counts, n_experts, bm):
    a = top_e.size
    padded = (counts + bm - 1) // bm * bm
    pend = jnp.cumsum(padded)
    pstart = pend - padded
    hot = top_e[..., None] == jnp.arange(n_experts, dtype=I32)
    dest = (jnp.sum(jnp.where(hot, pstart, 0), axis=-1) + rank).astype(I32)
    n_blocks = -(-a // bm) + n_experts
    blk_lo = jnp.arange(n_blocks, dtype=I32) * bm
    blk_expert = jnp.minimum(jnp.sum(blk_lo[:, None] >= pend[None, :], axis=1), n_experts - 1).astype(I32)
    n_active = (pend[-1] // bm).astype(I32)
    has_pad = jnp.any((blk_lo[:, None] + bm) == pend[None, :], axis=1)
    needs_zero = (has_pad | (jnp.arange(n_blocks) >= n_active)).astype(I32)
    return dest, blk_expert, n_active.reshape(1), needs_zero, pstart.astype(I32)


def _pad_cols(w, n):
    return jnp.pad(w, ((0, 0), (0, n - w.shape[1])))


def _layer(h, mem, l, p):
    b, s, d = h.shape
    t = b * s
    lambda_init = 0.8 - 0.6 * math.exp(-0.3 * l)
    n_main = 6 * GROUP
    w_in = p['w_in']
    w_main = w_in[:, :n_main].astype(BF16)
    w_i = _pad_cols(w_in[:, n_main:n_main + MLSTM_HEADS], LANES).astype(BF16)
    w_f = _pad_cols(w_in[:, n_main + MLSTM_HEADS:], LANES).astype(BF16)
    dq, dk, dv, mqk, mv, mo, gi, gf = _in_proj(h.reshape(t, d), p['norm_mix_g'].reshape(1, d),
                                               w_main, w_i, w_f, tm=min(1024, t))
    r3 = lambda a: a.reshape(b, s, a.shape[-1])
    vec = lambda a: a.reshape(1, -1)
    od = _diff_attention(r3(dq), r3(dk), r3(dv), vec(p['lambda_q1']), vec(p['lambda_k1']),
                         vec(p['lambda_q2']), vec(p['lambda_k2']), vec(p['diff_norm_g']),
                         lambda_init, bq=min(512, s))
    hm = _mlstm(r3(mqk), r3(mv), r3(gi), r3(gf), r3(mo), p['conv_w'], vec(p['conv_b']),
                _pad_cols(vec(p['b_igate']), LANES), _pad_cols(vec(p['b_fgate']), LANES),
                vec(p['mlstm_norm_g']), bb=2 if b % 2 == 0 else 1)
    kmem, vmem = _kv_proj(mem, vec(p['norm_mem_g']), p['w_ckv'].astype(BF16))
    w_out = p['w_out'].astype(BF16)
    n_experts = p['w_router'].shape[1]
    wr = _pad_cols(p['w_router'], LANES)
    wr_hi = wr.astype(BF16)
    wr = jnp.concatenate([wr_hi, (wr - wr_hi.astype(F32)).astype(BF16)], axis=1)
    br = jnp.concatenate([vec(p['b_router']), jnp.full((1, LANES - n_experts), NEG_INF, F32)], axis=1)
    tm = min(512, s)
    h2, xn3, te, tg, cnt, cstart = _post(od, hm, h, kmem, vmem, w_out[:GROUP], w_out[GROUP:],
                                         vec(p['norm_xattn_g']), p['w_cq'].astype(BF16), p['w_co'].astype(BF16),
                                         vec(p['norm_ffn_g']), wr, br, tm=tm)
    te = te.reshape(t, LANES)
    counts = cnt[0, :n_experts].astype(I32)
    dest, blk_expert, n_active, needs_zero, pstart = _route(te[:, :TOP_K], te[:, TOP_K:2 * TOP_K],
                                                            counts, n_experts, MOE_BLOCK)
    tb_d = min(512, t)
    xs = _dispatch(needs_zero, dest.reshape(t // tb_d, 1, tb_d * TOP_K), xn3.reshape(t * (d // LANES), LANES),
                   d // LANES, MOE_BLOCK, tb_d)
    w1g, w1l = _split_w1(p['w1'], rb=256)
    y = _experts(blk_expert, n_active, xs, w1g, w1l,
                 p['b1'][:, None, 0::2], p['b1'][:, None, 1::2],
                 p['w2'].astype(BF16), p['b2'][:, None, :], MOE_BLOCK)
    src, loc, max_ch = _combine_plan(dest, te[:, :TOP_K], cstart[:, 0, :n_experts].astype(I32),
                                     counts, pstart, tm, d // LANES)
    return functools.partial(_combine, src, loc, max_ch, y, h2.reshape(t, d), tg.reshape(t, LANES), tb=tm)


def kernel(x, mem, norm_mix_g, w_in, conv_w, conv_b, b_igate, b_fgate, mlstm_norm_g, lambda_q1, lambda_k1, lambda_q2, lambda_k2, diff_norm_g, w_out, norm_xattn_g, norm_mem_g, w_cq, w_ckv, w_co, norm_ffn_g, w_router, b_router, w1, b1, w2, b2, norm_final_g):
    stacked = dict(norm_mix_g=norm_mix_g, w_in=w_in, conv_w=conv_w, conv_b=conv_b, b_igate=b_igate,
                   b_fgate=b_fgate, mlstm_norm_g=mlstm_norm_g, lambda_q1=lambda_q1, lambda_k1=lambda_k1,
                   lambda_q2=lambda_q2, lambda_k2=lambda_k2, diff_norm_g=diff_norm_g, w_out=w_out,
                   norm_xattn_g=norm_xattn_g, norm_mem_g=norm_mem_g, w_cq=w_cq, w_ckv=w_ckv, w_co=w_co,
                   norm_ffn_g=norm_ffn_g, w_router=w_router, b_router=b_router, w1=w1, b1=b1, w2=w2, b2=b2)
    depth = w_in.shape[0]
    b, s, d = x.shape
    h = x
    for l in range(depth):
        p = {k: v[l] for k, v in stacked.items()}
        combine = _layer(h, mem, l, p)
        h = combine(norm_final_g.reshape(1, d), final_norm=l == depth - 1).reshape(b, s, d)
    return h
```

```python
import functools
import math

import jax
import jax.numpy as jnp
from jax import lax
from jax.experimental import pallas as pl
from jax.experimental.pallas import tpu as pltpu

F32 = jnp.float32
BF16 = jnp.bfloat16
I32 = jnp.int32

RMS_EPS = 1e-5
LANES = 128
VMEM_LIMIT = 56 * 1024 * 1024

DIFF_HEADS = 4
DIFF_QK_DIM = 64
MLSTM_HEADS = 4
MLSTM_QK_DIM = 64
CONV_K = 4
CHUNK = 128
XATTN_HEADS = 4
TOP_K = 4
SWIGLU_ALPHA = 1.702
SWIGLU_LIMIT = 7.0
MOE_BLOCK = 1024
GROUP = 512
NEG_INF = float("-inf")


def _rms(x, g):
    return x * lax.rsqrt(jnp.mean(x * x, axis=-1, keepdims=True) + RMS_EPS) * g


def _store_row_tiles(ref, m):
    rows, d = m.shape
    sub = d // LANES
    for c in range(sub):
        ref[pl.ds(c, rows, stride=sub), :] = m[:, c * LANES:(c + 1) * LANES]


def _load_row_tiles(ref, sub):
    rows = ref.shape[0] // sub
    return jnp.concatenate([ref[pl.ds(c, rows, stride=sub), :] for c in range(sub)], axis=1)


def _params(sem, vmem=VMEM_LIMIT, flags=None):
    return pltpu.CompilerParams(dimension_semantics=sem, vmem_limit_bytes=vmem, flags=flags)


def _inproj_kernel(x_ref, g_ref, w_ref, wi_ref, wf_ref,
                   dq_ref, dk_ref, dv_ref, mqk_ref, mv_ref, mo_ref, gi_ref, gf_ref):
    xb = _rms(x_ref[...], g_ref[...]).astype(BF16)
    for n, o_ref in enumerate((dq_ref, dk_ref, dv_ref, mqk_ref, mv_ref, mo_ref)):
        o_ref[...] = jnp.dot(xb, w_ref[:, n * GROUP:(n + 1) * GROUP],
                             preferred_element_type=F32).astype(o_ref.dtype)
    gi_ref[...] = jnp.dot(xb, wi_ref[...], preferred_element_type=F32)
    gf_ref[...] = jnp.dot(xb, wf_ref[...], preferred_element_type=F32)


def _in_proj(x2, g, w_main, w_i, w_f, tm):
    t, d = x2.shape
    row = lambda i: (i, 0)
    const = lambda i: (0, 0)
    out_dtypes = (BF16, BF16, BF16, F32, BF16, F32)
    return pl.pallas_call(
        _inproj_kernel,
        grid=(t // tm,),
        in_specs=[pl.BlockSpec((tm, d), row), pl.BlockSpec((1, d), const),
                  pl.BlockSpec(w_main.shape, const), pl.BlockSpec(w_i.shape, const),
                  pl.BlockSpec(w_f.shape, const)],
        out_specs=[pl.BlockSpec((tm, GROUP), row)] * 6 + [pl.BlockSpec((tm, LANES), row)] * 2,
        out_shape=[jax.ShapeDtypeStruct((t, GROUP), dt) for dt in out_dtypes]
        + [jax.ShapeDtypeStruct((t, LANES), F32)] * 2,
        compiler_params=_params(("parallel",)),
        name="in_proj",
    )(x2, g, w_main, w_i, w_f)


def _diffattn_kernel(q_ref, k_ref, v_ref, lq1_ref, lk1_ref, lq2_ref, lk2_ref, g_ref, o_ref,
                     *, bq, lambda_init):
    s_len = q_ref.shape[1]
    lane = lax.broadcasted_iota(I32, (1, LANES), 1)
    lo = lane < DIFF_QK_DIM
    row = lax.broadcasted_iota(I32, (bq, bq), 0)
    col = lax.broadcasted_iota(I32, (bq, bq), 1)
    causal = col <= row
    lam = (jnp.exp(jnp.sum(lq1_ref[...] * lk1_ref[...], axis=-1, keepdims=True))
           - jnp.exp(jnp.sum(lq2_ref[...] * lk2_ref[...], axis=-1, keepdims=True)) + lambda_init)

    def scores(qm, i, j):
        kj = k_ref[0, j * bq:(j + 1) * bq, :]
        s = lax.dot_general(qm, kj, (((1,), (1,)), ((), ())), preferred_element_type=F32)
        return jnp.where(causal, s, -jnp.inf) if j == i else s

    def lane_fold(x, op):
        out = x[:, :LANES]
        for c in range(1, bq // LANES):
            out = op(out, x[:, c * LANES:(c + 1) * LANES])
        return out

    def softmax_av(qm, i):
        mx = lane_fold(scores(qm, i, 0), jnp.maximum)
        for j in range(1, i + 1):
            mx = jnp.maximum(mx, lane_fold(scores(qm, i, j), jnp.maximum))
        m = jnp.max(mx, axis=-1, keepdims=True)
        lsum, acc = None, None
        for j in range(i + 1):
            p = jnp.exp(scores(qm, i, j) - m)
            pv = jnp.dot(p.astype(BF16), v_ref[0, j * bq:(j + 1) * bq, :], preferred_element_type=F32)
            ps = lane_fold(p, jnp.add)
            lsum, acc = (ps, pv) if j == 0 else (lsum + ps, acc + pv)
        return acc / jnp.sum(lsum, axis=-1, keepdims=True)

    for i in range(s_len // bq):
        q = q_ref[0, i * bq:(i + 1) * bq, :] * jnp.asarray(DIFF_QK_DIM ** -0.5, BF16)
        zero = jnp.zeros_like(q)
        od = softmax_av(jnp.where(lo, q, zero), i) - lam * softmax_av(jnp.where(lo, zero, q), i)
        o_ref[0, i * bq:(i + 1) * bq, :] = (_rms(od, g_ref[...]) * (1.0 - lambda_init)).astype(o_ref.dtype)


def _diff_attention(dq, dk, dv, lq1, lk1, lq2, lk2, g, lambda_init, bq):
    b, s, w = dq.shape
    blk = lambda bi, h: (bi, 0, h)
    vec = lambda bi, h: (0, 0)
    return pl.pallas_call(
        functools.partial(_diffattn_kernel, bq=bq, lambda_init=lambda_init),
        grid=(b, DIFF_HEADS),
        in_specs=[pl.BlockSpec((1, s, LANES), blk)] * 3
        + [pl.BlockSpec((1, DIFF_QK_DIM), vec)] * 4
        + [pl.BlockSpec((1, LANES), lambda bi, h: (0, h))],
        out_specs=pl.BlockSpec((1, s, LANES), blk),
        out_shape=jax.ShapeDtypeStruct((b, s, w), BF16),
        compiler_params=_params(("parallel", "parallel")),
        name="diff_attn",
    )(dq, dk, dv, lq1, lk1, lq2, lk2, g)


def _log_sigmoid(x):
    return -(jnp.maximum(-x, 0.0) + jnp.log1p(jnp.exp(-jnp.abs(x))))


def _mlstm_kernel(mqk_ref, mv_ref, gi_ref, gf_ref, mo_ref, cw_ref, cb_ref, bi_ref, bf_ref, ng_ref,
                  o_ref, conv_scr, c_scr, m_scr, *, bb):
    c = pl.program_id(1)
    L = CHUNK
    dk = MLSTM_QK_DIM
    nq = MLSTM_HEADS * dk
    tail = 8

    @pl.when(c == 0)
    def _():
        conv_scr[:, 0:tail, :] = jnp.zeros((bb, tail, 2 * nq), F32)
        c_scr[...] = jnp.zeros_like(c_scr)
        m_scr[...] = jnp.zeros_like(m_scr)

    row = lax.broadcasted_iota(I32, (L, L), 0)
    col = lax.broadcasted_iota(I32, (L, L), 1)
    causal = col <= row
    tril = causal.astype(F32)
    lane = lax.broadcasted_iota(I32, (1, LANES), 1)
    sub = lax.broadcasted_iota(I32, (LANES, 1), 0)
    ones_col = jnp.where(lane == 0, 1.0, 0.0).astype(BF16) * jnp.ones((L, 1), BF16)

    for b in range(bb):
        conv_scr[b, tail:tail + L, :] = mqk_ref[b]
        y = cb_ref[...]
        for j in range(CONV_K):
            y = y + conv_scr[b, pl.ds(tail - (CONV_K - 1) + j, L), :] * cw_ref[j:j + 1, :]
        conv_scr[b, 0:tail, :] = conv_scr[b, L:L + tail, :]
        qk = y * jax.nn.sigmoid(y)

        ig = gi_ref[b] + bi_ref[...]
        lf = _log_sigmoid(gf_ref[b] + bf_ref[...])
        bcum = jnp.dot(tril, lf, preferred_element_type=F32, precision=lax.Precision.HIGHEST)
        a_t = (ig - bcum).T

        for pair in range(MLSTM_HEADS // 2):
            q_pair = qk[:, pair * LANES:(pair + 1) * LANES] * (dk ** -0.5)
            k_pair = qk[:, nq + pair * LANES:nq + (pair + 1) * LANES]
            kt_pair = k_pair.T
            c_old = c_scr[b, pair]
            c_bf = c_old.astype(BF16)
            upd = jnp.zeros_like(c_old)
            decays = []
            for hh in range(2):
                h = 2 * pair + hh
                sel = (lane >= hh * dk) & (lane < (hh + 1) * dk)
                qm = jnp.where(sel, q_pair, 0.0).astype(BF16)
                km = jnp.where(sel, k_pair, 0.0).astype(BF16)
                selr = (sub >= hh * dk) & (sub < (hh + 1) * dk)
                ktm = jnp.where(selr, kt_pair, 0.0).astype(BF16)
                v_h = mv_ref[b, :, h * LANES:(h + 1) * LANES]
                v_aug = jnp.concatenate([v_h, ones_col], axis=1)

                m_st = m_scr[b, h, 0:1, 0:1]
                bc = bcum[:, h:h + 1]
                ic = ig[:, h:h + 1]
                a_m = jnp.where(causal, a_t[h:h + 1, :], -jnp.inf)
                inter = bc + m_st
                m_t = jnp.maximum(inter, bc + jnp.max(a_m, axis=-1, keepdims=True))
                qkt = lax.dot_general(qm, km, (((1,), (1,)), ((), ())), preferred_element_type=F32)
                w = qkt * jnp.exp(a_m + (bc - m_t))
                g = jnp.exp(inter - m_t)
                num_aug = (g * jnp.dot(qm, c_bf, preferred_element_type=F32)
                           + jnp.dot(w.astype(BF16), v_aug, preferred_element_type=F32))
                num = num_aug[:, :LANES]
                den = num_aug[:, LANES:LANES + 1]
                h_t = num / jnp.maximum(jnp.abs(den), jnp.exp(-m_t))

                b_last = bc[L - 1:L, :]
                gs = b_last - bc + ic
                m_new = jnp.maximum(b_last + m_st, jnp.max(gs, axis=0, keepdims=True))
                decays.append(jnp.exp(b_last + m_st - m_new))
                ws = jnp.exp(gs - m_new)
                wsv = (ws * v_aug.astype(F32)).astype(BF16)
                upd = upd + jnp.dot(ktm, wsv, preferred_element_type=F32)
                m_scr[b, h, 0:1, :] = jnp.broadcast_to(m_new, (1, LANES))

                hn = _rms(h_t, ng_ref[:, h * LANES:(h + 1) * LANES])
                gate = jax.nn.sigmoid(mo_ref[b, :, h * LANES:(h + 1) * LANES])
                o_ref[b, :, h * LANES:(h + 1) * LANES] = (hn * gate).astype(o_ref.dtype)
            d_rows = jnp.where(sub < dk, decays[0], decays[1])
            c_scr[b, pair] = d_rows * c_old + upd


def _mlstm(mqk, mv, gi, gf, mo, conv_w, conv_b, b_i, b_f, norm_g, bb):
    b, s, w = mqk.shape
    blk = lambda bi, c: (bi, c, 0)
    const = lambda bi, c: (0, 0)
    return pl.pallas_call(
        functools.partial(_mlstm_kernel, bb=bb),
        grid=(b // bb, s // CHUNK),
        in_specs=[pl.BlockSpec((bb, CHUNK, w), blk), pl.BlockSpec((bb, CHUNK, w), blk),
                  pl.BlockSpec((bb, CHUNK, LANES), blk), pl.BlockSpec((bb, CHUNK, LANES), blk),
                  pl.BlockSpec((bb, CHUNK, w), blk),
                  pl.BlockSpec(conv_w.shape, const), pl.BlockSpec(conv_b.shape, const),
                  pl.BlockSpec(b_i.shape, const), pl.BlockSpec(b_f.shape, const),
                  pl.BlockSpec(norm_g.shape, const)],
        out_specs=pl.BlockSpec((bb, CHUNK, w), blk),
        out_shape=jax.ShapeDtypeStruct((b, s, w), BF16),
        scratch_shapes=[pltpu.VMEM((bb, CHUNK + 8, w), F32),
                        pltpu.VMEM((bb, MLSTM_HEADS // 2, LANES, 2 * LANES), F32),
                        pltpu.VMEM((bb, MLSTM_HEADS, 8, LANES), F32)],
        compiler_params=_params(("parallel", "arbitrary")),
        name="mlstm",
    )(mqk, mv, gi, gf, mo, conv_w, conv_b, b_i, b_f, norm_g)


def _kvproj_kernel(mem_ref, g_ref, w_ref, k_ref, v_ref):
    d = mem_ref.shape[-1]
    mb = _rms(mem_ref[0], g_ref[...]).astype(BF16)
    k_ref[0] = jnp.dot(mb, w_ref[:, :d], preferred_element_type=F32).astype(k_ref.dtype)
    v_ref[0] = jnp.dot(mb, w_ref[:, d:], preferred_element_type=F32).astype(v_ref.dtype)


def _kv_proj(mem, g, w_ckv):
    b, m, d = mem.shape
    blk = lambda bi: (bi, 0, 0)
    const = lambda bi: (0, 0)
    return pl.pallas_call(
        _kvproj_kernel,
        grid=(b,),
        in_specs=[pl.BlockSpec((1, m, d), blk), pl.BlockSpec((1, d), const),
                  pl.BlockSpec(w_ckv.shape, const)],
        out_specs=[pl.BlockSpec((1, m, d), blk)] * 2,
        out_shape=[jax.ShapeDtypeStruct((b, m, d), BF16)] * 2,
        compiler_params=_params(("parallel",)),
        name="kv_proj",
    )(mem, g, w_ckv)


def _post_kernel(od_ref, hm_ref, x_ref, k_ref, v_ref, wo1_ref, wo2_ref, g2_ref, wcq_ref, wco_ref,
                 g3_ref, wr_ref, br_ref, h2_ref, xn3_ref, te_ref, tg_ref, cnt_ref, cstart_ref, base_scr):
    d = x_ref.shape[-1]
    hd = d // XATTN_HEADS
    h1 = (x_ref[0] + jnp.dot(od_ref[0], wo1_ref[...], preferred_element_type=F32)
          + jnp.dot(hm_ref[0], wo2_ref[...], preferred_element_type=F32))
    q = jnp.dot(_rms(h1, g2_ref[...]).astype(BF16), wcq_ref[...], preferred_element_type=F32)
    q = (q * (hd ** -0.5)).astype(BF16)
    heads = []
    for h in range(XATTN_HEADS):
        sl = slice(h * hd, (h + 1) * hd)
        s = lax.dot_general(q[:, sl], k_ref[0, :, sl], (((1,), (1,)), ((), ())),
                            preferred_element_type=F32)
        e = jnp.exp(s - jnp.max(s, axis=-1, keepdims=True))
        p = e / jnp.sum(e, axis=-1, keepdims=True)
        heads.append(jnp.dot(p.astype(BF16), v_ref[0, :, sl], preferred_element_type=F32))
    o = jnp.concatenate(heads, axis=1).astype(BF16)
    h2 = h1 + jnp.dot(o, wco_ref[...], preferred_element_type=F32)
    h2_ref[0] = h2
    xn3 = _rms(h2, g3_ref[...])
    _store_row_tiles(xn3_ref.at[0], xn3)
    x_hi = xn3.astype(BF16)
    x_lo = (xn3 - x_hi.astype(F32)).astype(BF16)
    hh_hl = jnp.dot(x_hi, wr_ref[...], preferred_element_type=F32)
    lh = jnp.dot(x_lo, wr_ref[:, :LANES], preferred_element_type=F32)
    logits = hh_hl[:, :LANES] + (hh_hl[:, LANES:] + lh) + br_ref[...]
    lane = lax.broadcasted_iota(I32, logits.shape, 1)
    lane_f = lane.astype(F32)
    cur = logits
    te = jnp.zeros(logits.shape, F32)
    chosen = jnp.zeros(logits.shape, F32)
    vals, hits = [], []
    for k in range(TOP_K):
        m = jnp.max(cur, axis=-1, keepdims=True)
        idx = jnp.min(jnp.where(cur == m, lane_f, float(LANES)), axis=-1, keepdims=True)
        hit = lane_f == idx
        vals.append(m)
        hits.append(hit)
        te = jnp.where(lane == k, idx, te)
        chosen = jnp.where(hit, 1.0, chosen)
        cur = jnp.where(hit, NEG_INF, cur)
    es = [jnp.exp(v - vals[0]) for v in vals]
    tot = es[0] + es[1] + es[2] + es[3]
    tg = jnp.zeros(logits.shape, F32)
    for k in range(TOP_K):
        tg = jnp.where(lane == k, es[k] / tot, tg)
    tg_ref[0] = tg

    @pl.when((pl.program_id(0) == 0) & (pl.program_id(1) == 0))
    def _():
        base_scr[...] = jnp.zeros_like(base_scr)
    tm = logits.shape[0]
    earlier = (lax.broadcasted_iota(I32, (tm, tm), 1) < lax.broadcasted_iota(I32, (tm, tm), 0)).astype(BF16)
    prior = jnp.dot(earlier, chosen.astype(BF16), preferred_element_type=F32) + base_scr[...]
    for k in range(TOP_K):
        rank = jnp.sum(jnp.where(hits[k], prior, 0.0), axis=-1, keepdims=True)
        te = jnp.where(lane == TOP_K + k, rank, te)
    te_ref[0] = te.astype(I32)
    cstart_ref[0] = jnp.broadcast_to(base_scr[...], cstart_ref.shape[1:])
    base_scr[...] = base_scr[...] + jnp.sum(chosen, axis=0, keepdims=True)
    cnt_ref[...] = base_scr[...]


def _post(od, hm, x, kmem, vmem, wo1, wo2, g2, wcq, wco, g3, wr, br, tm):
    b, s, d = x.shape
    w = od.shape[-1]
    m = kmem.shape[1]
    blk = lambda bi, i: (bi, i, 0)
    mem = lambda bi, i: (bi, 0, 0)
    const = lambda bi, i: (0, 0)
    full = lambda a: pl.BlockSpec(a.shape, const)
    return pl.pallas_call(
        _post_kernel,
        grid=(b, s // tm),
        in_specs=[pl.BlockSpec((1, tm, w), blk), pl.BlockSpec((1, tm, w), blk),
                  pl.BlockSpec((1, tm, d), blk), pl.BlockSpec((1, m, d), mem),
                  pl.BlockSpec((1, m, d), mem), full(wo1), full(wo2), full(g2), full(wcq),
                  full(wco), full(g3), full(wr), full(br)],
        out_specs=[pl.BlockSpec((1, tm, d), blk),
                   pl.BlockSpec((1, tm * (d // LANES), LANES), blk),
                   pl.BlockSpec((1, tm, LANES), blk), pl.BlockSpec((1, tm, LANES), blk),
                   pl.BlockSpec((1, LANES), const),
                   pl.BlockSpec((1, 8, LANES), lambda bi, i: (bi * (s // tm) + i, 0, 0))],
        out_shape=[jax.ShapeDtypeStruct((b, s, d), F32), jax.ShapeDtypeStruct((b, s * (d // LANES), LANES), F32),
                   jax.ShapeDtypeStruct((b, s, LANES), I32), jax.ShapeDtypeStruct((b, s, LANES), F32),
                   jax.ShapeDtypeStruct((1, LANES), F32),
                   jax.ShapeDtypeStruct((b * (s // tm), 8, LANES), F32)],
        scratch_shapes=[pltpu.VMEM((1, LANES), F32)],
        compiler_params=_params(("arbitrary", "arbitrary")),
        name="post_mixer",
    )(od, hm, x, kmem, vmem, wo1, wo2, g2, wcq, wco, g3, wr, br)


def _split_kernel(w_ref, p_ref, wg_ref, wl_ref):
    wb = w_ref[0].astype(BF16)
    two = 2 * LANES
    for g in range(wb.shape[1] // two):
        r = jnp.dot(wb[:, g * two:(g + 1) * two], p_ref[...], preferred_element_type=F32)
        wg_ref[0, :, g * LANES:(g + 1) * LANES] = r[:, :LANES].astype(BF16)
        wl_ref[0, :, g * LANES:(g + 1) * LANES] = r[:, LANES:].astype(BF16)


def _split_w1(w1, rb):
    e, d, de2 = w1.shape
    two = 2 * LANES
    src = jnp.arange(two)[:, None]
    dst = jnp.arange(two)[None, :]
    perm = jnp.where(dst < LANES, src == 2 * dst, src == 2 * (dst - LANES) + 1).astype(BF16)
    blk = lambda ei, r: (ei, r, 0)
    return pl.pallas_call(
        _split_kernel,
        grid=(e, d // rb),
        in_specs=[pl.BlockSpec((1, rb, de2), blk), pl.BlockSpec((two, two), lambda ei, r: (0, 0))],
        out_specs=[pl.BlockSpec((1, rb, de2 // 2), blk)] * 2,
        out_shape=[jax.ShapeDtypeStruct((e, d, de2 // 2), BF16)] * 2,
        compiler_params=_params(("parallel", "parallel")),
        name="split_w1",
    )(w1, perm)


def _dispatch_kernel(nz_ref, dest_ref, x_ref, xs_hbm, zbuf, sem, zsem, *, tb, bm, n_blocks, sub):
    i = pl.program_id(0)
    tile = lambda r: pl.ds(pl.multiple_of(r * sub, sub), sub)

    @pl.when(i == 0)
    def _():
        zbuf[...] = jnp.zeros_like(zbuf)

        def zstart(blk, carry):
            @pl.when(nz_ref[blk] != 0)
            def _():
                pltpu.make_async_copy(zbuf, xs_hbm.at[pl.ds(pl.multiple_of(blk * bm * sub, bm * sub), bm * sub)],
                                      zsem).start()
            return carry
        lax.fori_loop(0, n_blocks, zstart, 0)

        def zwait(blk, carry):
            @pl.when(nz_ref[blk] != 0)
            def _():
                pltpu.make_async_copy(zbuf, xs_hbm.at[pl.ds(0, bm * sub)], zsem).wait()
            return carry
        lax.fori_loop(0, n_blocks, zwait, 0)

    def issue(t, carry):
        for k in range(TOP_K):
            dst = dest_ref[0, 0, t * TOP_K + k]
            pltpu.make_async_copy(x_ref.at[tile(t)], xs_hbm.at[tile(dst)], sem).start(priority=k % 2)
        return carry
    lax.fori_loop(0, tb, issue, 0)
    for k in range(TOP_K):
        pltpu.make_async_copy(x_ref, xs_hbm.at[pl.ds(0, tb * sub)], sem).wait()


def _dispatch(needs_zero, dest_blocks, xn3, sub, bm, tb):
    t = xn3.shape[0] // sub
    n_blocks = needs_zero.shape[0]
    grid_spec = pltpu.PrefetchScalarGridSpec(
        num_scalar_prefetch=1,
        grid=(t // tb,),
        in_specs=[pl.BlockSpec((1, 1, tb * TOP_K), lambda i, nz: (i, 0, 0), memory_space=pltpu.SMEM),
                  pl.BlockSpec((tb * sub, LANES), lambda i, nz: (i, 0))],
        out_specs=pl.BlockSpec(memory_space=pl.ANY),
        scratch_shapes=[pltpu.VMEM((bm * sub, LANES), F32), pltpu.SemaphoreType.DMA, pltpu.SemaphoreType.DMA],
    )
    return pl.pallas_call(
        functools.partial(_dispatch_kernel, tb=tb, bm=bm, n_blocks=n_blocks, sub=sub),
        grid_spec=grid_spec,
        out_shape=jax.ShapeDtypeStruct((n_blocks * bm * sub, LANES), F32),
        compiler_params=_params(("arbitrary",)),
        name="dispatch",
    )(needs_zero, dest_blocks, xn3)


def _expert_kernel(be_ref, na_ref, x_ref, w1g_ref, w1l_ref, b1g_ref, b1l_ref, w2_ref, b2_ref, y_ref):
    i = pl.program_id(0)

    @pl.when(i < na_ref[0])
    def _():
        xb = _load_row_tiles(x_ref, w1g_ref.shape[1] // LANES).astype(BF16)
        glu = jnp.dot(xb, w1g_ref[0], preferred_element_type=F32) + b1g_ref[0]
        lin = jnp.dot(xb, w1l_ref[0], preferred_element_type=F32) + b1l_ref[0]
        glu = jnp.minimum(glu, SWIGLU_LIMIT)
        lin = jnp.clip(lin, -SWIGLU_LIMIT, SWIGLU_LIMIT)
        act = glu * jax.nn.sigmoid(SWIGLU_ALPHA * glu) * (lin + 1.0)
        _store_row_tiles(y_ref, jnp.dot(act.astype(BF16), w2_ref[0], preferred_element_type=F32) + b2_ref[0])

    @pl.when(i >= na_ref[0])
    def _():
        y_ref[...] = jnp.zeros_like(y_ref)


def _experts(blk_expert, n_active, xs, w1g, w1l, b1g, b1l, w2, b2, bm):
    d = w1g.shape[1]
    sub = d // LANES
    n_pad = xs.shape[0] // sub
    n_blocks = n_pad // bm
    de = w1g.shape[-1]
    last = lambda i, na: jnp.minimum(i, na[0] - 1)
    wmap = lambda i, be, na: (be[last(i, na)], 0, 0)
    grid_spec = pltpu.PrefetchScalarGridSpec(
        num_scalar_prefetch=2,
        grid=(n_blocks,),
        in_specs=[pl.BlockSpec((bm * sub, LANES), lambda i, be, na: (last(i, na), 0)),
                  pl.BlockSpec((1, d, de), wmap), pl.BlockSpec((1, d, de), wmap),
                  pl.BlockSpec((1, 1, de), wmap), pl.BlockSpec((1, 1, de), wmap),
                  pl.BlockSpec((1, de, d), wmap), pl.BlockSpec((1, 1, d), wmap)],
        out_specs=pl.BlockSpec((bm * sub, LANES), lambda i, be, na: (i, 0)),
    )
    return pl.pallas_call(
        _expert_kernel,
        grid_spec=grid_spec,
        out_shape=jax.ShapeDtypeStruct((n_pad * sub, LANES), F32),
        compiler_params=_params(("arbitrary",)),
        name="experts",
    )(blk_expert, n_active, xs, w1g, w1l, b1g, b1l, w2, b2)


COMBINE_CHUNK = 8


def _combine_plan(dest, top_e, cstart, counts, pstart, tb, sub):
    ch = COMBINE_CHUNK
    nb, n_experts = cstart.shape
    t = dest.shape[0]
    cnt = jnp.concatenate([cstart[1:], counts[None, :]], axis=0) - cstart
    run_lo = pstart[None, :] + cstart
    lo = run_lo // ch * ch
    nch = jnp.where(cnt > 0, (run_lo + cnt - lo + ch - 1) // ch, 0)
    cum = jnp.cumsum(nch, axis=1)
    first = cum - nch
    max_ch = tb * TOP_K // ch + 2 * n_experts
    c = jnp.arange(max_ch, dtype=I32)
    e_of_c = jnp.minimum(jnp.sum(c[None, :, None] >= cum[:, None, :], axis=-1), n_experts - 1)
    experts = jnp.arange(n_experts, dtype=I32)
    pick_c = lambda tab: jnp.sum(jnp.where(e_of_c[:, :, None] == experts, tab[:, None, :], 0), axis=-1)
    src = jnp.where(c[None, :] < cum[:, -1:], pick_c(lo - first * ch) + c[None, :] * ch, 0).astype(I32)
    shift = (first * ch - lo)[:, None, None, :]
    hot = top_e.reshape(nb, tb, TOP_K)[..., None] == experts
    loc = (dest.reshape(nb, tb, TOP_K) + jnp.sum(jnp.where(hot, shift, 0), axis=-1)).astype(I32)
    return (src * sub).reshape(nb, 1, max_ch), (loc * sub).reshape(nb, 1, tb * TOP_K), max_ch


def _combine_kernel(src_ref, snext_ref, loc_ref, y_hbm, h2_ref, tg_ref, g_ref, o_ref,
                    buf, rows_scr, sem, *, tb, final_norm):
    i = pl.program_id(0)
    sub = h2_ref.shape[1] // LANES
    n_chunks = src_ref.shape[-1]
    rows = COMBINE_CHUNK * sub
    slot = i % 2

    def fetch(s_ref, s):
        def issue(c, carry):
            lo = pl.multiple_of(s_ref[0, 0, c], sub)
            pltpu.make_async_copy(y_hbm.at[pl.ds(lo, rows)], buf.at[s, pl.ds(pl.multiple_of(c * rows, rows), rows)],
                                  sem.at[s]).start(priority=1)
            return carry
        lax.fori_loop(0, n_chunks, issue, 0, unroll=4)

    @pl.when(i == 0)
    def _():
        fetch(src_ref, 0)

    @pl.when(i + 1 < pl.num_programs(0))
    def _():
        fetch(snext_ref, 1 - slot)

    pltpu.make_async_copy(y_hbm.at[pl.ds(0, n_chunks * rows)], buf.at[slot], sem.at[slot]).wait()

    def token(t, carry):
        dst = pl.ds(pl.multiple_of(t * sub, sub), sub)
        for k in range(TOP_K):
            r = pl.multiple_of(loc_ref[0, 0, t * TOP_K + k], sub)
            rows_scr[k, dst, :] = buf[slot, pl.ds(r, sub), :]
        return carry
    lax.fori_loop(0, tb, token, 0, unroll=4)

    acc = h2_ref[...]
    for k in range(TOP_K):
        acc = acc + _load_row_tiles(rows_scr.at[k], sub) * tg_ref[:, k:k + 1]
    o_ref[...] = _rms(acc, g_ref[...]) if final_norm else acc


def _combine(src, loc, max_ch, y, h2, tg, g, tb, final_norm):
    t, d = h2.shape
    sub = d // LANES
    nb = t // tb
    row = lambda i: (i, 0)
    smem = lambda n, shift: pl.BlockSpec((1, 1, n), lambda i: (jnp.minimum(i + shift, nb - 1), 0, 0),
                                         memory_space=pltpu.SMEM)
    return pl.pallas_call(
        functools.partial(_combine_kernel, tb=tb, final_norm=final_norm),
        grid=(nb,),
        in_specs=[smem(max_ch, 0), smem(max_ch, 1), smem(tb * TOP_K, 0),
                  pl.BlockSpec(memory_space=pl.ANY),
                  pl.BlockSpec((tb, d), row), pl.BlockSpec((tb, LANES), row),
                  pl.BlockSpec((1, d), lambda i: (0, 0))],
        out_specs=pl.BlockSpec((tb, d), row),
        out_shape=jax.ShapeDtypeStruct((t, d), F32),
        scratch_shapes=[pltpu.VMEM((2, max_ch * COMBINE_CHUNK * sub, LANES), F32),
                        pltpu.VMEM((TOP_K, tb * sub, LANES), F32), pltpu.SemaphoreType.DMA((2,))],
        compiler_params=_params(("arbitrary",)),
        name="combine",
    )(src, src, loc, y, h2, tg, g)


def _route(top_e, rank, counts, n_experts, bm):
    a = top_e.size
    padded = (counts + bm - 1) // bm * bm
    pend = jnp.cumsum(padded)
    pstart = pend - padded
    hot = top_e[..., None] == jnp.arange(n_experts, dtype=I32)
    dest = (jnp.sum(jnp.where(hot, pstart, 0), axis=-1) + rank).astype(I32)
    n_blocks = -(-a // bm) + n_experts
    blk_lo = jnp.arange(n_blocks, dtype=I32) * bm
    blk_expert = jnp.minimum(jnp.sum(blk_lo[:, None] >= pend[None, :], axis=1), n_experts - 1).astype(I32)
    n_active = (pend[-1] // bm).astype(I32)
    has_pad = jnp.any((blk_lo[:, None] + bm) == pend[None, :], axis=1)
    needs_zero = (has_pad | (jnp.arange(n_blocks) >= n_active)).astype(I32)
    return dest, blk_expert, n_active.reshape(1), needs_zero, pstart.astype(I32)


def _pad_cols(w, n):
    return jnp.pad(w, ((0, 0), (0, n - w.shape[1])))


def _layer(h, mem, l, p):
    b, s, d = h.shape
    t = b * s
    lambda_init = 0.8 - 0.6 * math.exp(-0.3 * l)
    n_main = 6 * GROUP
    w_in = p['w_in']
    w_main = w_in[:, :n_main].astype(BF16)
    w_i = _pad_cols(w_in[:, n_main:n_main + MLSTM_HEADS], LANES).astype(BF16)
    w_f = _pad_cols(w_in[:, n_main + MLSTM_HEADS:], LANES).astype(BF16)
    dq, dk, dv, mqk, mv, mo, gi, gf = _in_proj(h.reshape(t, d), p['norm_mix_g'].reshape(1, d),
                                               w_main, w_i, w_f, tm=min(1024, t))
    r3 = lambda a: a.reshape(b, s, a.shape[-1])
    vec = lambda a: a.reshape(1, -1)
    od = _diff_attention(r3(dq), r3(dk), r3(dv), vec(p['lambda_q1']), vec(p['lambda_k1']),
                         vec(p['lambda_q2']), vec(p['lambda_k2']), vec(p['diff_norm_g']),
                         lambda_init, bq=min(512, s))
    hm = _mlstm(r3(mqk), r3(mv), r3(gi), r3(gf), r3(mo), p['conv_w'], vec(p['conv_b']),
                _pad_cols(vec(p['b_igate']), LANES), _pad_cols(vec(p['b_fgate']), LANES),
                vec(p['mlstm_norm_g']), bb=2 if b % 2 == 0 else 1)
    kmem, vmem = _kv_proj(mem, vec(p['norm_mem_g']), p['w_ckv'].astype(BF16))
    w_out = p['w_out'].astype(BF16)
    n_experts = p['w_router'].shape[1]
    wr = _pad_cols(p['w_router'], LANES)
    wr_hi = wr.astype(BF16)
    wr = jnp.concatenate([wr_hi, (wr - wr_hi.astype(F32)).astype(BF16)], axis=1)
    br = jnp.concatenate([vec(p['b_router']), jnp.full((1, LANES - n_experts), NEG_INF, F32)], axis=1)
    tm = min(512, s)
    h2, xn3, te, tg, cnt, cstart = _post(od, hm, h, kmem, vmem, w_out[:GROUP], w_out[GROUP:],
                                         vec(p['norm_xattn_g']), p['w_cq'].astype(BF16), p['w_co'].astype(BF16),
                                         vec(p['norm_ffn_g']), wr, br, tm=tm)
    te = te.reshape(t, LANES)
    counts = cnt[0, :n_experts].astype(I32)
    dest, blk_expert, n_active, needs_zero, pstart = _route(te[:, :TOP_K], te[:, TOP_K:2 * TOP_K],
                                                            counts, n_experts, MOE_BLOCK)
    tb_d = min(512, t)
    xs = _dispatch(needs_zero, dest.reshape(t // tb_d, 1, tb_d * TOP_K), xn3.reshape(t * (d // LANES), LANES),
                   d // LANES, MOE_BLOCK, tb_d)
    w1g, w1l = _split_w1(p['w1'], rb=256)
    y = _experts(blk_expert, n_active, xs, w1g, w1l,
                 p['b1'][:, None, 0::2], p['b1'][:, None, 1::2],
                 p['w2'].astype(BF16), p['b2'][:, None, :], MOE_BLOCK)
    src, loc, max_ch = _combine_plan(dest, te[:, :TOP_K], cstart[:, 0, :n_experts].astype(I32),
                                     counts, pstart, tm, d // LANES)
    return functools.partial(_combine, src, loc, max_ch, y, h2.reshape(t, d), tg.reshape(t, LANES), tb=tm)


def kernel(x, mem, norm_mix_g, w_in, conv_w, conv_b, b_igate, b_fgate, mlstm_norm_g, lambda_q1, lambda_k1, lambda_q2, lambda_k2, diff_norm_g, w_out, norm_xattn_g, norm_mem_g, w_cq, w_ckv, w_co, norm_ffn_g, w_router, b_router, w1, b1, w2, b2, norm_final_g):
    stacked = dict(norm_mix_g=norm_mix_g, w_in=w_in, conv_w=conv_w, conv_b=conv_b, b_igate=b_igate,
                   b_fgate=b_fgate, mlstm_norm_g=mlstm_norm_g, lambda_q1=lambda_q1, lambda_k1=lambda_k1,
                   lambda_q2=lambda_q2, lambda_k2=lambda_k2, diff_norm_g=diff_norm_g, w_out=w_out,
                   norm_xattn_g=norm_xattn_g, norm_mem_g=norm_mem_g, w_cq=w_cq, w_ckv=w_ckv, w_co=w_co,
                   norm_ffn_g=norm_ffn_g, w_router=w_router, b_router=b_router, w1=w1, b1=b1, w2=w2, b2=b2)
    depth = w_in.shape[0]
    b, s, d = x.shape
    h = x
    for l in range(depth):
        p = {k: v[l] for k, v in stacked.items()}
        combine = _layer(h, mem, l, p)
        h = combine(norm_final_g.reshape(1, d), final_norm=l == depth - 1).reshape(b, s, d)
    return h
```
